```python
import math
import jax
import jax.numpy as jnp
from jax import lax
import numpy as np

D_MODEL = 1024
BATCH = 8
SEQ = 4096
DEPTH = 4

CTX_LEN = 256
GRID_W = 64
N_MOD = 6
CONV_W = 512
CONV_K = 3
SSM_W = 512
SSM_GROUP = 16
SSM_GROUPS = SSM_W // SSM_GROUP
SSM_STATE = 64
NA_HEADS = 8
NA_HEAD_DIM = 64
NA_W = NA_HEADS * NA_HEAD_DIM
WIN_H = 8
WIN_W = 16
MLP_HIDDEN = 4 * D_MODEL
N_BRANCH = 3
IN_SIZES = (CONV_W, CONV_W, CONV_W, SSM_W, NA_W, NA_W, NA_W, D_MODEL, D_MODEL, D_MODEL)
IN_OFF = tuple(sum(IN_SIZES[:i]) for i in range(len(IN_SIZES) + 1))
IN_PROJ_W = IN_OFF[-1]
RMS_EPS = 1e-6
NEG_INF = -1e30
S5_MIN_DECAY = 1e-4

kernel_name = 'hybrid_conv_s5_natten_prefix_dit_trunk'


def rmsnorm(x, g):
    xf = x.astype(jnp.float32)
    y = xf * lax.rsqrt(jnp.mean(xf * xf, axis=-1, keepdims=True) + RMS_EPS)
    return (y * g.astype(jnp.float32)).astype(x.dtype)


def modulate(h, shift, scale):
    return h * (1.0 + scale) + shift


def adaln(cond, w_mod, b_mod, n):
    m = jax.nn.silu(cond) @ w_mod[:, :n * D_MODEL] + b_mod[:n * D_MODEL]
    m = m.reshape(-1, n, D_MODEL)
    return [m[:, i:i + 1, :] for i in range(n)]


def split_in(p):
    return [p[..., IN_OFF[i]:IN_OFF[i + 1]] for i in range(len(IN_SIZES))]


def depthwise_conv3(u, w):
    up = jnp.pad(u, ((0, 0), (1, 1), (0, 0)))
    return up[:, :-2] * w[0] + up[:, 1:-1] * w[1] + up[:, 2:] * w[2]


def short_conv_branch(xa, b_gate, c_gate, conv_w, w_out):
    return (b_gate * depthwise_conv3(c_gate * xa, conv_w)) @ w_out


def s5_discretize(lam_re, lam_im, log_step, b_re, b_im, c_re, c_im):
    f32 = jnp.float32
    lam = lax.complex(jnp.minimum(lam_re.astype(f32), -S5_MIN_DECAY), lam_im.astype(f32))
    lam_dt = lam * jnp.exp(log_step.astype(f32))[..., None]
    lam_bar = jnp.exp(lam_dt)
    b_bar = ((lam_bar - 1.0) / lam)[..., None] * lax.complex(b_re.astype(f32), b_im.astype(f32))
    c_mat = lax.complex(c_re.astype(f32), c_im.astype(f32))
    return lam_dt, lam_bar, b_bar, c_mat


def s5_drive(u, b_bar_d):
    ug = u.astype(jnp.float32).reshape(u.shape[:2] + (SSM_GROUPS, SSM_GROUP))
    return jnp.einsum('blgn,gpn->blgp', ug.astype(jnp.complex64), b_bar_d)


def _scan_combine(left, right):
    a_l, b_l = left
    a_r, b_r = right
    return a_r * a_l, a_r * b_l + b_r


def diag_scan(a, bu):
    a_seq = jnp.broadcast_to(a, bu.shape)
    return lax.associative_scan(_scan_combine, (a_seq, bu), axis=1)[1]


def s5_context_states(uc, s5p):
    _, lam_bar, b_bar, _ = s5p
    h_f = diag_scan(lam_bar[0], s5_drive(uc, b_bar[0]))
    h_b = jnp.flip(diag_scan(lam_bar[1], jnp.flip(s5_drive(uc, b_bar[1]), 1)), 1)
    return h_f, h_b


def s5_latent_states(ux, s5p, h0_f, h0_b):
    lam_dt, lam_bar, b_bar, _ = s5p
    steps = jnp.arange(1, ux.shape[1] + 1, dtype=jnp.float32)[:, None, None]
    h_f = diag_scan(lam_bar[0], s5_drive(ux, b_bar[0])) + jnp.exp(lam_dt[0] * steps) * h0_f[:, None]
    h_b = diag_scan(lam_bar[1], jnp.flip(s5_drive(ux, b_bar[1]), 1)) + jnp.exp(lam_dt[1] * steps) * h0_b[:, None]
    return h_f, jnp.flip(h_b, 1)


def s5_output(u, h_f, h_b, c_mat, d_skip, w_glu_a, w_glu_b):
    y = jnp.real(jnp.einsum('blgp,gnp->blgn', h_f, c_mat[0]) + jnp.einsum('blgp,gnp->blgn', h_b, c_mat[1]))
    y = y.reshape(u.shape).astype(u.dtype) + d_skip * u
    g = jax.nn.gelu(y)
    return (g @ w_glu_a) * jax.nn.sigmoid(g @ w_glu_b)


def to_heads(t):
    return t.reshape(t.shape[:2] + (NA_HEADS, NA_HEAD_DIM))


def context_attention(qc, kc, vc):
    scale = NA_HEAD_DIM ** -0.5
    s = jnp.einsum('bqhd,bkhd->bhqk', to_heads(qc), to_heads(kc)).astype(jnp.float32) * scale
    p = jax.nn.softmax(s, axis=-1).astype(vc.dtype)
    o = jnp.einsum('bhqk,bkhd->bqhd', p, to_heads(vc))
    return o.reshape(qc.shape[:2] + (NA_W,))


def neighbourhood_attention(q, k, v, kc, vc, rpb):
    bsz, seq = q.shape[:2]
    rows = seq // GRID_W
    win_h = min(WIN_H, rows)
    n_loc = win_h * GRID_W
    scale = NA_HEAD_DIM ** -0.5
    grid = (bsz, rows, GRID_W, NA_HEADS, NA_HEAD_DIM)
    qg, kg, vg = q.reshape(grid), k.reshape(grid), v.reshape(grid)
    kch, vch = to_heads(kc), to_heads(vc)
    col = jnp.arange(GRID_W)
    col_start = jnp.clip(col - WIN_W // 2, 0, GRID_W - WIN_W)
    key_in = (col[None, :] >= col_start[:, None]) & (col[None, :] < col_start[:, None] + WIN_W)
    dc_idx = jnp.clip(col[None, :] - col[:, None] + WIN_W - 1, 0, 2 * WIN_W - 2)
    rpb_cols = rpb[:, :, dc_idx]

    def row_block(r):
        r0 = jnp.clip(r - win_h // 2, 0, rows - win_h)
        q_r = lax.dynamic_index_in_dim(qg, r, axis=1, keepdims=False)
        k_r = lax.dynamic_slice_in_dim(kg, r0, win_h, axis=1)
        v_r = lax.dynamic_slice_in_dim(vg, r0, win_h, axis=1)
        dr = r0 + jnp.arange(win_h) - r
        bias = jnp.transpose(rpb_cols[:, dr + WIN_H - 1], (0, 2, 1, 3))
        s_loc = jnp.einsum('bqhd,bwkhd->bhqwk', q_r, k_r).astype(jnp.float32) * scale + bias
        s_loc = jnp.where(key_in[:, None, :], s_loc, NEG_INF)
        s_ctx = jnp.einsum('bqhd,bchd->bhqc', q_r, kch).astype(jnp.float32) * scale
        s = jnp.concatenate([s_loc.reshape(bsz, NA_HEADS, GRID_W, n_loc), s_ctx], axis=-1)
        p = jax.nn.softmax(s, axis=-1).astype(v.dtype)
        p_loc = p[..., :n_loc].reshape(bsz, NA_HEADS, GRID_W, win_h, GRID_W)
        return (jnp.einsum('bhqwk,bwkhd->bqhd', p_loc, v_r)
                + jnp.einsum('bhqc,bchd->bqhd', p[..., n_loc:], vch))

    o = lax.map(row_block, jnp.arange(rows))
    return jnp.moveaxis(o, 0, 1).reshape(bsz, seq, NA_W)


def gated_merge(ya, yb, yc, ga, gb, gc, w_out):
    return (jax.nn.sigmoid(ga) * ya + jax.nn.sigmoid(gb) * yb + jax.nn.sigmoid(gc) * yc) @ w_out


def sqrelu_mlp(h, w1, w2):
    return jnp.square(jax.nn.relu(h @ w1)) @ w2


def setup_inputs(seed: int = 0) -> dict:
    key = jax.random.key(seed)
    ks = iter(jax.random.split(key, 32))
    f32 = jnp.float32

    def nrm(shape, scale):
        return jax.random.normal(next(ks), shape, f32) * scale

    L, G, P, N = DEPTH, SSM_GROUPS, SSM_STATE, SSM_GROUP
    x = nrm((BATCH, SEQ, D_MODEL), 1.0)
    c = nrm((BATCH, D_MODEL), 1.0)
    ctx = nrm((BATCH, CTX_LEN, D_MODEL), 1.0)
    c_ctx = nrm((D_MODEL,), 1.0)
    w_mod = nrm((L, D_MODEL, N_MOD * D_MODEL), D_MODEL ** -0.5)
    b_mod = nrm((L, N_MOD * D_MODEL), 0.01)
    norm1_g = 1.0 + nrm((L, D_MODEL), 0.01)
    w_in = nrm((L, D_MODEL, IN_PROJ_W), D_MODEL ** -0.5)
    conv_w = nrm((L, CONV_K, CONV_W), CONV_K ** -0.5)
    conv_out = nrm((L, CONV_W, D_MODEL), CONV_W ** -0.5)
    s5_lam_re = -0.5 + nrm((L, 2, G, P), 0.01)
    s5_lam_im = math.pi * jnp.arange(P, dtype=f32) + nrm((L, 2, G, P), 0.01)
    s5_log_step = jax.random.uniform(next(ks), (L, 2, G), f32, math.log(1e-3), math.log(1e-1))
    s5_b_re = nrm((L, 2, G, P, N), (2 * N) ** -0.5)
    s5_b_im = nrm((L, 2, G, P, N), (2 * N) ** -0.5)
    s5_c_re = nrm((L, 2, G, N, P), (2 * P) ** -0.5)
    s5_c_im = nrm((L, 2, G, N, P), (2 * P) ** -0.5)
    s5_d = nrm((L, SSM_W), 1.0)
    s5_glu_a = nrm((L, SSM_W, D_MODEL), SSM_W ** -0.5)
    s5_glu_b = nrm((L, SSM_W, D_MODEL), SSM_W ** -0.5)
    na_rpb = nrm((L, NA_HEADS, 2 * WIN_H - 1, 2 * WIN_W - 1), 0.1)
    na_out = nrm((L, NA_W, D_MODEL), NA_W ** -0.5)
    w_out = nrm((L, D_MODEL, D_MODEL), D_MODEL ** -0.5)
    norm2_g = 1.0 + nrm((L, D_MODEL), 0.01)
    mlp_w1 = nrm((L, D_MODEL, MLP_HIDDEN), D_MODEL ** -0.5)
    mlp_w2 = nrm((L, MLP_HIDDEN, D_MODEL), MLP_HIDDEN ** -0.5)
    final_norm_g = 1.0 + nrm((D_MODEL,), 0.01)
    return {'x': x, 'c': c, 'ctx': ctx, 'c_ctx': c_ctx, 'w_mod': w_mod, 'b_mod': b_mod,
            'norm1_g': norm1_g, 'w_in': w_in, 'conv_w': conv_w, 'conv_out': conv_out,
            's5_lam_re': s5_lam_re, 's5_lam_im': s5_lam_im, 's5_log_step': s5_log_step,
            's5_b_re': s5_b_re, 's5_b_im': s5_b_im, 's5_c_re': s5_c_re, 's5_c_im': s5_c_im,
            's5_d': s5_d, 's5_glu_a': s5_glu_a, 's5_glu_b': s5_glu_b, 'na_rpb': na_rpb,
            'na_out': na_out, 'w_out': w_out, 'norm2_g': norm2_g, 'mlp_w1': mlp_w1,
            'mlp_w2': mlp_w2, 'final_norm_g': final_norm_g}


def reference(x, c, ctx, c_ctx, w_mod, b_mod, norm1_g, w_in, conv_w, conv_out,
              s5_lam_re, s5_lam_im, s5_log_step, s5_b_re, s5_b_im, s5_c_re, s5_c_im,
              s5_d, s5_glu_a, s5_glu_b, na_rpb, na_out, w_out, norm2_g, mlp_w1, mlp_w2,
              final_norm_g):
    cx = ctx
    for l in range(DEPTH):
        ctx_out = l < DEPTH - 1
        mx = adaln(c, w_mod[l], b_mod[l], N_MOD)
        mc = adaln(c_ctx, w_mod[l], b_mod[l], N_MOD if ctx_out else 2)
        hx = modulate(rmsnorm(x, norm1_g[l]), mx[0], mx[1])
        hc = modulate(rmsnorm(cx, norm1_g[l]), mc[0], mc[1])
        s5p = s5_discretize(s5_lam_re[l], s5_lam_im[l], s5_log_step[l],
                            s5_b_re[l], s5_b_im[l], s5_c_re[l], s5_c_im[l])
        xa, xb, xcg, ux, qx, kx, vx, gax, gbx, gcx = split_in(hx @ w_in[l])
        if ctx_out:
            ca, cb, ccg, uc, qc, kc, vc, gac, gbc, gcc = split_in(hc @ w_in[l])
        else:
            uc = hc @ w_in[l][:, IN_OFF[3]:IN_OFF[4]]
            kc, vc = jnp.split(hc @ w_in[l][:, IN_OFF[5]:IN_OFF[7]], 2, axis=-1)
        hc_f, hc_b = s5_context_states(uc, s5p)
        hx_f, hx_b = s5_latent_states(ux, s5p, hc_f[:, -1], hc_b[:, 0])
        ya = short_conv_branch(xa, xb, xcg, conv_w[l], conv_out[l])
        yb = s5_output(ux, hx_f, hx_b, s5p[3], s5_d[l], s5_glu_a[l], s5_glu_b[l])
        yc = neighbourhood_attention(qx, kx, vx, kc, vc, na_rpb[l]) @ na_out[l]
        x_mix = gated_merge(ya, yb, yc, gax, gbx, gcx, w_out[l])
        if ctx_out:
            ya_c = short_conv_branch(ca, cb, ccg, conv_w[l], conv_out[l])
            yb_c = s5_output(uc, hc_f, hc_b, s5p[3], s5_d[l], s5_glu_a[l], s5_glu_b[l])
            yc_c = context_attention(qc, kc, vc) @ na_out[l]
            c_mix = gated_merge(ya_c, yb_c, yc_c, gac, gbc, gcc, w_out[l])
            cx = cx + mc[2] * c_mix
            cx = cx + mc[5] * sqrelu_mlp(modulate(rmsnorm(cx, norm2_g[l]), mc[3], mc[4]),
                                         mlp_w1[l], mlp_w2[l])
        x = x + mx[2] * x_mix
        x = x + mx[5] * sqrelu_mlp(modulate(rmsnorm(x, norm2_g[l]), mx[3], mx[4]),
                                   mlp_w1[l], mlp_w2[l])
    return rmsnorm(x, final_norm_g)
```

```python
import functools
import math

import numpy as np
import jax
import jax.numpy as jnp
from jax import lax
from jax.experimental import pallas as pl
from jax.experimental.pallas import tpu as pltpu

D_MODEL = 1024
CTX_LEN = 256
GRID_W = 64
N_MOD = 6
CONV_W = 512
SSM_W = 512
SSM_GROUP = 16
SSM_GROUPS = SSM_W // SSM_GROUP
SSM_STATE = 64
NA_HEADS = 8
NA_HEAD_DIM = 64
NA_W = NA_HEADS * NA_HEAD_DIM
WIN_H = 8
WIN_W = 16
MLP_HIDDEN = 4 * D_MODEL
IN_SIZES = (CONV_W, CONV_W, CONV_W, SSM_W, NA_W, NA_W, NA_W, D_MODEL, D_MODEL, D_MODEL)
IN_OFF = tuple(sum(IN_SIZES[:i]) for i in range(len(IN_SIZES) + 1))
IN_PROJ_W = IN_OFF[-1]
RMS_EPS = 1e-6
NEG_INF = -1e30
S5_MIN_DECAY = 1e-4

F32 = jnp.float32
BF16 = jnp.bfloat16

LANES = 128
SUBLANES = 8
BF16_ROWS = 16
TM = 256
SCAN_T = 32
NA_QROWS = TM // GRID_W
NA_KROWS = NA_QROWS + WIN_H
SSM_LANE_GROUPS = LANES // SSM_GROUP
SSM_BLOCKS = SSM_W // LANES
SSM_BLOCK_STATES = SSM_LANE_GROUPS * SSM_STATE
SSM_STATES = SSM_GROUPS * SSM_STATE
VMEM_LIMIT = 56 * 1024 * 1024


def _cparams(*sem):
    return pltpu.CompilerParams(dimension_semantics=sem, vmem_limit_bytes=VMEM_LIMIT)


def _rms(x, g):
    return x * lax.rsqrt(jnp.mean(x * x, axis=-1, keepdims=True) + RMS_EPS) * g


def _dot(a, b):
    return jnp.dot(a, b, preferred_element_type=F32)


def _dot_nt(a, b):
    return lax.dot_general(a, b, (((1,), (1,)), ((), ())), preferred_element_type=F32)


def _adaln_kernel(c_ref, w_ref, b_ref, o_ref):
    c = c_ref[...]
    s = c * jax.nn.sigmoid(c)
    o_ref[...] = jnp.dot(s, w_ref[...], preferred_element_type=F32,
                         precision=lax.Precision.HIGHEST) + b_ref[...]


def _adaln(cond, w_mod, b_mod):
    depth = w_mod.shape[0]
    rows = cond.shape[0]
    n_tiles = (N_MOD * D_MODEL) // D_MODEL
    return pl.pallas_call(
        _adaln_kernel,
        grid=(depth, n_tiles),
        in_specs=[
            pl.BlockSpec((rows, D_MODEL), lambda l, j: (0, 0)),
            pl.BlockSpec((None, D_MODEL, D_MODEL), lambda l, j: (l, 0, j)),
            pl.BlockSpec((None, 1, D_MODEL), lambda l, j: (l, 0, j)),
        ],
        out_specs=pl.BlockSpec((None, rows, D_MODEL), lambda l, j: (l, 0, j)),
        out_shape=jax.ShapeDtypeStruct((depth, rows, N_MOD * D_MODEL), F32),
        compiler_params=_cparams("parallel", "parallel"),
        name="adaln",
    )(cond, w_mod, b_mod.reshape(depth, 1, N_MOD * D_MODEL))


def _s5_prep_kernel(lre_ref, lim_ref, ls_ref, bre_ref, bim_ref, are_ref, aim_ref, obre_ref, obim_ref):
    lr = jnp.minimum(lre_ref[...], -S5_MIN_DECAY)
    li = lim_ref[...]
    dt = jnp.exp(ls_ref[...])
    xr = lr * dt
    xi = li * dt
    e = jnp.exp(xr)
    ar = e * jnp.cos(xi)
    ai = e * jnp.sin(xi)
    are_ref[...] = ar
    aim_ref[...] = ai
    nr = ar - 1.0
    den = lr * lr + li * li
    cr = (nr * lr + ai * li) / den
    ci = (ai * lr - nr * li) / den
    bre = bre_ref[...]
    bim = bim_ref[...]
    obre_ref[...] = cr * bre - ci * bim
    obim_ref[...] = cr * bim + ci * bre


def _s5_prep(lam_re, lam_im, log_step, b_re, b_im):
    depth = lam_re.shape[0]
    rows = depth * 2 * SSM_GROUPS
    lre = lam_re.reshape(rows, 1, SSM_STATE)
    lim = lam_im.reshape(rows, 1, SSM_STATE)
    ls = jnp.broadcast_to(log_step.reshape(rows, 1, 1), (rows, 1, SSM_STATE))
    bre = jnp.swapaxes(b_re.reshape(rows, SSM_STATE, SSM_GROUP), 1, 2)
    bim = jnp.swapaxes(b_im.reshape(rows, SSM_STATE, SSM_GROUP), 1, 2)
    v2 = jax.ShapeDtypeStruct((rows, 1, SSM_STATE), F32)
    v3 = jax.ShapeDtypeStruct((rows, SSM_GROUP, SSM_STATE), F32)
    return pl.pallas_call(
        _s5_prep_kernel,
        out_shape=[v2, v2, v3, v3],
        name="s5_prep",
    )(lre, lim, ls, bre, bim)


def _block_diag(m):
    nb, ng, a, b = m.shape
    eye = jnp.eye(ng, dtype=m.dtype)
    return jnp.einsum("qgab,gh->qgahb", m, eye).reshape(nb, ng * a, ng * b)


def _in_proj_kernel(x_ref, sh_ref, sc_ref, g_ref, w_ref, conv_ref, u_ref, qkv_ref, gate_ref):
    h = _rms(x_ref[...], g_ref[...]) * (1.0 + sc_ref[...]) + sh_ref[...]
    hb = h.astype(BF16)
    conv_ref[...] = _dot(hb, w_ref[:, IN_OFF[0]:IN_OFF[3]]).astype(BF16)
    u_ref[...] = _dot(hb, w_ref[:, IN_OFF[3]:IN_OFF[4]]).astype(BF16)
    qkv_ref[...] = _dot(hb, w_ref[:, IN_OFF[4]:IN_OFF[7]]).astype(BF16)
    gate_ref[...] = _dot(hb, w_ref[:, IN_OFF[7]:IN_OFF[10]]).astype(BF16)


def _mod_spec():
    return pl.BlockSpec((None, None, 1, D_MODEL), lambda b, i: (b, jnp.minimum(i, 1), 0, 0))


def _const_spec(shape):
    nd = len(shape)
    return pl.BlockSpec(shape, lambda b, i: (0,) * nd)


def _in_proj(xc, shift, scale, g, w):
    bsz, s, _ = xc.shape
    nt = s // TM
    tok = lambda width: pl.BlockSpec((None, TM, width), lambda b, i: (b, i, 0))
    return pl.pallas_call(
        _in_proj_kernel,
        grid=(bsz, nt),
        in_specs=[tok(D_MODEL), _mod_spec(), _mod_spec(), _const_spec((1, D_MODEL)),
                  _const_spec((D_MODEL, IN_PROJ_W))],
        out_specs=[tok(3 * CONV_W),
                   pl.BlockSpec((TM, SSM_W), lambda b, i: (i, b)),
                   tok(3 * NA_W), tok(3 * D_MODEL)],
        out_shape=[jax.ShapeDtypeStruct((bsz, s, 3 * CONV_W), BF16),
                   jax.ShapeDtypeStruct((s, bsz * SSM_W), BF16),
                   jax.ShapeDtypeStruct((bsz, s, 3 * NA_W), BF16),
                   jax.ShapeDtypeStruct((bsz, s, 3 * D_MODEL), BF16)],
        compiler_params=_cparams("parallel", "parallel"),
        name="in_proj",
    )(xc, shift, scale, g, w)


def _s5_scan_kernel(uf_ref, ub_ref, bre_ref, bim_ref, cre_ref, cim_ref, are_ref, aim_ref,
                    yf_ref, yb_ref, sre_ref, sim_ref, hre_ref, him_ref):
    @pl.when(pl.program_id(0) == 0)
    def _():
        hre_ref[...] = jnp.zeros_like(hre_ref)
        him_ref[...] = jnp.zeros_like(him_ref)

    bs = SSM_BLOCK_STATES
    for d, (u_ref, y_ref) in enumerate(((uf_ref, yf_ref), (ub_ref, yb_ref))):
        for q in range(SSM_BLOCKS):
            uq = u_ref[:, q * LANES:(q + 1) * LANES]
            sre_ref[d, :, q * bs:(q + 1) * bs] = _dot(uq, bre_ref[d, q])
            sim_ref[d, :, q * bs:(q + 1) * bs] = _dot(uq, bim_ref[d, q])

        for q in range(SSM_BLOCKS):
            cols = slice(q * bs, (q + 1) * bs)
            ar = are_ref[d, :, cols]
            ai = aim_ref[d, :, cols]

            def step(t, carry, d=d, cols=cols, ar=ar, ai=ai):
                hr, hi = carry
                tt = t if d == 0 else SCAN_T - 1 - t
                rows = pl.ds(pl.multiple_of(tt * SUBLANES, SUBLANES), SUBLANES)
                nr = ar * hr - ai * hi + sre_ref[d, rows, cols]
                ni = ar * hi + ai * hr + sim_ref[d, rows, cols]
                sre_ref[d, rows, cols] = nr
                sim_ref[d, rows, cols] = ni
                return nr, ni

            hr, hi = lax.fori_loop(0, SCAN_T, step, (hre_ref[d, :, cols], him_ref[d, :, cols]),
                                   unroll=4)
            hre_ref[d, :, cols] = hr
            him_ref[d, :, cols] = hi

        for q in range(SSM_BLOCKS):
            cols = slice(q * bs, (q + 1) * bs)
            y = (_dot(sre_ref[d, :, cols].astype(BF16), cre_ref[d, q])
                 - _dot(sim_ref[d, :, cols].astype(BF16), cim_ref[d, q]))
            y_ref[:, q * LANES:(q + 1) * LANES] = y.astype(BF16)


def _s5_scan(u_tm, bre, bim, cre, cim, a_re, a_im, seq):
    rows_total = u_tm.shape[0]
    bsz = rows_total // seq
    assert bsz == SUBLANES
    tile_rows = SCAN_T * bsz
    nt = seq // SCAN_T
    nc = CTX_LEN // SCAN_T

    def bwd_tile(i):
        return jnp.where(i < nc, nc - 1 - i, nt - 1 - (i - nc))

    full = lambda a: pl.BlockSpec(a.shape, lambda i: (0,) * a.ndim)
    y_sds = jax.ShapeDtypeStruct((rows_total, SSM_W), BF16)
    return pl.pallas_call(
        _s5_scan_kernel,
        grid=(nt,),
        in_specs=[pl.BlockSpec((tile_rows, SSM_W), lambda i: (i, 0)),
                  pl.BlockSpec((tile_rows, SSM_W), lambda i: (bwd_tile(i), 0)),
                  full(bre), full(bim), full(cre), full(cim), full(a_re), full(a_im)],
        out_specs=[pl.BlockSpec((tile_rows, SSM_W), lambda i: (i, 0)),
                   pl.BlockSpec((tile_rows, SSM_W), lambda i: (bwd_tile(i), 0))],
        out_shape=[y_sds, y_sds],
        scratch_shapes=[pltpu.VMEM((2, tile_rows, SSM_STATES), F32),
                        pltpu.VMEM((2, tile_rows, SSM_STATES), F32),
                        pltpu.VMEM((2, bsz, SSM_STATES), F32),
                        pltpu.VMEM((2, bsz, SSM_STATES), F32)],
        compiler_params=_cparams("arbitrary"),
        name="s5_scan",
    )(u_tm, u_tm, bre, bim, cre, cim, a_re, a_im)


def _attention_kernel(q_ref, k_ref, v_ref, bias_ref, o_ref, *, grid_rows):
    i = pl.program_id(1)
    scale = NA_HEAD_DIM ** -0.5

    @pl.when(i == 0)
    def _():
        for h in range(NA_HEADS):
            hs = slice(h * NA_HEAD_DIM, (h + 1) * NA_HEAD_DIM)
            s = _dot_nt(q_ref[:, hs], k_ref[0:CTX_LEN, hs]) * scale
            p = jnp.exp(s - jnp.max(s, axis=-1, keepdims=True))
            o = _dot(p.astype(BF16), v_ref[0:CTX_LEN, hs]) / jnp.sum(p, axis=-1, keepdims=True)
            o_ref[:, hs] = o.astype(BF16)

    @pl.when(i > 0)
    def _():
        r_a = (i - 1) * NA_QROWS
        k_start = jnp.clip(r_a - WIN_H // 2, 0, grid_rows - NA_KROWS)
        start = pl.multiple_of(CTX_LEN + k_start * GRID_W, GRID_W)
        loc = pl.ds(start, NA_KROWS * GRID_W)
        for h in range(NA_HEADS):
            hs = slice(h * NA_HEAD_DIM, (h + 1) * NA_HEAD_DIM)
            qh = q_ref[:, hs]
            s_loc = _dot_nt(qh, k_ref[loc, hs]) * scale + bias_ref[h]
            s_ctx = _dot_nt(qh, k_ref[0:CTX_LEN, hs]) * scale
            m = jnp.maximum(jnp.max(s_loc, axis=-1, keepdims=True),
                            jnp.max(s_ctx, axis=-1, keepdims=True))
            p_loc = jnp.exp(s_loc - m)
            p_ctx = jnp.exp(s_ctx - m)
            l = jnp.sum(p_loc, axis=-1, keepdims=True) + jnp.sum(p_ctx, axis=-1, keepdims=True)
            o = _dot(p_loc.astype(BF16), v_ref[loc, hs]) + _dot(p_ctx.astype(BF16), v_ref[0:CTX_LEN, hs])
            o_ref[:, hs] = (o / l).astype(BF16)


def _na_bias_index(grid_rows):
    reps = ((0, 0), (NA_QROWS, 0), (grid_rows - NA_QROWS, grid_rows - NA_KROWS))
    rq = np.arange(NA_QROWS)[:, None, None, None]
    cq = np.arange(GRID_W)[None, :, None, None]
    kr = np.arange(NA_KROWS)[None, None, :, None]
    kc = np.arange(GRID_W)[None, None, None, :]
    dr_all, dc_all, ok_all = [], [], []
    for r_a, k_start in reps:
        r = r_a + rq
        r0 = np.clip(r - WIN_H // 2, 0, grid_rows - WIN_H)
        krow = k_start + kr
        cs = np.clip(cq - WIN_W // 2, 0, GRID_W - WIN_W)
        ok = (krow >= r0) & (krow < r0 + WIN_H) & (kc >= cs) & (kc < cs + WIN_W)
        dr = np.clip(krow - r + WIN_H - 1, 0, 2 * WIN_H - 2)
        dc = np.clip(kc - cq + WIN_W - 1, 0, 2 * WIN_W - 2)
        shape = (TM, NA_KROWS * GRID_W)
        dr_all.append(np.broadcast_to(dr, ok.shape).reshape(shape))
        dc_all.append(np.broadcast_to(dc, ok.shape).reshape(shape))
        ok_all.append(ok.reshape(shape))
    return np.stack(dr_all), np.stack(dc_all), np.stack(ok_all)


def _na_bias_table(rpb, grid_rows):
    dr, dc, ok = _na_bias_index(grid_rows)
    t = rpb[:, dr, dc]
    t = jnp.where(ok[None], t, NEG_INF)
    return jnp.swapaxes(t, 0, 1)


def _attention(qkv, bias, grid_rows):
    bsz, s, _ = qkv.shape
    nt = s // TM
    n_keys = NA_KROWS * GRID_W

    def bias_class(b, i):
        return (jnp.where(i <= 1, 0, jnp.where(i == nt - 1, 2, 1)), 0, 0, 0)

    return pl.pallas_call(
        functools.partial(_attention_kernel, grid_rows=grid_rows),
        grid=(bsz, nt),
        in_specs=[pl.BlockSpec((None, TM, NA_W), lambda b, i: (b, i, 0)),
                  pl.BlockSpec((None, s, NA_W), lambda b, i: (b, 0, 1)),
                  pl.BlockSpec((None, s, NA_W), lambda b, i: (b, 0, 2)),
                  pl.BlockSpec((None, NA_HEADS, TM, n_keys), bias_class)],
        out_specs=pl.BlockSpec((None, TM, NA_W), lambda b, i: (b, i, 0)),
        out_shape=jax.ShapeDtypeStruct((bsz, s, NA_W), BF16),
        compiler_params=_cparams("parallel", "arbitrary"),
        name="attention",
    )(qkv, qkv, qkv, bias)


def _gelu_tanh(x):
    return 0.5 * x * (1.0 + jnp.tanh(math.sqrt(2.0 / math.pi) * (x + 0.044715 * (x * x * x))))


def _merge_kernel(x_ref, cg_ref, cprev_ref, cnext_ref, u_ref, yf_ref, yb_ref, o_ref, gate_ref,
                  g1_ref, cw_ref, cout_ref, d_ref, glua_ref, glub_ref, naout_ref, wout_ref,
                  out_ref, *, n_tiles):
    i = pl.program_id(1)

    def gated_input(ref):
        return ref[:, 2 * CONV_W:3 * CONV_W].astype(F32) * ref[:, 0:CONV_W].astype(F32)

    z = gated_input(cg_ref)
    has_prev = (i >= 2).astype(F32)
    has_next = jnp.logical_and(i != 0, i != n_tiles - 1).astype(F32)
    z_before = gated_input(cprev_ref)[BF16_ROWS - 1:BF16_ROWS] * has_prev
    z_after = gated_input(cnext_ref)[0:1] * has_next
    row = lax.broadcasted_iota(jnp.int32, z.shape, 0)
    z_prev = jnp.where(row == 0, z_before, pltpu.roll(z, 1, 0))
    z_next = jnp.where(row == TM - 1, z_after, pltpu.roll(z, TM - 1, 0))
    conv = z_prev * cw_ref[0:1, :] + z * cw_ref[1:2, :] + z_next * cw_ref[2:3, :]
    a_pre = cg_ref[:, CONV_W:2 * CONV_W].astype(F32) * conv
    ya = _dot(a_pre.astype(BF16), cout_ref[...])

    y = yf_ref[...].astype(F32) + yb_ref[...].astype(F32) + d_ref[...] * u_ref[...].astype(F32)
    g = _gelu_tanh(y).astype(BF16)
    yb = _dot(g, glua_ref[...]) * jax.nn.sigmoid(_dot(g, glub_ref[...]))

    yc = _dot(o_ref[...], naout_ref[...])

    mix = (jax.nn.sigmoid(gate_ref[:, 0:D_MODEL].astype(F32)) * ya
           + jax.nn.sigmoid(gate_ref[:, D_MODEL:2 * D_MODEL].astype(F32)) * yb
           + jax.nn.sigmoid(gate_ref[:, 2 * D_MODEL:3 * D_MODEL].astype(F32)) * yc)
    out_ref[...] = x_ref[...] + g1_ref[...] * _dot(mix.astype(BF16), wout_ref[...])


def _merge(xc, conv_g, u_tm, yf, yb, attn_o, gates, gate1, conv_w, conv_out, s5_d, glu_a, glu_b,
           na_out, w_out):
    bsz, s, _ = xc.shape
    nt = s // TM
    halo_blocks = s // BF16_ROWS
    per_tile = TM // BF16_ROWS
    tok = lambda width: pl.BlockSpec((None, TM, width), lambda b, i: (b, i, 0))
    tm_spec = pl.BlockSpec((TM, SSM_W), lambda b, i: (i, b))
    prev_spec = pl.BlockSpec((None, BF16_ROWS, 3 * CONV_W),
                             lambda b, i: (b, jnp.maximum(i * per_tile - 1, 0), 0))
    next_spec = pl.BlockSpec((None, BF16_ROWS, 3 * CONV_W),
                             lambda b, i: (b, jnp.minimum((i + 1) * per_tile, halo_blocks - 1), 0))
    return pl.pallas_call(
        functools.partial(_merge_kernel, n_tiles=nt),
        grid=(bsz, nt),
        in_specs=[tok(D_MODEL), tok(3 * CONV_W), prev_spec, next_spec, tm_spec, tm_spec, tm_spec,
                  tok(NA_W), tok(3 * D_MODEL), _mod_spec(),
                  _const_spec((3, CONV_W)), _const_spec((CONV_W, D_MODEL)), _const_spec((1, SSM_W)),
                  _const_spec((SSM_W, D_MODEL)), _const_spec((SSM_W, D_MODEL)),
                  _const_spec((NA_W, D_MODEL)), _const_spec((D_MODEL, D_MODEL))],
        out_specs=tok(D_MODEL),
        out_shape=jax.ShapeDtypeStruct(xc.shape, F32),
        compiler_params=_cparams("parallel", "parallel"),
        name="merge",
    )(xc, conv_g, conv_g, conv_g, u_tm, yf, yb, attn_o, gates, gate1, conv_w, conv_out, s5_d,
      glu_a, glu_b, na_out, w_out)


def _mlp_kernel(x_ref, sh_ref, sc_ref, gt_ref, g_ref, w1_ref, w2_ref, fg_ref, out_ref, *, final):
    x = x_ref[...]
    h = (_rms(x, g_ref[...]) * (1.0 + sc_ref[...]) + sh_ref[...]).astype(BF16)
    acc = jnp.zeros_like(x)
    for c in range(MLP_HIDDEN // D_MODEL):
        cols = slice(c * D_MODEL, (c + 1) * D_MODEL)
        a = jnp.maximum(_dot(h, w1_ref[:, cols]), 0.0)
        acc = acc + _dot((a * a).astype(BF16), w2_ref[cols, :])
    y = x + gt_ref[...] * acc
    if final:
        y = _rms(y, fg_ref[...])
    out_ref[...] = y


def _mlp(xc, shift, scale, gate, g, w1, w2, final_g, final):
    bsz, s, _ = xc.shape
    nt = s // TM
    tok = pl.BlockSpec((None, TM, D_MODEL), lambda b, i: (b, i, 0))
    return pl.pallas_call(
        functools.partial(_mlp_kernel, final=final),
        grid=(bsz, nt),
        in_specs=[tok, _mod_spec(), _mod_spec(), _mod_spec(), _const_spec((1, D_MODEL)),
                  _const_spec((D_MODEL, MLP_HIDDEN)), _const_spec((MLP_HIDDEN, D_MODEL)),
                  _const_spec((1, D_MODEL))],
        out_specs=tok,
        out_shape=jax.ShapeDtypeStruct(xc.shape, F32),
        compiler_params=_cparams("parallel", "parallel"),
        name="mlp",
    )(xc, shift, scale, gate, g, w1, w2, final_g)


def kernel(x, c, ctx, c_ctx, w_mod, b_mod, norm1_g, w_in, conv_w, conv_out, s5_lam_re, s5_lam_im,
           s5_log_step, s5_b_re, s5_b_im, s5_c_re, s5_c_im, s5_d, s5_glu_a, s5_glu_b, na_rpb,
           na_out, w_out, norm2_g, mlp_w1, mlp_w2, final_norm_g):
    bsz, seq, _ = x.shape
    depth = w_mod.shape[0]
    assert ctx.shape[1] == CTX_LEN == TM and seq % TM == 0 and bsz == SUBLANES
    grid_rows = seq // GRID_W
    assert grid_rows >= NA_KROWS and grid_rows % NA_QROWS == 0
    s = CTX_LEN + seq

    cond_rows = 2 * SUBLANES
    cond = jnp.zeros((cond_rows, D_MODEL), F32).at[:bsz].set(c).at[bsz].set(c_ctx)
    mods = _adaln(cond, w_mod, b_mod).reshape(depth, cond_rows, N_MOD, D_MODEL)
    lat = mods[:, :bsz]
    cx = jnp.broadcast_to(mods[:, bsz:bsz + 1], lat.shape)
    mod = jnp.stack([cx, lat], axis=2)[:, :, :, :, None, :]

    a_re, a_im, bb_re, bb_im = _s5_prep(s5_lam_re, s5_lam_im, s5_log_step, s5_b_re, s5_b_im)
    a_shape = (depth, 2, 1, SSM_STATES)
    a_re = jnp.broadcast_to(a_re.reshape(a_shape), (depth, 2, SUBLANES, SSM_STATES))
    a_im = jnp.broadcast_to(a_im.reshape(a_shape), (depth, 2, SUBLANES, SSM_STATES))

    def drive_mats(bb):
        m = bb.reshape(depth * 2 * SSM_BLOCKS, SSM_LANE_GROUPS, SSM_GROUP, SSM_STATE)
        return _block_diag(m).reshape(depth, 2, SSM_BLOCKS, LANES, SSM_BLOCK_STATES).astype(BF16)

    def out_mats(cc):
        m = jnp.swapaxes(cc, -1, -2).reshape(depth * 2 * SSM_BLOCKS, SSM_LANE_GROUPS, SSM_STATE,
                                             SSM_GROUP)
        return _block_diag(m).reshape(depth, 2, SSM_BLOCKS, SSM_BLOCK_STATES, LANES).astype(BF16)

    bre_m, bim_m = drive_mats(bb_re), drive_mats(bb_im)
    cre_m, cim_m = out_mats(s5_c_re), out_mats(s5_c_im)

    bf = lambda w: w.astype(BF16)
    w_in_b, conv_out_b, glu_a_b, glu_b_b = bf(w_in), bf(conv_out), bf(s5_glu_a), bf(s5_glu_b)
    na_out_b, w_out_b, w1_b, w2_b = bf(na_out), bf(w_out), bf(mlp_w1), bf(mlp_w2)

    xc = jnp.concatenate([ctx, x], axis=1)
    row = lambda v: v.reshape(1, -1)
    for l in range(depth):
        m = mod[l]
        conv_g, u_tm, qkv, gates = _in_proj(xc, m[:, :, 0], m[:, :, 1], row(norm1_g[l]), w_in_b[l])
        u2 = u_tm.reshape(s * bsz, SSM_W)
        yf, yb = _s5_scan(u2, bre_m[l], bim_m[l], cre_m[l], cim_m[l], a_re[l], a_im[l], s)
        yf = yf.reshape(s, bsz * SSM_W)
        yb = yb.reshape(s, bsz * SSM_W)
        attn_o = _attention(qkv, _na_bias_table(na_rpb[l], grid_rows), grid_rows)
        xc = _merge(xc, conv_g, u_tm, yf, yb, attn_o, gates, m[:, :, 2], conv_w[l], conv_out_b[l],
                    row(s5_d[l]), glu_a_b[l], glu_b_b[l], na_out_b[l], w_out_b[l])
        xc = _mlp(xc, m[:, :, 3], m[:, :, 4], m[:, :, 5], row(norm2_g[l]), w1_b[l], w2_b[l],
                  row(final_norm_g), final=(l == depth - 1))
    return xc[:, CTX_LEN:]
```

```python
import functools
import math

import numpy as np
import jax
import jax.numpy as jnp
from jax import lax
from jax.experimental import pallas as pl
from jax.experimental.pallas import tpu as pltpu

D_MODEL = 1024
CTX_LEN = 256
GRID_W = 64
N_MOD = 6
CONV_W = 512
SSM_W = 512
SSM_GROUP = 16
SSM_GROUPS = SSM_W // SSM_GROUP
SSM_STATE = 64
NA_HEADS = 8
NA_HEAD_DIM = 64
NA_W = NA_HEADS * NA_HEAD_DIM
WIN_H = 8
WIN_W = 16
MLP_HIDDEN = 4 * D_MODEL
IN_SIZES = (CONV_W, CONV_W, CONV_W, SSM_W, NA_W, NA_W, NA_W, D_MODEL, D_MODEL, D_MODEL)
IN_OFF = tuple(sum(IN_SIZES[:i]) for i in range(len(IN_SIZES) + 1))
IN_PROJ_W = IN_OFF[-1]
RMS_EPS = 1e-6
NEG_INF = -1e30
S5_MIN_DECAY = 1e-4

F32 = jnp.float32
BF16 = jnp.bfloat16

LANES = 128
SUBLANES = 8
BF16_ROWS = 16
TM = 256
SCAN_T = 32
NA_QROWS = TM // GRID_W
NA_KROWS = NA_QROWS + WIN_H
SSM_LANE_GROUPS = LANES // SSM_GROUP
SSM_BLOCKS = SSM_W // LANES
SSM_BLOCK_STATES = SSM_LANE_GROUPS * SSM_STATE
SSM_STATES = SSM_GROUPS * SSM_STATE
VMEM_LIMIT = 56 * 1024 * 1024


def _cparams(*sem):
    return pltpu.CompilerParams(dimension_semantics=sem, vmem_limit_bytes=VMEM_LIMIT)


def _rms(x, g):
    return x * lax.rsqrt(jnp.mean(x * x, axis=-1, keepdims=True) + RMS_EPS) * g


def _dot(a, b):
    return jnp.dot(a, b, preferred_element_type=F32)


def _dot_nt(a, b):
    return lax.dot_general(a, b, (((1,), (1,)), ((), ())), preferred_element_type=F32)


def _adaln_kernel(c_ref, w_ref, b_ref, o_ref):
    c = c_ref[...]
    s = c * jax.nn.sigmoid(c)
    o_ref[...] = jnp.dot(s, w_ref[...], preferred_element_type=F32,
                         precision=lax.Precision.HIGHEST) + b_ref[...]


def _adaln(cond, w_mod, b_mod):
    depth = w_mod.shape[0]
    rows = cond.shape[0]
    n_tiles = (N_MOD * D_MODEL) // D_MODEL
    return pl.pallas_call(
        _adaln_kernel,
        grid=(depth, n_tiles),
        in_specs=[
            pl.BlockSpec((rows, D_MODEL), lambda l, j: (0, 0)),
            pl.BlockSpec((None, D_MODEL, D_MODEL), lambda l, j: (l, 0, j)),
            pl.BlockSpec((None, 1, D_MODEL), lambda l, j: (l, 0, j)),
        ],
        out_specs=pl.BlockSpec((None, rows, D_MODEL), lambda l, j: (l, 0, j)),
        out_shape=jax.ShapeDtypeStruct((depth, rows, N_MOD * D_MODEL), F32),
        compiler_params=_cparams("parallel", "parallel"),
        name="adaln",
    )(cond, w_mod, b_mod.reshape(depth, 1, N_MOD * D_MODEL))


def _s5_prep_kernel(lre_ref, lim_ref, ls_ref, bre_ref, bim_ref, are_ref, aim_ref, obre_ref, obim_ref):
    lr = jnp.minimum(lre_ref[...], -S5_MIN_DECAY)
    li = lim_ref[...]
    dt = jnp.exp(ls_ref[...])
    xr = lr * dt
    xi = li * dt
    e = jnp.exp(xr)
    ar = e * jnp.cos(xi)
    ai = e * jnp.sin(xi)
    are_ref[...] = ar
    aim_ref[...] = ai
    nr = ar - 1.0
    den = lr * lr + li * li
    cr = (nr * lr + ai * li) / den
    ci = (ai * lr - nr * li) / den
    bre = bre_ref[...]
    bim = bim_ref[...]
    obre_ref[...] = cr * bre - ci * bim
    obim_ref[...] = cr * bim + ci * bre


def _s5_prep(lam_re, lam_im, log_step, b_re, b_im):
    depth = lam_re.shape[0]
    rows = depth * 2 * SSM_GROUPS
    lre = lam_re.reshape(rows, 1, SSM_STATE)
    lim = lam_im.reshape(rows, 1, SSM_STATE)
    ls = jnp.broadcast_to(log_step.reshape(rows, 1, 1), (rows, 1, SSM_STATE))
    bre = jnp.swapaxes(b_re.reshape(rows, SSM_STATE, SSM_GROUP), 1, 2)
    bim = jnp.swapaxes(b_im.reshape(rows, SSM_STATE, SSM_GROUP), 1, 2)
    v2 = jax.ShapeDtypeStruct((rows, 1, SSM_STATE), F32)
    v3 = jax.ShapeDtypeStruct((rows, SSM_GROUP, SSM_STATE), F32)
    return pl.pallas_call(
        _s5_prep_kernel,
        out_shape=[v2, v2, v3, v3],
        name="s5_prep",
    )(lre, lim, ls, bre, bim)


def _block_diag(m):
    nb, ng, a, b = m.shape
    eye = jnp.eye(ng, dtype=m.dtype)
    return jnp.einsum("qgab,gh->qgahb", m, eye).reshape(nb, ng * a, ng * b)


def _in_proj_kernel(x_ref, sh_ref, sc_ref, g_ref, w_ref, conv_ref, u_ref, qkv_ref, gate_ref):
    h = _rms(x_ref[...], g_ref[...]) * (1.0 + sc_ref[...]) + sh_ref[...]
    hb = h.astype(BF16)
    conv_ref[...] = _dot(hb, w_ref[:, IN_OFF[0]:IN_OFF[3]]).astype(BF16)
    u_ref[...] = _dot(hb, w_ref[:, IN_OFF[3]:IN_OFF[4]]).astype(BF16)
    qkv_ref[...] = _dot(hb, w_ref[:, IN_OFF[4]:IN_OFF[7]]).astype(BF16)
    gate_ref[...] = _dot(hb, w_ref[:, IN_OFF[7]:IN_OFF[10]]).astype(BF16)


def _mod_spec():
    return pl.BlockSpec((None, None, 1, D_MODEL), lambda b, i: (b, jnp.minimum(i, 1), 0, 0))


def _const_spec(shape):
    nd = len(shape)
    return pl.BlockSpec(shape, lambda b, i: (0,) * nd)


def _in_proj(xc, shift, scale, g, w):
    bsz, s, _ = xc.shape
    nt = s // TM
    tok = lambda width: pl.BlockSpec((None, TM, width), lambda b, i: (b, i, 0))
    return pl.pallas_call(
        _in_proj_kernel,
        grid=(bsz, nt),
        in_specs=[tok(D_MODEL), _mod_spec(), _mod_spec(), _const_spec((1, D_MODEL)),
                  _const_spec((D_MODEL, IN_PROJ_W))],
        out_specs=[tok(3 * CONV_W),
                   pl.BlockSpec((TM, SSM_W), lambda b, i: (i, b)),
                   tok(3 * NA_W), tok(3 * D_MODEL)],
        out_shape=[jax.ShapeDtypeStruct((bsz, s, 3 * CONV_W), BF16),
                   jax.ShapeDtypeStruct((s, bsz * SSM_W), BF16),
                   jax.ShapeDtypeStruct((bsz, s, 3 * NA_W), BF16),
                   jax.ShapeDtypeStruct((bsz, s, 3 * D_MODEL), BF16)],
        compiler_params=_cparams("parallel", "parallel"),
        name="in_proj",
    )(xc, shift, scale, g, w)


def _s5_scan_kernel(uf_ref, ub_ref, bre_ref, bim_ref, cre_ref, cim_ref, are_ref, aim_ref,
                    yf_ref, yb_ref, sre_ref, sim_ref, hre_ref, him_ref):
    @pl.when(pl.program_id(0) == 0)
    def _():
        hre_ref[...] = jnp.zeros_like(hre_ref)
        him_ref[...] = jnp.zeros_like(him_ref)

    bs = SSM_BLOCK_STATES
    for d, (u_ref, y_ref) in enumerate(((uf_ref, yf_ref), (ub_ref, yb_ref))):
        for q in range(SSM_BLOCKS):
            uq = u_ref[:, q * LANES:(q + 1) * LANES]
            sre_ref[d, :, q * bs:(q + 1) * bs] = _dot(uq, bre_ref[d, q])
            sim_ref[d, :, q * bs:(q + 1) * bs] = _dot(uq, bim_ref[d, q])

        for q in range(SSM_BLOCKS):
            cols = slice(q * bs, (q + 1) * bs)
            ar = are_ref[d, :, cols]
            ai = aim_ref[d, :, cols]

            def step(t, carry, d=d, cols=cols, ar=ar, ai=ai):
                hr, hi = carry
                tt = t if d == 0 else SCAN_T - 1 - t
                rows = pl.ds(pl.multiple_of(tt * SUBLANES, SUBLANES), SUBLANES)
                nr = ar * hr - ai * hi + sre_ref[d, rows, cols]
                ni = ar * hi + ai * hr + sim_ref[d, rows, cols]
                sre_ref[d, rows, cols] = nr
                sim_ref[d, rows, cols] = ni
                return nr, ni

            hr, hi = lax.fori_loop(0, SCAN_T, step, (hre_ref[d, :, cols], him_ref[d, :, cols]),
                                   unroll=4)
            hre_ref[d, :, cols] = hr
            him_ref[d, :, cols] = hi

        for q in range(SSM_BLOCKS):
            cols = slice(q * bs, (q + 1) * bs)
            y = (_dot(sre_ref[d, :, cols].astype(BF16), cre_ref[d, q])
                 - _dot(sim_ref[d, :, cols].astype(BF16), cim_ref[d, q]))
            y_ref[:, q * LANES:(q + 1) * LANES] = y.astype(BF16)


def _s5_scan(u_tm, bre, bim, cre, cim, a_re, a_im, seq):
    rows_total = u_tm.shape[0]
    bsz = rows_total // seq
    assert bsz == SUBLANES
    tile_rows = SCAN_T * bsz
    nt = seq // SCAN_T
    nc = CTX_LEN // SCAN_T

    def bwd_tile(i):
        return jnp.where(i < nc, nc - 1 - i, nt - 1 - (i - nc))

    full = lambda a: pl.BlockSpec(a.shape, lambda i: (0,) * a.ndim)
    y_sds = jax.ShapeDtypeStruct((rows_total, SSM_W), BF16)
    return pl.pallas_call(
        _s5_scan_kernel,
        grid=(nt,),
        in_specs=[pl.BlockSpec((tile_rows, SSM_W), lambda i: (i, 0)),
                  pl.BlockSpec((tile_rows, SSM_W), lambda i: (bwd_tile(i), 0)),
                  full(bre), full(bim), full(cre), full(cim), full(a_re), full(a_im)],
        out_specs=[pl.BlockSpec((tile_rows, SSM_W), lambda i: (i, 0)),
                   pl.BlockSpec((tile_rows, SSM_W), lambda i: (bwd_tile(i), 0))],
        out_shape=[y_sds, y_sds],
        scratch_shapes=[pltpu.VMEM((2, tile_rows, SSM_STATES), F32),
                        pltpu.VMEM((2, tile_rows, SSM_STATES), F32),
                        pltpu.VMEM((2, bsz, SSM_STATES), F32),
                        pltpu.VMEM((2, bsz, SSM_STATES), F32)],
        compiler_params=_cparams("arbitrary"),
        name="s5_scan",
    )(u_tm, u_tm, bre, bim, cre, cim, a_re, a_im)


def _attention_kernel(q_ref, k_ref, v_ref, bias_ref, o_ref, *, grid_rows):
    i = pl.program_id(1)
    scale = NA_HEAD_DIM ** -0.5

    @pl.when(i == 0)
    def _():
        for h in range(NA_HEADS):
            hs = slice(h * NA_HEAD_DIM, (h + 1) * NA_HEAD_DIM)
            s = _dot_nt(q_ref[:, hs], k_ref[0:CTX_LEN, hs]) * scale
            p = jnp.exp(s - jnp.max(s, axis=-1, keepdims=True))
            o = _dot(p.astype(BF16), v_ref[0:CTX_LEN, hs]) / jnp.sum(p, axis=-1, keepdims=True)
            o_ref[:, hs] = o.astype(BF16)

    @pl.when(i > 0)
    def _():
        r_a = (i - 1) * NA_QROWS
        k_start = jnp.clip(r_a - WIN_H // 2, 0, grid_rows - NA_KROWS)
        start = pl.multiple_of(CTX_LEN + k_start * GRID_W, GRID_W)
        loc = pl.ds(start, NA_KROWS * GRID_W)
        for h in range(NA_HEADS):
            hs = slice(h * NA_HEAD_DIM, (h + 1) * NA_HEAD_DIM)
            qh = q_ref[:, hs]
            s_loc = _dot_nt(qh, k_ref[loc, hs]) * scale + bias_ref[h]
            s_ctx = _dot_nt(qh, k_ref[0:CTX_LEN, hs]) * scale
            m = jnp.maximum(jnp.max(s_loc, axis=-1, keepdims=True),
                            jnp.max(s_ctx, axis=-1, keepdims=True))
            p_loc = jnp.exp(s_loc - m)
            p_ctx = jnp.exp(s_ctx - m)
            l = jnp.sum(p_loc, axis=-1, keepdims=True) + jnp.sum(p_ctx, axis=-1, keepdims=True)
            o = _dot(p_loc.astype(BF16), v_ref[loc, hs]) + _dot(p_ctx.astype(BF16), v_ref[0:CTX_LEN, hs])
            o_ref[:, hs] = (o / l).astype(BF16)


def _na_tile_classes(grid_rows):
    return ((0, 0), (NA_QROWS, 0), (grid_rows - NA_QROWS, grid_rows - NA_KROWS))


def _na_valid(grid_rows):
    rq = np.arange(NA_QROWS)[:, None, None, None]
    cq = np.arange(GRID_W)[None, :, None, None]
    kr = np.arange(NA_KROWS)[None, None, :, None]
    kc = np.arange(GRID_W)[None, None, None, :]
    oks = []
    for r_a, k_start in _na_tile_classes(grid_rows):
        r0 = np.clip(r_a + rq - WIN_H // 2, 0, grid_rows - WIN_H)
        krow = k_start + kr
        cs = np.clip(cq - WIN_W // 2, 0, GRID_W - WIN_W)
        ok = (krow >= r0) & (krow < r0 + WIN_H) & (kc >= cs) & (kc < cs + WIN_W)
        oks.append(ok.reshape(TM, NA_KROWS * GRID_W))
    return np.stack(oks)


def _na_bias_tables(rpb, grid_rows):
    depth, heads, n_dr, n_dc = rpb.shape
    w = GRID_W
    lo = (w - 1) - (WIN_W - 1)
    v = jnp.pad(rpb, ((0, 0), (0, 0), (0, 0), (lo, 2 * w - lo - n_dc)))
    skew = jnp.broadcast_to(v[..., None, :], v.shape[:-1] + (w, 2 * w))
    skew = skew.reshape(v.shape[:-1] + (2 * w * w,))[..., :w * (2 * w - 1)]
    toep = skew.reshape(v.shape[:-1] + (w, 2 * w - 1))[..., w - 1:]
    margin = NA_QROWS
    toep = jnp.pad(toep, ((0, 0), (0, 0), (margin, margin), (0, 0), (0, 0)))
    tables = []
    for r_a, k_start in _na_tile_classes(grid_rows):
        blocks = []
        for rq in range(NA_QROWS):
            first = k_start - (r_a + rq) + (WIN_H - 1) + margin
            blocks.append(toep[:, :, first:first + NA_KROWS])
        t = jnp.stack(blocks, axis=2)
        t = jnp.transpose(t, (0, 1, 2, 4, 3, 5)).reshape(depth, heads, TM, NA_KROWS * w)
        tables.append(t)
    t = jnp.stack(tables, axis=1)
    return jnp.where(_na_valid(grid_rows)[None, :, None], t, NEG_INF)


def _attention(qkv, bias, grid_rows):
    bsz, s, _ = qkv.shape
    nt = s // TM
    n_keys = NA_KROWS * GRID_W

    def bias_class(b, i):
        return (jnp.where(i <= 1, 0, jnp.where(i == nt - 1, 2, 1)), 0, 0, 0)

    return pl.pallas_call(
        functools.partial(_attention_kernel, grid_rows=grid_rows),
        grid=(bsz, nt),
        in_specs=[pl.BlockSpec((None, TM, NA_W), lambda b, i: (b, i, 0)),
                  pl.BlockSpec((None, s, NA_W), lambda b, i: (b, 0, 1)),
                  pl.BlockSpec((None, s, NA_W), lambda b, i: (b, 0, 2)),
                  pl.BlockSpec((None, NA_HEADS, TM, n_keys), bias_class)],
        out_specs=pl.BlockSpec((None, TM, NA_W), lambda b, i: (b, i, 0)),
        out_shape=jax.ShapeDtypeStruct((bsz, s, NA_W), BF16),
        compiler_params=_cparams("parallel", "arbitrary"),
        name="attention",
    )(qkv, qkv, qkv, bias)


def _gelu_tanh(x):
    return 0.5 * x * (1.0 + jnp.tanh(math.sqrt(2.0 / math.pi) * (x + 0.044715 * (x * x * x))))


def _merge_kernel(x_ref, cg_ref, cprev_ref, cnext_ref, u_ref, yf_ref, yb_ref, o_ref, gate_ref,
                  g1_ref, cw_ref, cout_ref, d_ref, glua_ref, glub_ref, naout_ref, wout_ref,
                  out_ref, *, n_tiles):
    i = pl.program_id(1)

    def gated_input(ref):
        return ref[:, 2 * CONV_W:3 * CONV_W].astype(F32) * ref[:, 0:CONV_W].astype(F32)

    z = gated_input(cg_ref)
    has_prev = (i >= 2).astype(F32)
    has_next = jnp.logical_and(i != 0, i != n_tiles - 1).astype(F32)
    z_before = gated_input(cprev_ref)[BF16_ROWS - 1:BF16_ROWS] * has_prev
    z_after = gated_input(cnext_ref)[0:1] * has_next
    row = lax.broadcasted_iota(jnp.int32, z.shape, 0)
    z_prev = jnp.where(row == 0, z_before, pltpu.roll(z, 1, 0))
    z_next = jnp.where(row == TM - 1, z_after, pltpu.roll(z, TM - 1, 0))
    conv = z_prev * cw_ref[0:1, :] + z * cw_ref[1:2, :] + z_next * cw_ref[2:3, :]
    a_pre = cg_ref[:, CONV_W:2 * CONV_W].astype(F32) * conv
    ya = _dot(a_pre.astype(BF16), cout_ref[...])

    y = yf_ref[...].astype(F32) + yb_ref[...].astype(F32) + d_ref[...] * u_ref[...].astype(F32)
    g = _gelu_tanh(y).astype(BF16)
    yb = _dot(g, glua_ref[...]) * jax.nn.sigmoid(_dot(g, glub_ref[...]))

    yc = _dot(o_ref[...], naout_ref[...])

    mix = (jax.nn.sigmoid(gate_ref[:, 0:D_MODEL].astype(F32)) * ya
           + jax.nn.sigmoid(gate_ref[:, D_MODEL:2 * D_MODEL].astype(F32)) * yb
           + jax.nn.sigmoid(gate_ref[:, 2 * D_MODEL:3 * D_MODEL].astype(F32)) * yc)
    out_ref[...] = x_ref[...] + g1_ref[...] * _dot(mix.astype(BF16), wout_ref[...])


def _merge(xc, conv_g, u_tm, yf, yb, attn_o, gates, gate1, conv_w, conv_out, s5_d, glu_a, glu_b,
           na_out, w_out):
    bsz, s, _ = xc.shape
    nt = s // TM
    halo_blocks = s // BF16_ROWS
    per_tile = TM // BF16_ROWS
    tok = lambda width: pl.BlockSpec((None, TM, width), lambda b, i: (b, i, 0))
    tm_spec = pl.BlockSpec((TM, SSM_W), lambda b, i: (i, b))
    prev_spec = pl.BlockSpec((None, BF16_ROWS, 3 * CONV_W),
                             lambda b, i: (b, jnp.maximum(i * per_tile - 1, 0), 0))
    next_spec = pl.BlockSpec((None, BF16_ROWS, 3 * CONV_W),
                             lambda b, i: (b, jnp.minimum((i + 1) * per_tile, halo_blocks - 1), 0))
    return pl.pallas_call(
        functools.partial(_merge_kernel, n_tiles=nt),
        grid=(bsz, nt),
        in_specs=[tok(D_MODEL), tok(3 * CONV_W), prev_spec, next_spec, tm_spec, tm_spec, tm_spec,
                  tok(NA_W), tok(3 * D_MODEL), _mod_spec(),
                  _const_spec((3, CONV_W)), _const_spec((CONV_W, D_MODEL)), _const_spec((1, SSM_W)),
                  _const_spec((SSM_W, D_MODEL)), _const_spec((SSM_W, D_MODEL)),
                  _const_spec((NA_W, D_MODEL)), _const_spec((D_MODEL, D_MODEL))],
        out_specs=tok(D_MODEL),
        out_shape=jax.ShapeDtypeStruct(xc.shape, F32),
        compiler_params=_cparams("parallel", "parallel"),
        name="merge",
    )(xc, conv_g, conv_g, conv_g, u_tm, yf, yb, attn_o, gates, gate1, conv_w, conv_out, s5_d,
      glu_a, glu_b, na_out, w_out)


def _mlp_kernel(x_ref, sh_ref, sc_ref, gt_ref, g_ref, w1_ref, w2_ref, fg_ref, out_ref, *, final):
    x = x_ref[...]
    h = (_rms(x, g_ref[...]) * (1.0 + sc_ref[...]) + sh_ref[...]).astype(BF16)
    acc = jnp.zeros_like(x)
    for c in range(MLP_HIDDEN // D_MODEL):
        cols = slice(c * D_MODEL, (c + 1) * D_MODEL)
        a = jnp.maximum(_dot(h, w1_ref[:, cols]), 0.0)
        acc = acc + _dot((a * a).astype(BF16), w2_ref[cols, :])
    y = x + gt_ref[...] * acc
    if final:
        y = _rms(y, fg_ref[...])
    out_ref[...] = y


def _mlp(xc, shift, scale, gate, g, w1, w2, final_g, final):
    bsz, s, _ = xc.shape
    nt = s // TM
    tok = pl.BlockSpec((None, TM, D_MODEL), lambda b, i: (b, i, 0))
    return pl.pallas_call(
        functools.partial(_mlp_kernel, final=final),
        grid=(bsz, nt),
        in_specs=[tok, _mod_spec(), _mod_spec(), _mod_spec(), _const_spec((1, D_MODEL)),
                  _const_spec((D_MODEL, MLP_HIDDEN)), _const_spec((MLP_HIDDEN, D_MODEL)),
                  _const_spec((1, D_MODEL))],
        out_specs=tok,
        out_shape=jax.ShapeDtypeStruct(xc.shape, F32),
        compiler_params=_cparams("parallel", "parallel"),
        name="mlp",
    )(xc, shift, scale, gate, g, w1, w2, final_g)


def kernel(x, c, ctx, c_ctx, w_mod, b_mod, norm1_g, w_in, conv_w, conv_out, s5_lam_re, s5_lam_im,
           s5_log_step, s5_b_re, s5_b_im, s5_c_re, s5_c_im, s5_d, s5_glu_a, s5_glu_b, na_rpb,
           na_out, w_out, norm2_g, mlp_w1, mlp_w2, final_norm_g):
    bsz, seq, _ = x.shape
    depth = w_mod.shape[0]
    assert ctx.shape[1] == CTX_LEN == TM and seq % TM == 0 and bsz == SUBLANES
    grid_rows = seq // GRID_W
    assert grid_rows >= NA_KROWS and grid_rows % NA_QROWS == 0
    s = CTX_LEN + seq

    cond_rows = 2 * SUBLANES
    cond = jnp.zeros((cond_rows, D_MODEL), F32).at[:bsz].set(c).at[bsz].set(c_ctx)
    mods = _adaln(cond, w_mod, b_mod).reshape(depth, cond_rows, N_MOD, D_MODEL)
    lat = mods[:, :bsz]
    cx = jnp.broadcast_to(mods[:, bsz:bsz + 1], lat.shape)
    mod = jnp.stack([cx, lat], axis=2)[:, :, :, :, None, :]

    a_re, a_im, bb_re, bb_im = _s5_prep(s5_lam_re, s5_lam_im, s5_log_step, s5_b_re, s5_b_im)
    a_shape = (depth, 2, 1, SSM_STATES)
    a_re = jnp.broadcast_to(a_re.reshape(a_shape), (depth, 2, SUBLANES, SSM_STATES))
    a_im = jnp.broadcast_to(a_im.reshape(a_shape), (depth, 2, SUBLANES, SSM_STATES))

    def drive_mats(bb):
        m = bb.reshape(depth * 2 * SSM_BLOCKS, SSM_LANE_GROUPS, SSM_GROUP, SSM_STATE)
        return _block_diag(m).reshape(depth, 2, SSM_BLOCKS, LANES, SSM_BLOCK_STATES).astype(BF16)

    def out_mats(cc):
        m = jnp.swapaxes(cc, -1, -2).reshape(depth * 2 * SSM_BLOCKS, SSM_LANE_GROUPS, SSM_STATE,
                                             SSM_GROUP)
        return _block_diag(m).reshape(depth, 2, SSM_BLOCKS, SSM_BLOCK_STATES, LANES).astype(BF16)

    bre_m, bim_m = drive_mats(bb_re), drive_mats(bb_im)
    cre_m, cim_m = out_mats(s5_c_re), out_mats(s5_c_im)

    bf = lambda w: w.astype(BF16)
    w_in_b, conv_out_b, glu_a_b, glu_b_b = bf(w_in), bf(conv_out), bf(s5_glu_a), bf(s5_glu_b)
    na_out_b, w_out_b, w1_b, w2_b = bf(na_out), bf(w_out), bf(mlp_w1), bf(mlp_w2)

    na_bias = _na_bias_tables(na_rpb, grid_rows)

    xc = jnp.concatenate([ctx, x], axis=1)
    row = lambda v: v.reshape(1, -1)
    for l in range(depth):
        m = mod[l]
        conv_g, u_tm, qkv, gates = _in_proj(xc, m[:, :, 0], m[:, :, 1], row(norm1_g[l]), w_in_b[l])
        u2 = u_tm.reshape(s * bsz, SSM_W)
        yf, yb = _s5_scan(u2, bre_m[l], bim_m[l], cre_m[l], cim_m[l], a_re[l], a_im[l], s)
        yf = yf.reshape(s, bsz * SSM_W)
        yb = yb.reshape(s, bsz * SSM_W)
        attn_o = _attention(qkv, na_bias[l], grid_rows)
        xc = _merge(xc, conv_g, u_tm, yf, yb, attn_o, gates, m[:, :, 2], conv_w[l], conv_out_b[l],
                    row(s5_d[l]), glu_a_b[l], glu_b_b[l], na_out_b[l], w_out_b[l])
        xc = _mlp(xc, m[:, :, 3], m[:, :, 4], m[:, :, 5], row(norm2_g[l]), w1_b[l], w2_b[l],
                  row(final_norm_g), final=(l == depth - 1))
    return xc[:, CTX_LEN:]
```

```python
import functools
import math

import numpy as np
import jax
import jax.numpy as jnp
from jax import lax
from jax.experimental import pallas as pl
from jax.experimental.pallas import tpu as pltpu

D_MODEL = 1024
CTX_LEN = 256
GRID_W = 64
N_MOD = 6
CONV_W = 512
SSM_W = 512
SSM_GROUP = 16
SSM_GROUPS = SSM_W // SSM_GROUP
SSM_STATE = 64
NA_HEADS = 8
NA_HEAD_DIM = 64
NA_W = NA_HEADS * NA_HEAD_DIM
WIN_H = 8
WIN_W = 16
MLP_HIDDEN = 4 * D_MODEL
IN_SIZES = (CONV_W, CONV_W, CONV_W, SSM_W, NA_W, NA_W, NA_W, D_MODEL, D_MODEL, D_MODEL)
IN_OFF = tuple(sum(IN_SIZES[:i]) for i in range(len(IN_SIZES) + 1))
IN_PROJ_W = IN_OFF[-1]
RMS_EPS = 1e-6
NEG_INF = -1e30
S5_MIN_DECAY = 1e-4

F32 = jnp.float32
BF16 = jnp.bfloat16

LANES = 128
SUBLANES = 8
BF16_ROWS = 16
TM = 256
SCAN_LC = 2
SCAN_T = 64
SCAN_CHUNKS = SCAN_T // SCAN_LC
NA_QROWS = TM // GRID_W
NA_KROWS = NA_QROWS + WIN_H
SSM_LANE_GROUPS = LANES // SSM_GROUP
SSM_BLOCKS = SSM_W // LANES
SSM_BLOCK_STATES = SSM_LANE_GROUPS * SSM_STATE
SSM_STATES = SSM_GROUPS * SSM_STATE
VMEM_LIMIT = 56 * 1024 * 1024


def _cparams(*sem):
    return pltpu.CompilerParams(dimension_semantics=sem, vmem_limit_bytes=VMEM_LIMIT)


def _rms(x, g):
    return x * lax.rsqrt(jnp.mean(x * x, axis=-1, keepdims=True) + RMS_EPS) * g


def _dot(a, b):
    return jnp.dot(a, b, preferred_element_type=F32)


def _dot_nt(a, b):
    return lax.dot_general(a, b, (((1,), (1,)), ((), ())), preferred_element_type=F32)


def _adaln_kernel(c_ref, w_ref, b_ref, o_ref):
    c = c_ref[...]
    s = c * jax.nn.sigmoid(c)
    o_ref[...] = jnp.dot(s, w_ref[...], preferred_element_type=F32,
                         precision=lax.Precision.HIGHEST) + b_ref[...]


def _adaln(cond, w_mod, b_mod):
    depth = w_mod.shape[0]
    rows = cond.shape[0]
    n_tiles = (N_MOD * D_MODEL) // D_MODEL
    return pl.pallas_call(
        _adaln_kernel,
        grid=(depth, n_tiles),
        in_specs=[
            pl.BlockSpec((rows, D_MODEL), lambda l, j: (0, 0)),
            pl.BlockSpec((None, D_MODEL, D_MODEL), lambda l, j: (l, 0, j)),
            pl.BlockSpec((None, 1, D_MODEL), lambda l, j: (l, 0, j)),
        ],
        out_specs=pl.BlockSpec((None, rows, D_MODEL), lambda l, j: (l, 0, j)),
        out_shape=jax.ShapeDtypeStruct((depth, rows, N_MOD * D_MODEL), F32),
        compiler_params=_cparams("parallel", "parallel"),
        name="adaln",
    )(cond, w_mod, b_mod.reshape(depth, 1, N_MOD * D_MODEL))


def _s5_prep_kernel(lre_ref, lim_ref, ls_ref, bre_ref, bim_ref, cre_ref, cim_ref,
                    are_ref, aim_ref, fre_ref, fim_ref, ore_ref, oim_ref):
    lr = jnp.minimum(lre_ref[...], -S5_MIN_DECAY)
    li = lim_ref[...]
    dt = jnp.exp(ls_ref[...])
    xr = lr * dt
    xi = li * dt

    def power(k):
        e = jnp.exp(k * xr)
        return e * jnp.cos(k * xi), e * jnp.sin(k * xi)

    ar, ai = power(1.0)
    nr = ar - 1.0
    den = lr * lr + li * li
    cr = (nr * lr + ai * li) / den
    ci = (ai * lr - nr * li) / den
    bre = bre_ref[...]
    bim = bim_ref[...]
    bbr = cr * bre - ci * bim
    bbi = cr * bim + ci * bre
    cre = cre_ref[...]
    cim = cim_ref[...]
    fre_ref[:, 0] = bbr
    fim_ref[:, 0] = bbi
    ore_ref[:, 0] = cre
    oim_ref[:, 0] = cim
    for k in range(1, SCAN_LC + 1):
        pr, pi = power(float(k))
        if k < SCAN_LC:
            fre_ref[:, k] = pr * bbr - pi * bbi
            fim_ref[:, k] = pr * bbi + pi * bbr
        ore_ref[:, k] = pr * cre - pi * cim
        oim_ref[:, k] = pr * cim + pi * cre
    are_ref[...], aim_ref[...] = power(float(SCAN_LC))


def _s5_prep(lam_re, lam_im, log_step, b_re, b_im, c_re, c_im):
    depth = lam_re.shape[0]
    rows = depth * 2 * SSM_GROUPS
    rb = SSM_GROUPS
    lre = lam_re.reshape(rows, 1, SSM_STATE)
    lim = lam_im.reshape(rows, 1, SSM_STATE)
    ls = jnp.broadcast_to(log_step.reshape(rows, 1, 1), (rows, 1, SSM_STATE))
    bre = jnp.swapaxes(b_re.reshape(rows, SSM_STATE, SSM_GROUP), 1, 2)
    bim = jnp.swapaxes(b_im.reshape(rows, SSM_STATE, SSM_GROUP), 1, 2)
    cre = c_re.reshape(rows, SSM_GROUP, SSM_STATE)
    cim = c_im.reshape(rows, SSM_GROUP, SSM_STATE)
    vec = pl.BlockSpec((rb, 1, SSM_STATE), lambda r: (r, 0, 0))
    mat = pl.BlockSpec((rb, SSM_GROUP, SSM_STATE), lambda r: (r, 0, 0))
    pw = lambda n: pl.BlockSpec((rb, n, SSM_GROUP, SSM_STATE), lambda r: (r, 0, 0, 0))
    v2 = jax.ShapeDtypeStruct((rows, 1, SSM_STATE), F32)
    v4 = lambda n: jax.ShapeDtypeStruct((rows, n, SSM_GROUP, SSM_STATE), F32)
    return pl.pallas_call(
        _s5_prep_kernel,
        grid=(rows // rb,),
        in_specs=[vec, vec, vec, mat, mat, mat, mat],
        out_specs=[vec, vec, pw(SCAN_LC), pw(SCAN_LC), pw(SCAN_LC + 1), pw(SCAN_LC + 1)],
        out_shape=[v2, v2, v4(SCAN_LC), v4(SCAN_LC), v4(SCAN_LC + 1), v4(SCAN_LC + 1)],
        compiler_params=pltpu.CompilerParams(dimension_semantics=("parallel",)),
        name="s5_prep",
    )(lre, lim, ls, bre, bim, cre, cim)


def _s5_toe_kernel(bre_ref, bim_ref, ore_ref, oim_ref, k_ref):
    hi = lax.Precision.HIGHEST
    for k in range(SCAN_LC):
        k_ref[k] = (jnp.dot(bre_ref[...], ore_ref[k], preferred_element_type=F32, precision=hi)
                    - jnp.dot(bim_ref[...], oim_ref[k], preferred_element_type=F32, precision=hi))


def _s5_toe(bd_re, bd_im, oc_re, oc_im):
    nb = bd_re.shape[0]
    bspec = pl.BlockSpec((None, LANES, SSM_BLOCK_STATES), lambda r: (r, 0, 0))
    ospec = pl.BlockSpec((None, SCAN_LC, SSM_BLOCK_STATES, LANES), lambda r: (r, 0, 0, 0))
    return pl.pallas_call(
        _s5_toe_kernel,
        grid=(nb,),
        in_specs=[bspec, bspec, ospec, ospec],
        out_specs=pl.BlockSpec((None, SCAN_LC, LANES, LANES), lambda r: (r, 0, 0, 0)),
        out_shape=jax.ShapeDtypeStruct((nb, SCAN_LC, LANES, LANES), F32),
        compiler_params=pltpu.CompilerParams(dimension_semantics=("parallel",)),
        name="s5_toe",
    )(bd_re, bd_im, oc_re, oc_im)


def _block_diag(m):
    ng, a, b = m.shape[-3:]
    eye = jnp.eye(ng, dtype=m.dtype)
    out = jnp.einsum("...gab,gh->...gahb", m, eye)
    return out.reshape(m.shape[:-3] + (ng * a, ng * b))


def _s5_matrices(lam_re, lam_im, log_step, b_re, b_im, c_re, c_im):
    depth = lam_re.shape[0]
    lc = SCAN_LC
    a_re, a_im, f_re, f_im, o_re, o_im = _s5_prep(lam_re, lam_im, log_step, b_re, b_im, c_re, c_im)
    lead = (depth, 2, SSM_BLOCKS, SSM_LANE_GROUPS)

    def per_step(x, n):
        return jnp.moveaxis(x.reshape(lead + (n, SSM_GROUP, SSM_STATE)), 4, 3)

    def mirror_bwd(x, axis):
        return jnp.stack([x[:, 0], jnp.flip(x[:, 1], axis=axis - 1)], axis=1)

    def mirror_fwd(x, axis):
        return jnp.stack([jnp.flip(x[:, 0], axis=axis - 1), x[:, 1]], axis=1)

    def fold_mat(f):
        m = _block_diag(mirror_fwd(per_step(f, lc), 3))
        return m.reshape(depth, 2, SSM_BLOCKS, lc * LANES, SSM_BLOCK_STATES).astype(BF16)

    def out_blocks(o):
        return _block_diag(jnp.swapaxes(per_step(o, lc + 1), -1, -2))

    def out_mat(ob):
        m = mirror_bwd(ob[:, :, :, 1:], 3)
        m = jnp.moveaxis(m, 3, 4)
        return m.reshape(depth, 2, SSM_BLOCKS, SSM_BLOCK_STATES, lc * LANES).astype(BF16)

    ob_re, ob_im = out_blocks(o_re), out_blocks(o_im)
    nb = depth * 2 * SSM_BLOCKS
    bd = lambda f: _block_diag(per_step(f, lc)[:, :, :, 0]).reshape(nb, LANES, SSM_BLOCK_STATES)
    oc = lambda ob: ob[:, :, :, :lc].reshape(nb, lc, SSM_BLOCK_STATES, LANES)
    kk = _s5_toe(bd(f_re), bd(f_im), oc(ob_re), oc(ob_im))
    kk = kk.reshape(depth, 2, SSM_BLOCKS, lc, LANES, LANES)
    zero = jnp.zeros((depth, SSM_BLOCKS, LANES, LANES), F32)
    toes = []
    for d in range(2):
        rows = []
        for j_in in range(lc):
            lag = [(j_out - j_in) if d == 0 else (j_in - j_out) for j_out in range(lc)]
            rows.append(jnp.concatenate([kk[:, d, :, g] if g >= 0 else zero for g in lag], axis=-1))
        toes.append(jnp.concatenate(rows, axis=-2))
    toe = jnp.stack(toes, axis=1).astype(BF16)

    a_shape = (depth, 2, 1, SSM_STATES)
    a_re = jnp.broadcast_to(a_re.reshape(a_shape), (depth, 2, SUBLANES, SSM_STATES))
    a_im = jnp.broadcast_to(a_im.reshape(a_shape), (depth, 2, SUBLANES, SSM_STATES))
    return fold_mat(f_re), fold_mat(f_im), out_mat(ob_re), out_mat(ob_im), toe, a_re, a_im


def _batch_rows(b):
    return pl.ds(b, TM, stride=SUBLANES)


def _in_proj_kernel(x_ref, sh_ref, sc_ref, g_ref, w_ref, conv_ref, u_ref, qkv_ref, gate_ref):
    h = _rms(x_ref[...], g_ref[...]) * (1.0 + sc_ref[...]) + sh_ref[...]
    hb = h.astype(BF16)
    conv_ref[...] = _dot(hb, w_ref[:, IN_OFF[0]:IN_OFF[3]]).astype(BF16)
    u = _dot(hb, w_ref[:, IN_OFF[3]:IN_OFF[4]])
    rows = _batch_rows(pl.program_id(1))
    for q in range(SSM_BLOCKS):
        u_ref[q, rows, :] = u[:, q * LANES:(q + 1) * LANES]
    qkv_ref[...] = _dot(hb, w_ref[:, IN_OFF[4]:IN_OFF[7]]).astype(BF16)
    gate_ref[...] = _dot(hb, w_ref[:, IN_OFF[7]:IN_OFF[10]]).astype(BF16)


def _grid_bi(batch_major):
    return (lambda b, i: (b, i)) if batch_major else (lambda i, b: (b, i))


def _tok_spec(width, batch_major=True):
    bi = _grid_bi(batch_major)
    return pl.BlockSpec((None, TM, width), lambda *g: bi(*g) + (0,))


def _mod_spec(batch_major=True):
    bi = _grid_bi(batch_major)

    def index(*g):
        b, i = bi(*g)
        return (b, jnp.minimum(i, 1), 0, 0)

    return pl.BlockSpec((None, None, 1, D_MODEL), index)


def _const_spec(shape):
    nd = len(shape)
    return pl.BlockSpec(shape, lambda *g: (0,) * nd)


def _slab_spec():
    return pl.BlockSpec((SSM_BLOCKS, TM * SUBLANES, LANES), lambda i, b: (0, i, 0))


def _in_proj(xc, shift, scale, g, w):
    bsz, s, _ = xc.shape
    nt = s // TM
    tok = lambda width: _tok_spec(width, batch_major=False)
    mod = _mod_spec(batch_major=False)
    return pl.pallas_call(
        _in_proj_kernel,
        grid=(nt, bsz),
        in_specs=[tok(D_MODEL), mod, mod, _const_spec((1, D_MODEL)),
                  _const_spec((D_MODEL, IN_PROJ_W))],
        out_specs=[tok(3 * CONV_W), _slab_spec(), tok(3 * NA_W), tok(3 * D_MODEL)],
        out_shape=[jax.ShapeDtypeStruct((bsz, s, 3 * CONV_W), BF16),
                   jax.ShapeDtypeStruct((SSM_BLOCKS, s * bsz, LANES), F32),
                   jax.ShapeDtypeStruct((bsz, s, 3 * NA_W), BF16),
                   jax.ShapeDtypeStruct((bsz, s, 3 * D_MODEL), BF16)],
        compiler_params=_cparams("parallel", "arbitrary"),
        name="in_proj",
    )(xc, shift, scale, g, w)


def _s5_scan_kernel(uf_ref, ub_ref, fre_ref, fim_ref, ore_ref, oim_ref, toe_ref, are_ref, aim_ref,
                    yf_ref, yb_ref, sre_ref, sim_ref, hre_ref, him_ref):
    @pl.when(pl.program_id(0) == 0)
    def _():
        hre_ref[...] = jnp.zeros_like(hre_ref)
        him_ref[...] = jnp.zeros_like(him_ref)

    bs = SSM_BLOCK_STATES
    rows_c = SCAN_CHUNKS * SUBLANES

    def put(y_ref, q, y, first):
        for j in range(SCAN_LC):
            yj = y[:, j * LANES:(j + 1) * LANES].reshape(SCAN_CHUNKS, SUBLANES, LANES)
            y_ref[q, :, j] = yj if first else y_ref[q, :, j] + yj

    for d, (u_ref, y_ref) in enumerate(((uf_ref, yf_ref), (ub_ref, yb_ref))):
        for q in range(SSM_BLOCKS):
            uc = jnp.concatenate([u_ref[q, :, j].reshape(rows_c, LANES) for j in range(SCAN_LC)],
                                 axis=1).astype(BF16)
            sre_ref[d, :, q * bs:(q + 1) * bs] = _dot(uc, fre_ref[d, q])
            sim_ref[d, :, q * bs:(q + 1) * bs] = _dot(uc, fim_ref[d, q])
            put(y_ref, q, _dot(uc, toe_ref[d, q]), True)

        for q in range(SSM_BLOCKS):
            cols = slice(q * bs, (q + 1) * bs)
            ar = are_ref[d, :, cols]
            ai = aim_ref[d, :, cols]

            def step(c, carry, d=d, cols=cols, ar=ar, ai=ai):
                hr, hi = carry
                cc = c if d == 0 else SCAN_CHUNKS - 1 - c
                rows = pl.ds(pl.multiple_of(cc * SUBLANES, SUBLANES), SUBLANES)
                inc_r = sre_ref[d, rows, cols]
                inc_i = sim_ref[d, rows, cols]
                sre_ref[d, rows, cols] = hr
                sim_ref[d, rows, cols] = hi
                return ar * hr - ai * hi + inc_r, ar * hi + ai * hr + inc_i

            hr, hi = lax.fori_loop(0, SCAN_CHUNKS, step, (hre_ref[d, :, cols], him_ref[d, :, cols]),
                                   unroll=4)
            hre_ref[d, :, cols] = hr
            him_ref[d, :, cols] = hi

        for q in range(SSM_BLOCKS):
            cols = slice(q * bs, (q + 1) * bs)
            y = (_dot(sre_ref[d, :, cols].astype(BF16), ore_ref[d, q])
                 - _dot(sim_ref[d, :, cols].astype(BF16), oim_ref[d, q]))
            put(y_ref, q, y, False)


def _s5_scan(u_tm, f_re, f_im, o_re, o_im, toe, a_re, a_im, seq):
    rows_total = u_tm.shape[1]
    bsz = rows_total // seq
    assert bsz == SUBLANES and SCAN_T % SCAN_LC == 0 and CTX_LEN % SCAN_T == 0
    nt = seq // SCAN_T
    nc = CTX_LEN // SCAN_T
    chunked = (SSM_BLOCKS, seq // SCAN_LC, SCAN_LC, bsz, LANES)
    block = (SSM_BLOCKS, SCAN_CHUNKS, SCAN_LC, bsz, LANES)

    def bwd_tile(i):
        return jnp.where(i < nc, nc - 1 - i, nt - 1 - (i - nc))

    fwd_spec = pl.BlockSpec(block, lambda i: (0, i, 0, 0, 0))
    bwd_spec = pl.BlockSpec(block, lambda i: (0, bwd_tile(i), 0, 0, 0))
    full = lambda a: pl.BlockSpec(a.shape, lambda i: (0,) * a.ndim)
    y_sds = jax.ShapeDtypeStruct(chunked, F32)
    u5 = u_tm.reshape(chunked)
    yf, yb = pl.pallas_call(
        _s5_scan_kernel,
        grid=(nt,),
        in_specs=[fwd_spec, bwd_spec, full(f_re), full(f_im), full(o_re), full(o_im), full(toe),
                  full(a_re), full(a_im)],
        out_specs=[fwd_spec, bwd_spec],
        out_shape=[y_sds, y_sds],
        scratch_shapes=[pltpu.VMEM((2, SCAN_CHUNKS * bsz, SSM_STATES), F32),
                        pltpu.VMEM((2, SCAN_CHUNKS * bsz, SSM_STATES), F32),
                        pltpu.VMEM((2, bsz, SSM_STATES), F32),
                        pltpu.VMEM((2, bsz, SSM_STATES), F32)],
        compiler_params=_cparams("arbitrary"),
        name="s5_scan",
    )(u5, u5, f_re, f_im, o_re, o_im, toe, a_re, a_im)
    return yf.reshape(u_tm.shape), yb.reshape(u_tm.shape)


def _attention_kernel(q_ref, k_ref, v_ref, bias_ref, o_ref, *, grid_rows):
    i = pl.program_id(1)
    scale = NA_HEAD_DIM ** -0.5

    @pl.when(i == 0)
    def _():
        for h in range(NA_HEADS):
            hs = slice(h * NA_HEAD_DIM, (h + 1) * NA_HEAD_DIM)
            s = _dot_nt(q_ref[:, hs], k_ref[0:CTX_LEN, hs]) * scale
            p = jnp.exp(s - jnp.max(s, axis=-1, keepdims=True))
            o = _dot(p.astype(BF16), v_ref[0:CTX_LEN, hs]) / jnp.sum(p, axis=-1, keepdims=True)
            o_ref[:, hs] = o.astype(BF16)

    @pl.when(i > 0)
    def _():
        r_a = (i - 1) * NA_QROWS
        k_start = jnp.clip(r_a - WIN_H // 2, 0, grid_rows - NA_KROWS)
        start = pl.multiple_of(CTX_LEN + k_start * GRID_W, GRID_W)
        loc = pl.ds(start, NA_KROWS * GRID_W)
        for h in range(NA_HEADS):
            hs = slice(h * NA_HEAD_DIM, (h + 1) * NA_HEAD_DIM)
            qh = q_ref[:, hs]
            s_loc = _dot_nt(qh, k_ref[loc, hs]) * scale + bias_ref[h]
            s_ctx = _dot_nt(qh, k_ref[0:CTX_LEN, hs]) * scale
            m = jnp.maximum(jnp.max(s_loc, axis=-1, keepdims=True),
                            jnp.max(s_ctx, axis=-1, keepdims=True))
            p_loc = jnp.exp(s_loc - m)
            p_ctx = jnp.exp(s_ctx - m)
            l = jnp.sum(p_loc, axis=-1, keepdims=True) + jnp.sum(p_ctx, axis=-1, keepdims=True)
            o = _dot(p_loc.astype(BF16), v_ref[loc, hs]) + _dot(p_ctx.astype(BF16), v_ref[0:CTX_LEN, hs])
            o_ref[:, hs] = (o / l).astype(BF16)


def _na_tile_classes(grid_rows):
    return ((0, 0), (NA_QROWS, 0), (grid_rows - NA_QROWS, grid_rows - NA_KROWS))


def _na_valid(grid_rows):
    rq = np.arange(NA_QROWS)[:, None, None, None]
    cq = np.arange(GRID_W)[None, :, None, None]
    kr = np.arange(NA_KROWS)[None, None, :, None]
    kc = np.arange(GRID_W)[None, None, None, :]
    oks = []
    for r_a, k_start in _na_tile_classes(grid_rows):
        r0 = np.clip(r_a + rq - WIN_H // 2, 0, grid_rows - WIN_H)
        krow = k_start + kr
        cs = np.clip(cq - WIN_W // 2, 0, GRID_W - WIN_W)
        ok = (krow >= r0) & (krow < r0 + WIN_H) & (kc >= cs) & (kc < cs + WIN_W)
        oks.append(ok.reshape(TM, NA_KROWS * GRID_W))
    return np.stack(oks)


def _na_bias_tables(rpb, grid_rows):
    depth, heads, n_dr, n_dc = rpb.shape
    w = GRID_W
    lo = (w - 1) - (WIN_W - 1)
    v = jnp.pad(rpb, ((0, 0), (0, 0), (0, 0), (lo, 2 * w - lo - n_dc)))
    skew = jnp.broadcast_to(v[..., None, :], v.shape[:-1] + (w, 2 * w))
    skew = skew.reshape(v.shape[:-1] + (2 * w * w,))[..., :w * (2 * w - 1)]
    toep = skew.reshape(v.shape[:-1] + (w, 2 * w - 1))[..., w - 1:]
    margin = NA_QROWS
    toep = jnp.pad(toep, ((0, 0), (0, 0), (margin, margin), (0, 0), (0, 0)))
    tables = []
    for r_a, k_start in _na_tile_classes(grid_rows):
        blocks = []
        for rq in range(NA_QROWS):
            first = k_start - (r_a + rq) + (WIN_H - 1) + margin
            blocks.append(toep[:, :, first:first + NA_KROWS])
        t = jnp.stack(blocks, axis=2)
        t = jnp.transpose(t, (0, 1, 2, 4, 3, 5)).reshape(depth, heads, TM, NA_KROWS * w)
        tables.append(t)
    t = jnp.stack(tables, axis=1)
    return jnp.where(_na_valid(grid_rows)[None, :, None], t, NEG_INF)


def _attention(qkv, bias, grid_rows):
    bsz, s, _ = qkv.shape
    nt = s // TM
    n_keys = NA_KROWS * GRID_W

    def bias_class(b, i):
        return (jnp.where(i <= 1, 0, jnp.where(i == nt - 1, 2, 1)), 0, 0, 0)

    return pl.pallas_call(
        functools.partial(_attention_kernel, grid_rows=grid_rows),
        grid=(bsz, nt),
        in_specs=[pl.BlockSpec((None, TM, NA_W), lambda b, i: (b, i, 0)),
                  pl.BlockSpec((None, s, NA_W), lambda b, i: (b, 0, 1)),
                  pl.BlockSpec((None, s, NA_W), lambda b, i: (b, 0, 2)),
                  pl.BlockSpec((None, NA_HEADS, TM, n_keys), bias_class)],
        out_specs=pl.BlockSpec((None, TM, NA_W), lambda b, i: (b, i, 0)),
        out_shape=jax.ShapeDtypeStruct((bsz, s, NA_W), BF16),
        compiler_params=_cparams("parallel", "arbitrary"),
        name="attention",
    )(qkv, qkv, qkv, bias)


def _gelu_tanh(x):
    return 0.5 * x * (1.0 + jnp.tanh(math.sqrt(2.0 / math.pi) * (x + 0.044715 * (x * x * x))))


def _merge_kernel(x_ref, cg_ref, cprev_ref, cnext_ref, u_ref, yf_ref, yb_ref, o_ref, gate_ref,
                  g1_ref, cw_ref, cout_ref, d_ref, glua_ref, glub_ref, naout_ref, wout_ref,
                  out_ref, *, n_tiles):
    i = pl.program_id(0)
    rows = _batch_rows(pl.program_id(1))

    def slab(ref):
        return jnp.concatenate([ref[q, rows, :] for q in range(SSM_BLOCKS)], axis=1)

    def gated_input(ref):
        return ref[:, 2 * CONV_W:3 * CONV_W].astype(F32) * ref[:, 0:CONV_W].astype(F32)

    z = gated_input(cg_ref)
    has_prev = (i >= 2).astype(F32)
    has_next = jnp.logical_and(i != 0, i != n_tiles - 1).astype(F32)
    z_before = gated_input(cprev_ref)[BF16_ROWS - 1:BF16_ROWS] * has_prev
    z_after = gated_input(cnext_ref)[0:1] * has_next
    row = lax.broadcasted_iota(jnp.int32, z.shape, 0)
    z_prev = jnp.where(row == 0, z_before, pltpu.roll(z, 1, 0))
    z_next = jnp.where(row == TM - 1, z_after, pltpu.roll(z, TM - 1, 0))
    conv = z_prev * cw_ref[0:1, :] + z * cw_ref[1:2, :] + z_next * cw_ref[2:3, :]
    a_pre = cg_ref[:, CONV_W:2 * CONV_W].astype(F32) * conv
    ya = _dot(a_pre.astype(BF16), cout_ref[...])

    y = slab(yf_ref) + slab(yb_ref) + d_ref[...] * slab(u_ref)
    g = _gelu_tanh(y).astype(BF16)
    yb = _dot(g, glua_ref[...]) * jax.nn.sigmoid(_dot(g, glub_ref[...]))

    yc = _dot(o_ref[...], naout_ref[...])

    mix = (jax.nn.sigmoid(gate_ref[:, 0:D_MODEL].astype(F32)) * ya
           + jax.nn.sigmoid(gate_ref[:, D_MODEL:2 * D_MODEL].astype(F32)) * yb
           + jax.nn.sigmoid(gate_ref[:, 2 * D_MODEL:3 * D_MODEL].astype(F32)) * yc)
    out_ref[...] = x_ref[...] + g1_ref[...] * _dot(mix.astype(BF16), wout_ref[...])


def _merge(xc, conv_g, u_tm, yf, yb, attn_o, gates, gate1, conv_w, conv_out, s5_d, glu_a, glu_b,
           na_out, w_out):
    bsz, s, _ = xc.shape
    nt = s // TM
    halo_blocks = s // BF16_ROWS
    per_tile = TM // BF16_ROWS
    tok = lambda width: _tok_spec(width, batch_major=False)
    tm_spec = _slab_spec()
    prev_spec = pl.BlockSpec((None, BF16_ROWS, 3 * CONV_W),
                             lambda i, b: (b, jnp.maximum(i * per_tile - 1, 0), 0))
    next_spec = pl.BlockSpec((None, BF16_ROWS, 3 * CONV_W),
                             lambda i, b: (b, jnp.minimum((i + 1) * per_tile, halo_blocks - 1), 0))
    return pl.pallas_call(
        functools.partial(_merge_kernel, n_tiles=nt),
        grid=(nt, bsz),
        in_specs=[tok(D_MODEL), tok(3 * CONV_W), prev_spec, next_spec, tm_spec, tm_spec, tm_spec,
                  tok(NA_W), tok(3 * D_MODEL), _mod_spec(batch_major=False),
                  _const_spec((3, CONV_W)), _const_spec((CONV_W, D_MODEL)), _const_spec((1, SSM_W)),
                  _const_spec((SSM_W, D_MODEL)), _const_spec((SSM_W, D_MODEL)),
                  _const_spec((NA_W, D_MODEL)), _const_spec((D_MODEL, D_MODEL))],
        out_specs=tok(D_MODEL),
        out_shape=jax.ShapeDtypeStruct(xc.shape, F32),
        compiler_params=_cparams("parallel", "arbitrary"),
        name="merge",
    )(xc, conv_g, conv_g, conv_g, u_tm, yf, yb, attn_o, gates, gate1, conv_w, conv_out, s5_d,
      glu_a, glu_b, na_out, w_out)


def _mlp_kernel(x_ref, sh_ref, sc_ref, gt_ref, g_ref, w1_ref, w2_ref, fg_ref, out_ref, *, final):
    x = x_ref[...]
    h = (_rms(x, g_ref[...]) * (1.0 + sc_ref[...]) + sh_ref[...]).astype(BF16)
    acc = jnp.zeros_like(x)
    for c in range(MLP_HIDDEN // D_MODEL):
        cols = slice(c * D_MODEL, (c + 1) * D_MODEL)
        a = jnp.maximum(_dot(h, w1_ref[:, cols]), 0.0)
        acc = acc + _dot((a * a).astype(BF16), w2_ref[cols, :])
    y = x + gt_ref[...] * acc
    if final:
        y = _rms(y, fg_ref[...])
    out_ref[...] = y


def _mlp(xc, shift, scale, gate, g, w1, w2, final_g, final):
    bsz, s, _ = xc.shape
    nt = s // TM
    tok = _tok_spec(D_MODEL)
    return pl.pallas_call(
        functools.partial(_mlp_kernel, final=final),
        grid=(bsz, nt),
        in_specs=[tok, _mod_spec(), _mod_spec(), _mod_spec(), _const_spec((1, D_MODEL)),
                  _const_spec((D_MODEL, MLP_HIDDEN)), _const_spec((MLP_HIDDEN, D_MODEL)),
                  _const_spec((1, D_MODEL))],
        out_specs=tok,
        out_shape=jax.ShapeDtypeStruct(xc.shape, F32),
        compiler_params=_cparams("parallel", "parallel"),
        name="mlp",
    )(xc, shift, scale, gate, g, w1, w2, final_g)


def kernel(x, c, ctx, c_ctx, w_mod, b_mod, norm1_g, w_in, conv_w, conv_out, s5_lam_re, s5_lam_im,
           s5_log_step, s5_b_re, s5_b_im, s5_c_re, s5_c_im, s5_d, s5_glu_a, s5_glu_b, na_rpb,
           na_out, w_out, norm2_g, mlp_w1, mlp_w2, final_norm_g):
    bsz, seq, _ = x.shape
    depth = w_mod.shape[0]
    assert ctx.shape[1] == CTX_LEN == TM and seq % TM == 0 and bsz == SUBLANES
    grid_rows = seq // GRID_W
    assert grid_rows >= NA_KROWS and grid_rows % NA_QROWS == 0
    s = CTX_LEN + seq

    cond_rows = 2 * SUBLANES
    cond = jnp.zeros((cond_rows, D_MODEL), F32).at[:bsz].set(c).at[bsz].set(c_ctx)
    mods = _adaln(cond, w_mod, b_mod).reshape(depth, cond_rows, N_MOD, D_MODEL)
    lat = mods[:, :bsz]
    cx = jnp.broadcast_to(mods[:, bsz:bsz + 1], lat.shape)
    mod = jnp.stack([cx, lat], axis=2)[:, :, :, :, None, :]

    s5_mats = _s5_matrices(s5_lam_re, s5_lam_im, s5_log_step, s5_b_re, s5_b_im, s5_c_re, s5_c_im)

    bf = lambda w: w.astype(BF16)
    w_in_b, conv_out_b, glu_a_b, glu_b_b = bf(w_in), bf(conv_out), bf(s5_glu_a), bf(s5_glu_b)
    na_out_b, w_out_b, w1_b, w2_b = bf(na_out), bf(w_out), bf(mlp_w1), bf(mlp_w2)

    na_bias = _na_bias_tables(na_rpb, grid_rows)

    xc = jnp.concatenate([ctx, x], axis=1)
    row = lambda v: v.reshape(1, -1)
    for l in range(depth):
        m = mod[l]
        conv_g, u_tm, qkv, gates = _in_proj(xc, m[:, :, 0], m[:, :, 1], row(norm1_g[l]), w_in_b[l])
        yf, yb = _s5_scan(u_tm, *(mat[l] for mat in s5_mats), s)
        attn_o = _attention(qkv, na_bias[l], grid_rows)
        xc = _merge(xc, conv_g, u_tm, yf, yb, attn_o, gates, m[:, :, 2], conv_w[l], conv_out_b[l],
                    row(s5_d[l]), glu_a_b[l], glu_b_b[l], na_out_b[l], w_out_b[l])
        xc = _mlp(xc, m[:, :, 3], m[:, :, 4], m[:, :, 5], row(norm2_g[l]), w1_b[l], w2_b[l],
                  row(final_norm_g), final=(l == depth - 1))
    return xc[:, CTX_LEN:]
```

```python
import functools
import math

import numpy as np
import jax
import jax.numpy as jnp
from jax import lax
from jax.experimental import pallas as pl
from jax.experimental.pallas import tpu as pltpu

D_MODEL = 1024
CTX_LEN = 256
GRID_W = 64
N_MOD = 6
CONV_W = 512
SSM_W = 512
SSM_GROUP = 16
SSM_GROUPS = SSM_W // SSM_GROUP
SSM_STATE = 64
NA_HEADS = 8
NA_HEAD_DIM = 64
NA_W = NA_HEADS * NA_HEAD_DIM
WIN_H = 8
WIN_W = 16
MLP_HIDDEN = 4 * D_MODEL
IN_SIZES = (CONV_W, CONV_W, CONV_W, SSM_W, NA_W, NA_W, NA_W, D_MODEL, D_MODEL, D_MODEL)
IN_OFF = tuple(sum(IN_SIZES[:i]) for i in range(len(IN_SIZES) + 1))
IN_PROJ_W = IN_OFF[-1]
RMS_EPS = 1e-6
NEG_INF = -1e30
S5_MIN_DECAY = 1e-4
LOG2_E = math.log2(math.e)

F32 = jnp.float32
BF16 = jnp.bfloat16

LANES = 128
SUBLANES = 8
BF16_ROWS = 16
TM = 256
SCAN_LC = 2
SCAN_T = 64
SCAN_CHUNKS = SCAN_T // SCAN_LC
NA_QROWS = TM // GRID_W
NA_KROWS = NA_QROWS + WIN_H
NA_KEY_BLOCK = TM
SSM_LANE_GROUPS = LANES // SSM_GROUP
SSM_BLOCKS = SSM_W // LANES
SSM_BLOCK_STATES = SSM_LANE_GROUPS * SSM_STATE
SSM_STATES = SSM_GROUPS * SSM_STATE
VMEM_LIMIT = 56 * 1024 * 1024


def _cparams(*sem):
    return pltpu.CompilerParams(dimension_semantics=sem, vmem_limit_bytes=VMEM_LIMIT)


def _rms(x, g):
    return x * lax.rsqrt(jnp.mean(x * x, axis=-1, keepdims=True) + RMS_EPS) * g


def _dot(a, b):
    return jnp.dot(a, b, preferred_element_type=F32)


def _dot_nt(a, b):
    return lax.dot_general(a, b, (((1,), (1,)), ((), ())), preferred_element_type=F32)


def _adaln_kernel(c_ref, w_ref, b_ref, o_ref):
    c = c_ref[...]
    s = c * jax.nn.sigmoid(c)
    o_ref[...] = jnp.dot(s, w_ref[...], preferred_element_type=F32,
                         precision=lax.Precision.HIGHEST) + b_ref[...]


def _adaln(cond, w_mod, b_mod):
    depth = w_mod.shape[0]
    rows = cond.shape[0]
    n_tiles = (N_MOD * D_MODEL) // D_MODEL
    return pl.pallas_call(
        _adaln_kernel,
        grid=(depth, n_tiles),
        in_specs=[
            pl.BlockSpec((rows, D_MODEL), lambda l, j: (0, 0)),
            pl.BlockSpec((None, D_MODEL, D_MODEL), lambda l, j: (l, 0, j)),
            pl.BlockSpec((None, 1, D_MODEL), lambda l, j: (l, 0, j)),
        ],
        out_specs=pl.BlockSpec((None, rows, D_MODEL), lambda l, j: (l, 0, j)),
        out_shape=jax.ShapeDtypeStruct((depth, rows, N_MOD * D_MODEL), F32),
        compiler_params=_cparams("parallel", "parallel"),
        name="adaln",
    )(cond, w_mod, b_mod.reshape(depth, 1, N_MOD * D_MODEL))


def _s5_prep_kernel(lre_ref, lim_ref, ls_ref, bre_ref, bim_ref, cre_ref, cim_ref,
                    are_ref, aim_ref, fre_ref, fim_ref, ore_ref, oim_ref):
    lr = jnp.minimum(lre_ref[...], -S5_MIN_DECAY)
    li = lim_ref[...]
    dt = jnp.exp(ls_ref[...])
    xr = lr * dt
    xi = li * dt

    def power(k):
        e = jnp.exp(k * xr)
        return e * jnp.cos(k * xi), e * jnp.sin(k * xi)

    ar, ai = power(1.0)
    nr = ar - 1.0
    den = lr * lr + li * li
    cr = (nr * lr + ai * li) / den
    ci = (ai * lr - nr * li) / den
    bre = bre_ref[...]
    bim = bim_ref[...]
    bbr = cr * bre - ci * bim
    bbi = cr * bim + ci * bre
    cre = cre_ref[...]
    cim = cim_ref[...]
    fre_ref[:, 0] = bbr
    fim_ref[:, 0] = bbi
    ore_ref[:, 0] = cre
    oim_ref[:, 0] = cim
    for k in range(1, SCAN_LC + 1):
        pr, pi = power(float(k))
        if k < SCAN_LC:
            fre_ref[:, k] = pr * bbr - pi * bbi
            fim_ref[:, k] = pr * bbi + pi * bbr
        ore_ref[:, k] = pr * cre - pi * cim
        oim_ref[:, k] = pr * cim + pi * cre
    are_ref[...], aim_ref[...] = power(float(SCAN_LC))


def _s5_prep(lam_re, lam_im, log_step, b_re, b_im, c_re, c_im):
    depth = lam_re.shape[0]
    rows = depth * 2 * SSM_GROUPS
    rb = SSM_GROUPS
    lre = lam_re.reshape(rows, 1, SSM_STATE)
    lim = lam_im.reshape(rows, 1, SSM_STATE)
    ls = jnp.broadcast_to(log_step.reshape(rows, 1, 1), (rows, 1, SSM_STATE))
    bre = jnp.swapaxes(b_re.reshape(rows, SSM_STATE, SSM_GROUP), 1, 2)
    bim = jnp.swapaxes(b_im.reshape(rows, SSM_STATE, SSM_GROUP), 1, 2)
    cre = c_re.reshape(rows, SSM_GROUP, SSM_STATE)
    cim = c_im.reshape(rows, SSM_GROUP, SSM_STATE)
    vec = pl.BlockSpec((rb, 1, SSM_STATE), lambda r: (r, 0, 0))
    mat = pl.BlockSpec((rb, SSM_GROUP, SSM_STATE), lambda r: (r, 0, 0))
    pw = lambda n: pl.BlockSpec((rb, n, SSM_GROUP, SSM_STATE), lambda r: (r, 0, 0, 0))
    v2 = jax.ShapeDtypeStruct((rows, 1, SSM_STATE), F32)
    v4 = lambda n: jax.ShapeDtypeStruct((rows, n, SSM_GROUP, SSM_STATE), F32)
    return pl.pallas_call(
        _s5_prep_kernel,
        grid=(rows // rb,),
        in_specs=[vec, vec, vec, mat, mat, mat, mat],
        out_specs=[vec, vec, pw(SCAN_LC), pw(SCAN_LC), pw(SCAN_LC + 1), pw(SCAN_LC + 1)],
        out_shape=[v2, v2, v4(SCAN_LC), v4(SCAN_LC), v4(SCAN_LC + 1), v4(SCAN_LC + 1)],
        compiler_params=pltpu.CompilerParams(dimension_semantics=("parallel",)),
        name="s5_prep",
    )(lre, lim, ls, bre, bim, cre, cim)


def _s5_toe_kernel(bre_ref, bim_ref, ore_ref, oim_ref, k_ref):
    hi = lax.Precision.HIGHEST
    for k in range(SCAN_LC):
        k_ref[k] = (jnp.dot(bre_ref[...], ore_ref[k], preferred_element_type=F32, precision=hi)
                    - jnp.dot(bim_ref[...], oim_ref[k], preferred_element_type=F32, precision=hi))


def _s5_toe(bd_re, bd_im, oc_re, oc_im):
    nb = bd_re.shape[0]
    bspec = pl.BlockSpec((None, LANES, SSM_BLOCK_STATES), lambda r: (r, 0, 0))
    ospec = pl.BlockSpec((None, SCAN_LC, SSM_BLOCK_STATES, LANES), lambda r: (r, 0, 0, 0))
    return pl.pallas_call(
        _s5_toe_kernel,
        grid=(nb,),
        in_specs=[bspec, bspec, ospec, ospec],
        out_specs=pl.BlockSpec((None, SCAN_LC, LANES, LANES), lambda r: (r, 0, 0, 0)),
        out_shape=jax.ShapeDtypeStruct((nb, SCAN_LC, LANES, LANES), F32),
        compiler_params=pltpu.CompilerParams(dimension_semantics=("parallel",)),
        name="s5_toe",
    )(bd_re, bd_im, oc_re, oc_im)


def _block_diag(m):
    ng, a, b = m.shape[-3:]
    eye = jnp.eye(ng, dtype=m.dtype)
    out = jnp.einsum("...gab,gh->...gahb", m, eye)
    return out.reshape(m.shape[:-3] + (ng * a, ng * b))


def _s5_matrices(lam_re, lam_im, log_step, b_re, b_im, c_re, c_im):
    depth = lam_re.shape[0]
    lc = SCAN_LC
    a_re, a_im, f_re, f_im, o_re, o_im = _s5_prep(lam_re, lam_im, log_step, b_re, b_im, c_re, c_im)
    lead = (depth, 2, SSM_BLOCKS, SSM_LANE_GROUPS)

    def per_step(x, n):
        return jnp.moveaxis(x.reshape(lead + (n, SSM_GROUP, SSM_STATE)), 4, 3)

    def mirror_bwd(x, axis):
        return jnp.stack([x[:, 0], jnp.flip(x[:, 1], axis=axis - 1)], axis=1)

    def mirror_fwd(x, axis):
        return jnp.stack([jnp.flip(x[:, 0], axis=axis - 1), x[:, 1]], axis=1)

    def fold_mat(f):
        m = _block_diag(mirror_fwd(per_step(f, lc), 3))
        return m.reshape(depth, 2, SSM_BLOCKS, lc * LANES, SSM_BLOCK_STATES).astype(BF16)

    def out_blocks(o):
        return _block_diag(jnp.swapaxes(per_step(o, lc + 1), -1, -2))

    def out_mat(ob):
        m = mirror_bwd(ob[:, :, :, 1:], 3)
        m = jnp.moveaxis(m, 3, 4)
        return m.reshape(depth, 2, SSM_BLOCKS, SSM_BLOCK_STATES, lc * LANES).astype(BF16)

    ob_re, ob_im = out_blocks(o_re), out_blocks(o_im)
    nb = depth * 2 * SSM_BLOCKS
    bd = lambda f: _block_diag(per_step(f, lc)[:, :, :, 0]).reshape(nb, LANES, SSM_BLOCK_STATES)
    oc = lambda ob: ob[:, :, :, :lc].reshape(nb, lc, SSM_BLOCK_STATES, LANES)
    kk = _s5_toe(bd(f_re), bd(f_im), oc(ob_re), oc(ob_im))
    kk = kk.reshape(depth, 2, SSM_BLOCKS, lc, LANES, LANES)
    zero = jnp.zeros((depth, SSM_BLOCKS, LANES, LANES), F32)
    toes = []
    for d in range(2):
        rows = []
        for j_in in range(lc):
            lag = [(j_out - j_in) if d == 0 else (j_in - j_out) for j_out in range(lc)]
            rows.append(jnp.concatenate([kk[:, d, :, g] if g >= 0 else zero for g in lag], axis=-1))
        toes.append(jnp.concatenate(rows, axis=-2))
    toe = jnp.stack(toes, axis=1).astype(BF16)

    a_shape = (depth, 2, 1, SSM_STATES)
    a_re = jnp.broadcast_to(a_re.reshape(a_shape), (depth, 2, SUBLANES, SSM_STATES))
    a_im = jnp.broadcast_to(a_im.reshape(a_shape), (depth, 2, SUBLANES, SSM_STATES))
    return fold_mat(f_re), fold_mat(f_im), out_mat(ob_re), out_mat(ob_im), toe, a_re, a_im


def _batch_rows(b):
    return pl.ds(b, TM, stride=SUBLANES)


def _in_proj_kernel(x_ref, sh_ref, sc_ref, g_ref, w_ref, wqv_ref, conv_ref, u_ref, qt_ref, k_ref,
                    vt_ref, gate_ref):
    h = _rms(x_ref[...], g_ref[...]) * (1.0 + sc_ref[...]) + sh_ref[...]
    hb = h.astype(BF16)
    conv_ref[...] = _dot(hb, w_ref[:, IN_OFF[0]:IN_OFF[3]]).astype(BF16)
    u = _dot(hb, w_ref[:, IN_OFF[3]:IN_OFF[4]])
    rows = _batch_rows(pl.program_id(1))
    for q in range(SSM_BLOCKS):
        u_ref[q, rows, :] = u[:, q * LANES:(q + 1) * LANES]
    qt_ref[...] = _dot_nt(wqv_ref[0:NA_W, :], hb).astype(BF16)
    k_ref[...] = _dot(hb, w_ref[:, IN_OFF[5]:IN_OFF[6]]).astype(BF16)
    vt_ref[...] = _dot_nt(wqv_ref[NA_W:2 * NA_W, :], hb).astype(BF16)
    gate_ref[...] = _dot(hb, w_ref[:, IN_OFF[7]:IN_OFF[10]]).astype(BF16)


def _grid_bi(batch_major):
    return (lambda b, i: (b, i)) if batch_major else (lambda i, b: (b, i))


def _tok_spec(width, batch_major=True):
    bi = _grid_bi(batch_major)
    return pl.BlockSpec((None, TM, width), lambda *g: bi(*g) + (0,))


def _mod_spec(batch_major=True):
    bi = _grid_bi(batch_major)

    def index(*g):
        b, i = bi(*g)
        return (b, jnp.minimum(i, 1), 0, 0)

    return pl.BlockSpec((None, None, 1, D_MODEL), index)


def _const_spec(shape):
    nd = len(shape)
    return pl.BlockSpec(shape, lambda *g: (0,) * nd)


def _slab_spec():
    return pl.BlockSpec((SSM_BLOCKS, TM * SUBLANES, LANES), lambda i, b: (0, i, 0))


def _fm_spec(batch_major=True):
    bi = _grid_bi(batch_major)
    return pl.BlockSpec((None, None, NA_W, TM), lambda *g: bi(*g) + (0, 0))


def _in_proj(xc, shift, scale, g, w, w_qv_t):
    bsz, s, _ = xc.shape
    nt = s // TM
    tok = lambda width: _tok_spec(width, batch_major=False)
    mod = _mod_spec(batch_major=False)
    fm = _fm_spec(batch_major=False)
    fm_sds = jax.ShapeDtypeStruct((bsz, nt, NA_W, TM), BF16)
    return pl.pallas_call(
        _in_proj_kernel,
        grid=(nt, bsz),
        in_specs=[tok(D_MODEL), mod, mod, _const_spec((1, D_MODEL)),
                  _const_spec((D_MODEL, IN_PROJ_W)), _const_spec((2 * NA_W, D_MODEL))],
        out_specs=[tok(3 * CONV_W), _slab_spec(), fm, tok(NA_W), fm, tok(3 * D_MODEL)],
        out_shape=[jax.ShapeDtypeStruct((bsz, s, 3 * CONV_W), BF16),
                   jax.ShapeDtypeStruct((SSM_BLOCKS, s * bsz, LANES), F32),
                   fm_sds,
                   jax.ShapeDtypeStruct((bsz, s, NA_W), BF16),
                   fm_sds,
                   jax.ShapeDtypeStruct((bsz, s, 3 * D_MODEL), BF16)],
        compiler_params=_cparams("parallel", "arbitrary"),
        name="in_proj",
    )(xc, shift, scale, g, w, w_qv_t)


def _s5_scan_kernel(uf_ref, ub_ref, fre_ref, fim_ref, ore_ref, oim_ref, toe_ref, are_ref, aim_ref,
                    yf_ref, yb_ref, sre_ref, sim_ref, hre_ref, him_ref):
    @pl.when(pl.program_id(0) == 0)
    def _():
        hre_ref[...] = jnp.zeros_like(hre_ref)
        him_ref[...] = jnp.zeros_like(him_ref)

    bs = SSM_BLOCK_STATES
    rows_c = SCAN_CHUNKS * SUBLANES

    def put(y_ref, q, y, first):
        for j in range(SCAN_LC):
            yj = y[:, j * LANES:(j + 1) * LANES].reshape(SCAN_CHUNKS, SUBLANES, LANES)
            y_ref[q, :, j] = yj if first else y_ref[q, :, j] + yj

    for d, (u_ref, y_ref) in enumerate(((uf_ref, yf_ref), (ub_ref, yb_ref))):
        for q in range(SSM_BLOCKS):
            uc = jnp.concatenate([u_ref[q, :, j].reshape(rows_c, LANES) for j in range(SCAN_LC)],
                                 axis=1).astype(BF16)
            sre_ref[d, :, q * bs:(q + 1) * bs] = _dot(uc, fre_ref[d, q])
            sim_ref[d, :, q * bs:(q + 1) * bs] = _dot(uc, fim_ref[d, q])
            put(y_ref, q, _dot(uc, toe_ref[d, q]), True)

        for q in range(SSM_BLOCKS):
            cols = slice(q * bs, (q + 1) * bs)
            ar = are_ref[d, :, cols]
            ai = aim_ref[d, :, cols]

            def step(c, carry, d=d, cols=cols, ar=ar, ai=ai):
                hr, hi = carry
                cc = c if d == 0 else SCAN_CHUNKS - 1 - c
                rows = pl.ds(pl.multiple_of(cc * SUBLANES, SUBLANES), SUBLANES)
                inc_r = sre_ref[d, rows, cols]
                inc_i = sim_ref[d, rows, cols]
                sre_ref[d, rows, cols] = hr
                sim_ref[d, rows, cols] = hi
                return ar * hr - ai * hi + inc_r, ar * hi + ai * hr + inc_i

            hr, hi = lax.fori_loop(0, SCAN_CHUNKS, step, (hre_ref[d, :, cols], him_ref[d, :, cols]),
                                   unroll=4)
            hre_ref[d, :, cols] = hr
            him_ref[d, :, cols] = hi

        for q in range(SSM_BLOCKS):
            cols = slice(q * bs, (q + 1) * bs)
            y = (_dot(sre_ref[d, :, cols].astype(BF16), ore_ref[d, q])
                 - _dot(sim_ref[d, :, cols].astype(BF16), oim_ref[d, q]))
            put(y_ref, q, y, False)


def _s5_scan(u_tm, f_re, f_im, o_re, o_im, toe, a_re, a_im, seq):
    rows_total = u_tm.shape[1]
    bsz = rows_total // seq
    assert bsz == SUBLANES and SCAN_T % SCAN_LC == 0 and CTX_LEN % SCAN_T == 0
    nt = seq // SCAN_T
    nc = CTX_LEN // SCAN_T
    chunked = (SSM_BLOCKS, seq // SCAN_LC, SCAN_LC, bsz, LANES)
    block = (SSM_BLOCKS, SCAN_CHUNKS, SCAN_LC, bsz, LANES)

    def bwd_tile(i):
        return jnp.where(i < nc, nc - 1 - i, nt - 1 - (i - nc))

    fwd_spec = pl.BlockSpec(block, lambda i: (0, i, 0, 0, 0))
    bwd_spec = pl.BlockSpec(block, lambda i: (0, bwd_tile(i), 0, 0, 0))
    full = lambda a: pl.BlockSpec(a.shape, lambda i: (0,) * a.ndim)
    y_sds = jax.ShapeDtypeStruct(chunked, F32)
    u5 = u_tm.reshape(chunked)
    yf, yb = pl.pallas_call(
        _s5_scan_kernel,
        grid=(nt,),
        in_specs=[fwd_spec, bwd_spec, full(f_re), full(f_im), full(o_re), full(o_im), full(toe),
                  full(a_re), full(a_im)],
        out_specs=[fwd_spec, bwd_spec],
        out_shape=[y_sds, y_sds],
        scratch_shapes=[pltpu.VMEM((2, SCAN_CHUNKS * bsz, SSM_STATES), F32),
                        pltpu.VMEM((2, SCAN_CHUNKS * bsz, SSM_STATES), F32),
                        pltpu.VMEM((2, bsz, SSM_STATES), F32),
                        pltpu.VMEM((2, bsz, SSM_STATES), F32)],
        compiler_params=_cparams("arbitrary"),
        name="s5_scan",
    )(u5, u5, f_re, f_im, o_re, o_im, toe, a_re, a_im)
    return yf.reshape(u_tm.shape), yb.reshape(u_tm.shape)


def _attention_kernel(qt_ref, k_ref, vt_ref, bias_ref, o_ref, s_ref, *, grid_rows):
    i = pl.program_id(1)
    hd = NA_HEAD_DIM
    qt = (qt_ref[...].astype(F32) * (hd ** -0.5 * LOG2_E)).astype(BF16)
    zeros = jnp.zeros((hd, TM), BF16)
    ones = jnp.ones((BF16_ROWS, NA_KEY_BLOCK), BF16)
    ctx_rows = slice(0, CTX_LEN)

    def pair_cols(h):
        return slice((h // 2) * LANES, (h // 2 + 1) * LANES)

    def padded_qt(h):
        qh = qt[h * hd:(h + 1) * hd]
        return jnp.concatenate([qh, zeros] if h % 2 == 0 else [zeros, qh], axis=0)

    @pl.when(i == 0)
    def _():
        for h in range(NA_HEADS):
            s = _dot(k_ref[ctx_rows, pair_cols(h)], padded_qt(h))
            p = jnp.exp2(s - jnp.max(s, axis=0, keepdims=True)).astype(BF16)
            ol = _dot(jnp.concatenate([vt_ref[0, h * hd:(h + 1) * hd, :], ones], axis=0), p)
            o_ref[h * hd:(h + 1) * hd, :] = (ol[0:hd] / ol[hd:hd + 1]).astype(BF16)

    @pl.when(i > 0)
    def _():
        r_a = (i - 1) * NA_QROWS
        k_start = jnp.clip(r_a - WIN_H // 2, 0, grid_rows - NA_KROWS)
        start = pl.multiple_of(CTX_LEN + k_start * GRID_W, NA_KEY_BLOCK)
        kb = NA_KEY_BLOCK
        blk0 = start // kb
        n_loc = NA_KROWS * GRID_W // kb
        n_all = n_loc + 1
        k_rows = [pl.ds(start + j * kb, kb) for j in range(n_loc)] + [ctx_rows]
        v_blocks = [blk0 + j for j in range(n_loc)] + [0]
        rhs = [padded_qt(h) for h in range(NA_HEADS)]

        def scores(h, j):
            s = _dot(k_ref[k_rows[j], pair_cols(h)], rhs[h])
            if j < n_loc:
                s = s + bias_ref[h, j * kb:(j + 1) * kb, :]
            s_ref[h % 2, j * kb:(j + 1) * kb] = s
            return jnp.max(s, axis=0, keepdims=True)

        m = functools.reduce(jnp.maximum, [scores(0, j) for j in range(n_all)])
        for h in range(NA_HEADS):
            ol = None
            next_max = []
            for j in range(n_all):
                if h + 1 < NA_HEADS:
                    next_max.append(scores(h + 1, j))
                p = jnp.exp2(s_ref[h % 2, j * kb:(j + 1) * kb] - m).astype(BF16)
                lhs = jnp.concatenate([vt_ref[v_blocks[j], h * hd:(h + 1) * hd, :], ones], axis=0)
                part = _dot(lhs, p)
                ol = part if ol is None else ol + part
            o_ref[h * hd:(h + 1) * hd, :] = (ol[0:hd] / ol[hd:hd + 1]).astype(BF16)
            if next_max:
                m = functools.reduce(jnp.maximum, next_max)


def _na_tile_classes(grid_rows):
    return ((0, 0), (NA_QROWS, 0), (grid_rows - NA_QROWS, grid_rows - NA_KROWS))


def _na_valid(grid_rows):
    kr = np.arange(NA_KROWS)[:, None, None, None]
    kc = np.arange(GRID_W)[None, :, None, None]
    rq = np.arange(NA_QROWS)[None, None, :, None]
    cq = np.arange(GRID_W)[None, None, None, :]
    oks = []
    for r_a, k_start in _na_tile_classes(grid_rows):
        r0 = np.clip(r_a + rq - WIN_H // 2, 0, grid_rows - WIN_H)
        krow = k_start + kr
        cs = np.clip(cq - WIN_W // 2, 0, GRID_W - WIN_W)
        ok = (krow >= r0) & (krow < r0 + WIN_H) & (kc >= cs) & (kc < cs + WIN_W)
        oks.append(ok.reshape(NA_KROWS * GRID_W, TM))
    return np.stack(oks)


def _na_bias_tables(rpb, grid_rows):
    depth, heads, n_dr, n_dc = rpb.shape
    w = GRID_W
    rpb = rpb * LOG2_E
    lo = (w - 1) - (WIN_W - 1)
    v = jnp.pad(rpb, ((0, 0), (0, 0), (0, 0), (lo, 2 * w - lo - n_dc)))
    skew = jnp.broadcast_to(v[..., None, :], v.shape[:-1] + (w, 2 * w))
    skew = skew.reshape(v.shape[:-1] + (2 * w * w,))[..., :w * (2 * w - 1)]
    toep = skew.reshape(v.shape[:-1] + (w, 2 * w - 1))[..., w - 1:]
    margin = NA_QROWS
    toep = jnp.pad(toep, ((0, 0), (0, 0), (margin, margin), (0, 0), (0, 0)))
    tables = []
    for r_a, k_start in _na_tile_classes(grid_rows):
        blocks = []
        for rq in range(NA_QROWS):
            first = k_start - (r_a + rq) + (WIN_H - 1) + margin
            blocks.append(toep[:, :, first:first + NA_KROWS])
        t = jnp.stack(blocks, axis=2)
        t = jnp.transpose(t, (0, 1, 3, 5, 2, 4)).reshape(depth, heads, NA_KROWS * w, TM)
        tables.append(t)
    t = jnp.stack(tables, axis=1)
    return jnp.where(_na_valid(grid_rows)[None, :, None], t, NEG_INF)


def _attention(qt, k, vt, bias, grid_rows):
    bsz, s, _ = k.shape
    nt = s // TM
    n_keys = NA_KROWS * GRID_W

    def bias_class(b, i):
        return (jnp.where(i <= 1, 0, jnp.where(i == nt - 1, 2, 1)), 0, 0, 0)

    return pl.pallas_call(
        functools.partial(_attention_kernel, grid_rows=grid_rows),
        grid=(bsz, nt),
        in_specs=[_fm_spec(),
                  pl.BlockSpec((None, s, NA_W), lambda b, i: (b, 0, 0)),
                  pl.BlockSpec((None, nt, NA_W, TM), lambda b, i: (b, 0, 0, 0)),
                  pl.BlockSpec((None, NA_HEADS, n_keys, TM), bias_class)],
        out_specs=_fm_spec(),
        out_shape=jax.ShapeDtypeStruct((bsz, nt, NA_W, TM), BF16),
        scratch_shapes=[pltpu.VMEM((2, n_keys + CTX_LEN, TM), F32)],
        compiler_params=_cparams("parallel", "arbitrary"),
        name="attention",
    )(qt, k, vt, bias)


def _gelu_tanh(x):
    return 0.5 * x * (1.0 + jnp.tanh(math.sqrt(2.0 / math.pi) * (x + 0.044715 * (x * x * x))))


def _sigmoid(x):
    return 0.5 * jnp.tanh(0.5 * x) + 0.5


def _merge_kernel(x_ref, cg_ref, cprev_ref, cnext_ref, u_ref, yf_ref, yb_ref, o_ref, gate_ref,
                  g1_ref, cw_ref, cout_ref, d_ref, glua_ref, glub_ref, naout_ref, wout_ref,
                  out_ref, *, n_tiles):
    i = pl.program_id(0)
    rows = _batch_rows(pl.program_id(1))

    def slab(ref):
        return jnp.concatenate([ref[q, rows, :] for q in range(SSM_BLOCKS)], axis=1)

    def gated_input(ref):
        return ref[:, 2 * CONV_W:3 * CONV_W].astype(F32) * ref[:, 0:CONV_W].astype(F32)

    z = gated_input(cg_ref)
    has_prev = (i >= 2).astype(F32)
    has_next = jnp.logical_and(i != 0, i != n_tiles - 1).astype(F32)
    z_before = gated_input(cprev_ref)[BF16_ROWS - 1:BF16_ROWS] * has_prev
    z_after = gated_input(cnext_ref)[0:1] * has_next
    row = lax.broadcasted_iota(jnp.int32, z.shape, 0)
    z_prev = jnp.where(row == 0, z_before, pltpu.roll(z, 1, 0))
    z_next = jnp.where(row == TM - 1, z_after, pltpu.roll(z, TM - 1, 0))
    conv = z_prev * cw_ref[0:1, :] + z * cw_ref[1:2, :] + z_next * cw_ref[2:3, :]
    a_pre = cg_ref[:, CONV_W:2 * CONV_W].astype(F32) * conv
    ya = _dot(a_pre.astype(BF16), cout_ref[...])

    y = slab(yf_ref) + slab(yb_ref) + d_ref[...] * slab(u_ref)
    g = _gelu_tanh(y).astype(BF16)
    yb = _dot(g, glua_ref[...]) * _sigmoid(_dot(g, glub_ref[...]))

    yc = lax.dot_general(o_ref[...], naout_ref[...], (((0,), (0,)), ((), ())),
                         preferred_element_type=F32)

    mix = (_sigmoid(gate_ref[:, 0:D_MODEL].astype(F32)) * ya
           + _sigmoid(gate_ref[:, D_MODEL:2 * D_MODEL].astype(F32)) * yb
           + _sigmoid(gate_ref[:, 2 * D_MODEL:3 * D_MODEL].astype(F32)) * yc)
    out_ref[...] = x_ref[...] + g1_ref[...] * _dot(mix.astype(BF16), wout_ref[...])


def _merge(xc, conv_g, u_tm, yf, yb, attn_o, gates, gate1, conv_w, conv_out, s5_d, glu_a, glu_b,
           na_out, w_out):
    bsz, s, _ = xc.shape
    nt = s // TM
    halo_blocks = s // BF16_ROWS
    per_tile = TM // BF16_ROWS
    tok = lambda width: _tok_spec(width, batch_major=False)
    tm_spec = _slab_spec()
    prev_spec = pl.BlockSpec((None, BF16_ROWS, 3 * CONV_W),
                             lambda i, b: (b, jnp.maximum(i * per_tile - 1, 0), 0))
    next_spec = pl.BlockSpec((None, BF16_ROWS, 3 * CONV_W),
                             lambda i, b: (b, jnp.minimum((i + 1) * per_tile, halo_blocks - 1), 0))
    return pl.pallas_call(
        functools.partial(_merge_kernel, n_tiles=nt),
        grid=(nt, bsz),
        in_specs=[tok(D_MODEL), tok(3 * CONV_W), prev_spec, next_spec, tm_spec, tm_spec, tm_spec,
                  _fm_spec(batch_major=False), tok(3 * D_MODEL), _mod_spec(batch_major=False),
                  _const_spec((3, CONV_W)), _const_spec((CONV_W, D_MODEL)), _const_spec((1, SSM_W)),
                  _const_spec((SSM_W, D_MODEL)), _const_spec((SSM_W, D_MODEL)),
                  _const_spec((NA_W, D_MODEL)), _const_spec((D_MODEL, D_MODEL))],
        out_specs=tok(D_MODEL),
        out_shape=jax.ShapeDtypeStruct(xc.shape, F32),
        compiler_params=_cparams("parallel", "arbitrary"),
        name="merge",
    )(xc, conv_g, conv_g, conv_g, u_tm, yf, yb, attn_o, gates, gate1, conv_w, conv_out, s5_d,
      glu_a, glu_b, na_out, w_out)


def _mlp_kernel(x_ref, sh_ref, sc_ref, gt_ref, g_ref, w1_ref, w2_ref, fg_ref, out_ref, *, final):
    x = x_ref[...]
    h = (_rms(x, g_ref[...]) * (1.0 + sc_ref[...]) + sh_ref[...]).astype(BF16)
    acc = jnp.zeros_like(x)
    for c in range(MLP_HIDDEN // D_MODEL):
        cols = slice(c * D_MODEL, (c + 1) * D_MODEL)
        a = jnp.maximum(_dot(h, w1_ref[:, cols]), 0.0)
        acc = acc + _dot((a * a).astype(BF16), w2_ref[cols, :])
    y = x + gt_ref[...] * acc
    if final:
        y = _rms(y, fg_ref[...])
    out_ref[...] = y


def _mlp(xc, shift, scale, gate, g, w1, w2, final_g, final):
    bsz, s, _ = xc.shape
    first = CTX_LEN // TM if final else 0
    nt = s // TM - first
    tok_in = pl.BlockSpec((None, TM, D_MODEL), lambda b, i: (b, i + first, 0))
    mod = pl.BlockSpec((None, None, 1, D_MODEL), lambda b, i: (b, jnp.minimum(i + first, 1), 0, 0))
    return pl.pallas_call(
        functools.partial(_mlp_kernel, final=final),
        grid=(bsz, nt),
        in_specs=[tok_in, mod, mod, mod, _const_spec((1, D_MODEL)),
                  _const_spec((D_MODEL, MLP_HIDDEN)), _const_spec((MLP_HIDDEN, D_MODEL)),
                  _const_spec((1, D_MODEL))],
        out_specs=_tok_spec(D_MODEL),
        out_shape=jax.ShapeDtypeStruct((bsz, nt * TM, D_MODEL), F32),
        compiler_params=_cparams("parallel", "parallel"),
        name="mlp",
    )(xc, shift, scale, gate, g, w1, w2, final_g)


def kernel(x, c, ctx, c_ctx, w_mod, b_mod, norm1_g, w_in, conv_w, conv_out, s5_lam_re, s5_lam_im,
           s5_log_step, s5_b_re, s5_b_im, s5_c_re, s5_c_im, s5_d, s5_glu_a, s5_glu_b, na_rpb,
           na_out, w_out, norm2_g, mlp_w1, mlp_w2, final_norm_g):
    bsz, seq, _ = x.shape
    depth = w_mod.shape[0]
    assert ctx.shape[1] == CTX_LEN == TM and seq % TM == 0 and bsz == SUBLANES
    grid_rows = seq // GRID_W
    assert grid_rows >= NA_KROWS and grid_rows % NA_QROWS == 0
    s = CTX_LEN + seq

    cond_rows = 2 * SUBLANES
    cond = jnp.zeros((cond_rows, D_MODEL), F32).at[:bsz].set(c).at[bsz].set(c_ctx)
    mods = _adaln(cond, w_mod, b_mod).reshape(depth, cond_rows, N_MOD, D_MODEL)
    lat = mods[:, :bsz]
    cx = jnp.broadcast_to(mods[:, bsz:bsz + 1], lat.shape)
    mod = jnp.stack([cx, lat], axis=2)[:, :, :, :, None, :]

    s5_mats = _s5_matrices(s5_lam_re, s5_lam_im, s5_log_step, s5_b_re, s5_b_im, s5_c_re, s5_c_im)

    bf = lambda w: w.astype(BF16)
    w_in_b, conv_out_b, glu_a_b, glu_b_b = bf(w_in), bf(conv_out), bf(s5_glu_a), bf(s5_glu_b)
    na_out_b, w_out_b, w1_b, w2_b = bf(na_out), bf(w_out), bf(mlp_w1), bf(mlp_w2)
    w_qv_t = jnp.swapaxes(jnp.concatenate([w_in_b[:, :, IN_OFF[4]:IN_OFF[5]],
                                           w_in_b[:, :, IN_OFF[6]:IN_OFF[7]]], axis=2), 1, 2)

    na_bias = _na_bias_tables(na_rpb, grid_rows)

    xc = jnp.concatenate([ctx, x], axis=1)
    row = lambda v: v.reshape(1, -1)
    for l in range(depth):
        m = mod[l]
        conv_g, u_tm, qt, k, vt, gates = _in_proj(xc, m[:, :, 0], m[:, :, 1], row(norm1_g[l]),
                                                  w_in_b[l], w_qv_t[l])
        yf, yb = _s5_scan(u_tm, *(mat[l] for mat in s5_mats), s)
        attn_o = _attention(qt, k, vt, na_bias[l], grid_rows)
        xc = _merge(xc, conv_g, u_tm, yf, yb, attn_o, gates, m[:, :, 2], conv_w[l], conv_out_b[l],
                    row(s5_d[l]), glu_a_b[l], glu_b_b[l], na_out_b[l], w_out_b[l])
        xc = _mlp(xc, m[:, :, 3], m[:, :, 4], m[:, :, 5], row(norm2_g[l]), w1_b[l], w2_b[l],
                  row(final_norm_g), final=(l == depth - 1))
    return xc
```

```python
import functools
import math

import numpy as np
import jax
import jax.numpy as jnp
from jax import lax
from jax.experimental import pallas as pl
from jax.experimental.pallas import tpu as pltpu

D_MODEL = 1024
CTX_LEN = 256
GRID_W = 64
N_MOD = 6
CONV_W = 512
SSM_W = 512
SSM_GROUP = 16
SSM_GROUPS = SSM_W // SSM_GROUP
SSM_STATE = 64
NA_HEADS = 8
NA_HEAD_DIM = 64
NA_W = NA_HEADS * NA_HEAD_DIM
WIN_H = 8
WIN_W = 16
MLP_HIDDEN = 4 * D_MODEL
IN_SIZES = (CONV_W, CONV_W, CONV_W, SSM_W, NA_W, NA_W, NA_W, D_MODEL, D_MODEL, D_MODEL)
IN_OFF = tuple(sum(IN_SIZES[:i]) for i in range(len(IN_SIZES) + 1))
IN_PROJ_W = IN_OFF[-1]
RMS_EPS = 1e-6
NEG_INF = -1e30
S5_MIN_DECAY = 1e-4
LOG2_E = math.log2(math.e)

F32 = jnp.float32
BF16 = jnp.bfloat16

LANES = 128
SUBLANES = 8
BF16_ROWS = 16
TM = 256
SCAN_LC = 2
SCAN_T = 64
SCAN_CHUNKS = SCAN_T // SCAN_LC
NA_QROWS = TM // GRID_W
NA_KROWS = NA_QROWS + WIN_H
NA_KEY_BLOCK = TM
SSM_LANE_GROUPS = LANES // SSM_GROUP
SSM_BLOCKS = SSM_W // LANES
SSM_BLOCK_STATES = SSM_LANE_GROUPS * SSM_STATE
SSM_STATES = SSM_GROUPS * SSM_STATE
VMEM_LIMIT = 56 * 1024 * 1024


def _cparams(*sem):
    return pltpu.CompilerParams(dimension_semantics=sem, vmem_limit_bytes=VMEM_LIMIT)


def _rms(x, g):
    return x * lax.rsqrt(jnp.mean(x * x, axis=-1, keepdims=True) + RMS_EPS) * g


def _dot(a, b):
    return jnp.dot(a, b, preferred_element_type=F32)


def _dot_nt(a, b):
    return lax.dot_general(a, b, (((1,), (1,)), ((), ())), preferred_element_type=F32)


def _adaln_kernel(c_ref, w_ref, b_ref, o_ref):
    c = c_ref[...]
    s = c * jax.nn.sigmoid(c)
    o_ref[...] = jnp.dot(s, w_ref[...], preferred_element_type=F32,
                         precision=lax.Precision.HIGHEST) + b_ref[...]


def _adaln(cond, w_mod, b_mod):
    depth = w_mod.shape[0]
    rows = cond.shape[0]
    n_tiles = (N_MOD * D_MODEL) // D_MODEL
    return pl.pallas_call(
        _adaln_kernel,
        grid=(depth, n_tiles),
        in_specs=[
            pl.BlockSpec((rows, D_MODEL), lambda l, j: (0, 0)),
            pl.BlockSpec((None, D_MODEL, D_MODEL), lambda l, j: (l, 0, j)),
            pl.BlockSpec((None, 1, D_MODEL), lambda l, j: (l, 0, j)),
        ],
        out_specs=pl.BlockSpec((None, rows, D_MODEL), lambda l, j: (l, 0, j)),
        out_shape=jax.ShapeDtypeStruct((depth, rows, N_MOD * D_MODEL), F32),
        compiler_params=_cparams("parallel", "parallel"),
        name="adaln",
    )(cond, w_mod, b_mod.reshape(depth, 1, N_MOD * D_MODEL))


def _s5_prep_kernel(lre_ref, lim_ref, ls_ref, bre_ref, bim_ref, cre_ref, cim_ref,
                    are_ref, aim_ref, fre_ref, fim_ref, ore_ref, oim_ref):
    lr = jnp.minimum(lre_ref[...], -S5_MIN_DECAY)
    li = lim_ref[...]
    dt = jnp.exp(ls_ref[...])
    xr = lr * dt
    xi = li * dt

    def power(k):
        e = jnp.exp(k * xr)
        return e * jnp.cos(k * xi), e * jnp.sin(k * xi)

    ar, ai = power(1.0)
    nr = ar - 1.0
    den = lr * lr + li * li
    cr = (nr * lr + ai * li) / den
    ci = (ai * lr - nr * li) / den
    bre = bre_ref[...]
    bim = bim_ref[...]
    bbr = cr * bre - ci * bim
    bbi = cr * bim + ci * bre
    cre = cre_ref[...]
    cim = cim_ref[...]
    fre_ref[:, 0] = bbr
    fim_ref[:, 0] = bbi
    ore_ref[:, 0] = cre
    oim_ref[:, 0] = cim
    for k in range(1, SCAN_LC + 1):
        pr, pi = power(float(k))
        if k < SCAN_LC:
            fre_ref[:, k] = pr * bbr - pi * bbi
            fim_ref[:, k] = pr * bbi + pi * bbr
        ore_ref[:, k] = pr * cre - pi * cim
        oim_ref[:, k] = pr * cim + pi * cre
    are_ref[...], aim_ref[...] = power(float(SCAN_LC))


def _s5_prep(lam_re, lam_im, log_step, b_re, b_im, c_re, c_im):
    depth = lam_re.shape[0]
    rows = depth * 2 * SSM_GROUPS
    rb = SSM_GROUPS
    lre = lam_re.reshape(rows, 1, SSM_STATE)
    lim = lam_im.reshape(rows, 1, SSM_STATE)
    ls = jnp.broadcast_to(log_step.reshape(rows, 1, 1), (rows, 1, SSM_STATE))
    bre = jnp.swapaxes(b_re.reshape(rows, SSM_STATE, SSM_GROUP), 1, 2)
    bim = jnp.swapaxes(b_im.reshape(rows, SSM_STATE, SSM_GROUP), 1, 2)
    cre = c_re.reshape(rows, SSM_GROUP, SSM_STATE)
    cim = c_im.reshape(rows, SSM_GROUP, SSM_STATE)
    vec = pl.BlockSpec((rb, 1, SSM_STATE), lambda r: (r, 0, 0))
    mat = pl.BlockSpec((rb, SSM_GROUP, SSM_STATE), lambda r: (r, 0, 0))
    pw = lambda n: pl.BlockSpec((rb, n, SSM_GROUP, SSM_STATE), lambda r: (r, 0, 0, 0))
    v2 = jax.ShapeDtypeStruct((rows, 1, SSM_STATE), F32)
    v4 = lambda n: jax.ShapeDtypeStruct((rows, n, SSM_GROUP, SSM_STATE), F32)
    return pl.pallas_call(
        _s5_prep_kernel,
        grid=(rows // rb,),
        in_specs=[vec, vec, vec, mat, mat, mat, mat],
        out_specs=[vec, vec, pw(SCAN_LC), pw(SCAN_LC), pw(SCAN_LC + 1), pw(SCAN_LC + 1)],
        out_shape=[v2, v2, v4(SCAN_LC), v4(SCAN_LC), v4(SCAN_LC + 1), v4(SCAN_LC + 1)],
        compiler_params=pltpu.CompilerParams(dimension_semantics=("parallel",)),
        name="s5_prep",
    )(lre, lim, ls, bre, bim, cre, cim)


def _s5_toe_kernel(bre_ref, bim_ref, ore_ref, oim_ref, k_ref):
    hi = lax.Precision.HIGHEST
    for k in range(SCAN_LC):
        k_ref[k] = (jnp.dot(bre_ref[...], ore_ref[k], preferred_element_type=F32, precision=hi)
                    - jnp.dot(bim_ref[...], oim_ref[k], preferred_element_type=F32, precision=hi))


def _s5_toe(bd_re, bd_im, oc_re, oc_im):
    nb = bd_re.shape[0]
    bspec = pl.BlockSpec((None, LANES, SSM_BLOCK_STATES), lambda r: (r, 0, 0))
    ospec = pl.BlockSpec((None, SCAN_LC, SSM_BLOCK_STATES, LANES), lambda r: (r, 0, 0, 0))
    return pl.pallas_call(
        _s5_toe_kernel,
        grid=(nb,),
        in_specs=[bspec, bspec, ospec, ospec],
        out_specs=pl.BlockSpec((None, SCAN_LC, LANES, LANES), lambda r: (r, 0, 0, 0)),
        out_shape=jax.ShapeDtypeStruct((nb, SCAN_LC, LANES, LANES), F32),
        compiler_params=pltpu.CompilerParams(dimension_semantics=("parallel",)),
        name="s5_toe",
    )(bd_re, bd_im, oc_re, oc_im)


def _block_diag(m):
    ng, a, b = m.shape[-3:]
    eye = jnp.eye(ng, dtype=m.dtype)
    out = jnp.einsum("...gab,gh->...gahb", m, eye)
    return out.reshape(m.shape[:-3] + (ng * a, ng * b))


def _s5_matrices(lam_re, lam_im, log_step, b_re, b_im, c_re, c_im):
    depth = lam_re.shape[0]
    lc = SCAN_LC
    a_re, a_im, f_re, f_im, o_re, o_im = _s5_prep(lam_re, lam_im, log_step, b_re, b_im, c_re, c_im)
    lead = (depth, 2, SSM_BLOCKS, SSM_LANE_GROUPS)

    def per_step(x, n):
        return jnp.moveaxis(x.reshape(lead + (n, SSM_GROUP, SSM_STATE)), 4, 3)

    def mirror_bwd(x, axis):
        return jnp.stack([x[:, 0], jnp.flip(x[:, 1], axis=axis - 1)], axis=1)

    def mirror_fwd(x, axis):
        return jnp.stack([jnp.flip(x[:, 0], axis=axis - 1), x[:, 1]], axis=1)

    def fold_mat(f):
        m = _block_diag(mirror_fwd(per_step(f, lc), 3))
        return m.reshape(depth, 2, SSM_BLOCKS, lc * LANES, SSM_BLOCK_STATES).astype(BF16)

    def out_blocks(o):
        return _block_diag(jnp.swapaxes(per_step(o, lc + 1), -1, -2))

    def out_mat(ob):
        m = mirror_bwd(ob[:, :, :, 1:], 3)
        m = jnp.moveaxis(m, 3, 4)
        return m.reshape(depth, 2, SSM_BLOCKS, SSM_BLOCK_STATES, lc * LANES).astype(BF16)

    ob_re, ob_im = out_blocks(o_re), out_blocks(o_im)
    nb = depth * 2 * SSM_BLOCKS
    bd = lambda f: _block_diag(per_step(f, lc)[:, :, :, 0]).reshape(nb, LANES, SSM_BLOCK_STATES)
    oc = lambda ob: ob[:, :, :, :lc].reshape(nb, lc, SSM_BLOCK_STATES, LANES)
    kk = _s5_toe(bd(f_re), bd(f_im), oc(ob_re), oc(ob_im))
    kk = kk.reshape(depth, 2, SSM_BLOCKS, lc, LANES, LANES)
    zero = jnp.zeros((depth, SSM_BLOCKS, LANES, LANES), F32)
    toes = []
    for d in range(2):
        rows = []
        for j_in in range(lc):
            lag = [(j_out - j_in) if d == 0 else (j_in - j_out) for j_out in range(lc)]
            rows.append(jnp.concatenate([kk[:, d, :, g] if g >= 0 else zero for g in lag], axis=-1))
        toes.append(jnp.concatenate(rows, axis=-2))
    toe = jnp.stack(toes, axis=1).astype(BF16)

    a_shape = (depth, 2, 1, SSM_STATES)
    a_re = jnp.broadcast_to(a_re.reshape(a_shape), (depth, 2, SUBLANES, SSM_STATES))
    a_im = jnp.broadcast_to(a_im.reshape(a_shape), (depth, 2, SUBLANES, SSM_STATES))
    return fold_mat(f_re), fold_mat(f_im), out_mat(ob_re), out_mat(ob_im), toe, a_re, a_im


def _batch_rows(b):
    return pl.ds(b, TM, stride=SUBLANES)


def _in_proj_kernel(x_ref, sh_ref, sc_ref, g_ref, w_ref, wqv_ref, conv_ref, u_ref, qt_ref, k_ref,
                    vt_ref, gate_ref):
    h = _rms(x_ref[...], g_ref[...]) * (1.0 + sc_ref[...]) + sh_ref[...]
    hb = h.astype(BF16)
    conv_ref[...] = _dot(hb, w_ref[:, IN_OFF[0]:IN_OFF[3]]).astype(BF16)
    u = _dot(hb, w_ref[:, IN_OFF[3]:IN_OFF[4]])
    rows = _batch_rows(pl.program_id(1))
    for q in range(SSM_BLOCKS):
        u_ref[q, rows, :] = u[:, q * LANES:(q + 1) * LANES]
    qt_ref[...] = _dot_nt(wqv_ref[0:NA_W, :], hb).astype(BF16)
    k_ref[...] = _dot(hb, w_ref[:, IN_OFF[5]:IN_OFF[6]]).astype(BF16)
    vt_ref[...] = _dot_nt(wqv_ref[NA_W:2 * NA_W, :], hb).astype(BF16)
    gate_ref[...] = _dot(hb, w_ref[:, IN_OFF[7]:IN_OFF[10]]).astype(BF16)


def _grid_bi(batch_major):
    return (lambda b, i: (b, i)) if batch_major else (lambda i, b: (b, i))


def _tok_spec(width, batch_major=True):
    bi = _grid_bi(batch_major)
    return pl.BlockSpec((None, TM, width), lambda *g: bi(*g) + (0,))


def _mod_spec(batch_major=True):
    bi = _grid_bi(batch_major)

    def index(*g):
        b, i = bi(*g)
        return (b, jnp.minimum(i, 1), 0, 0)

    return pl.BlockSpec((None, None, 1, D_MODEL), index)


def _const_spec(shape):
    nd = len(shape)
    return pl.BlockSpec(shape, lambda *g: (0,) * nd)


def _slab_spec():
    return pl.BlockSpec((SSM_BLOCKS, TM * SUBLANES, LANES), lambda i, b: (0, i, 0))


def _fm_spec(batch_major=True):
    bi = _grid_bi(batch_major)
    return pl.BlockSpec((None, None, NA_W, TM), lambda *g: bi(*g) + (0, 0))


def _in_proj(xc, shift, scale, g, w, w_qv_t):
    bsz, s, _ = xc.shape
    nt = s // TM
    tok = lambda width: _tok_spec(width, batch_major=False)
    mod = _mod_spec(batch_major=False)
    fm = _fm_spec(batch_major=False)
    fm_sds = jax.ShapeDtypeStruct((bsz, nt, NA_W, TM), BF16)
    return pl.pallas_call(
        _in_proj_kernel,
        grid=(nt, bsz),
        in_specs=[tok(D_MODEL), mod, mod, _const_spec((1, D_MODEL)),
                  _const_spec((D_MODEL, IN_PROJ_W)), _const_spec((2 * NA_W, D_MODEL))],
        out_specs=[tok(3 * CONV_W), _slab_spec(), fm, tok(NA_W), fm, tok(3 * D_MODEL)],
        out_shape=[jax.ShapeDtypeStruct((bsz, s, 3 * CONV_W), BF16),
                   jax.ShapeDtypeStruct((SSM_BLOCKS, s * bsz, LANES), F32),
                   fm_sds,
                   jax.ShapeDtypeStruct((bsz, s, NA_W), BF16),
                   fm_sds,
                   jax.ShapeDtypeStruct((bsz, s, 3 * D_MODEL), BF16)],
        compiler_params=_cparams("parallel", "arbitrary"),
        name="in_proj",
    )(xc, shift, scale, g, w, w_qv_t)


def _s5_scan_kernel(uf_ref, ub_ref, fre_ref, fim_ref, ore_ref, oim_ref, toe_ref, are_ref, aim_ref,
                    yf_ref, yb_ref, sre_ref, sim_ref, hre_ref, him_ref):
    @pl.when(pl.program_id(0) == 0)
    def _():
        hre_ref[...] = jnp.zeros_like(hre_ref)
        him_ref[...] = jnp.zeros_like(him_ref)

    bs = SSM_BLOCK_STATES
    rows_c = SCAN_CHUNKS * SUBLANES

    def put(y_ref, q, y, first):
        for j in range(SCAN_LC):
            yj = y[:, j * LANES:(j + 1) * LANES].reshape(SCAN_CHUNKS, SUBLANES, LANES)
            y_ref[q, :, j] = yj if first else y_ref[q, :, j] + yj

    for d, (u_ref, y_ref) in enumerate(((uf_ref, yf_ref), (ub_ref, yb_ref))):
        for q in range(SSM_BLOCKS):
            uc = jnp.concatenate([u_ref[q, :, j].reshape(rows_c, LANES) for j in range(SCAN_LC)],
                                 axis=1).astype(BF16)
            sre_ref[d, :, q * bs:(q + 1) * bs] = _dot(uc, fre_ref[d, q])
            sim_ref[d, :, q * bs:(q + 1) * bs] = _dot(uc, fim_ref[d, q])
            put(y_ref, q, _dot(uc, toe_ref[d, q]), True)

        for q in range(SSM_BLOCKS):
            cols = slice(q * bs, (q + 1) * bs)
            ar = are_ref[d, :, cols]
            ai = aim_ref[d, :, cols]

            def step(c, carry, d=d, cols=cols, ar=ar, ai=ai):
                hr, hi = carry
                cc = c if d == 0 else SCAN_CHUNKS - 1 - c
                rows = pl.ds(pl.multiple_of(cc * SUBLANES, SUBLANES), SUBLANES)
                inc_r = sre_ref[d, rows, cols]
                inc_i = sim_ref[d, rows, cols]
                sre_ref[d, rows, cols] = hr
                sim_ref[d, rows, cols] = hi
                return ar * hr - ai * hi + inc_r, ar * hi + ai * hr + inc_i

            hr, hi = lax.fori_loop(0, SCAN_CHUNKS, step, (hre_ref[d, :, cols], him_ref[d, :, cols]),
                                   unroll=4)
            hre_ref[d, :, cols] = hr
            him_ref[d, :, cols] = hi

        for q in range(SSM_BLOCKS):
            cols = slice(q * bs, (q + 1) * bs)
            y = (_dot(sre_ref[d, :, cols].astype(BF16), ore_ref[d, q])
                 - _dot(sim_ref[d, :, cols].astype(BF16), oim_ref[d, q]))
            put(y_ref, q, y, False)


def _s5_scan(u_tm, f_re, f_im, o_re, o_im, toe, a_re, a_im, seq):
    rows_total = u_tm.shape[1]
    bsz = rows_total // seq
    assert bsz == SUBLANES and SCAN_T % SCAN_LC == 0 and CTX_LEN % SCAN_T == 0
    nt = seq // SCAN_T
    nc = CTX_LEN // SCAN_T
    chunked = (SSM_BLOCKS, seq // SCAN_LC, SCAN_LC, bsz, LANES)
    block = (SSM_BLOCKS, SCAN_CHUNKS, SCAN_LC, bsz, LANES)

    def bwd_tile(i):
        return jnp.where(i < nc, nc - 1 - i, nt - 1 - (i - nc))

    fwd_spec = pl.BlockSpec(block, lambda i: (0, i, 0, 0, 0))
    bwd_spec = pl.BlockSpec(block, lambda i: (0, bwd_tile(i), 0, 0, 0))
    full = lambda a: pl.BlockSpec(a.shape, lambda i: (0,) * a.ndim)
    y_sds = jax.ShapeDtypeStruct(chunked, F32)
    u5 = u_tm.reshape(chunked)
    yf, yb = pl.pallas_call(
        _s5_scan_kernel,
        grid=(nt,),
        in_specs=[fwd_spec, bwd_spec, full(f_re), full(f_im), full(o_re), full(o_im), full(toe),
                  full(a_re), full(a_im)],
        out_specs=[fwd_spec, bwd_spec],
        out_shape=[y_sds, y_sds],
        scratch_shapes=[pltpu.VMEM((2, SCAN_CHUNKS * bsz, SSM_STATES), F32),
                        pltpu.VMEM((2, SCAN_CHUNKS * bsz, SSM_STATES), F32),
                        pltpu.VMEM((2, bsz, SSM_STATES), F32),
                        pltpu.VMEM((2, bsz, SSM_STATES), F32)],
        compiler_params=_cparams("arbitrary"),
        name="s5_scan",
    )(u5, u5, f_re, f_im, o_re, o_im, toe, a_re, a_im)
    return yf.reshape(u_tm.shape), yb.reshape(u_tm.shape)


def _attention_kernel(qt_ref, k_ref, vt_ref, bias_ref, o_ref, s_even_ref, s_odd_ref, *, grid_rows):
    i = pl.program_id(1)
    hd = NA_HEAD_DIM
    qt = (qt_ref[...].astype(F32) * (hd ** -0.5 * LOG2_E)).astype(BF16)
    zeros = jnp.zeros((hd, TM), BF16)
    ones = jnp.ones((BF16_ROWS, NA_KEY_BLOCK), BF16)
    ctx_rows = slice(0, CTX_LEN)

    def pair_cols(h):
        return slice((h // 2) * LANES, (h // 2 + 1) * LANES)

    def padded_qt(h):
        qh = qt[h * hd:(h + 1) * hd]
        return jnp.concatenate([qh, zeros] if h % 2 == 0 else [zeros, qh], axis=0)

    @pl.when(i == 0)
    def _():
        for h in range(NA_HEADS):
            s = _dot(k_ref[ctx_rows, pair_cols(h)], padded_qt(h))
            p = jnp.exp2(s - jnp.max(s, axis=0, keepdims=True)).astype(BF16)
            ol = _dot(jnp.concatenate([vt_ref[0, h * hd:(h + 1) * hd, :], ones], axis=0), p)
            o_ref[h * hd:(h + 1) * hd, :] = (ol[0:hd] / ol[hd:hd + 1]).astype(BF16)

    @pl.when(i > 0)
    def _():
        r_a = (i - 1) * NA_QROWS
        k_start = jnp.clip(r_a - WIN_H // 2, 0, grid_rows - NA_KROWS)
        start = pl.multiple_of(CTX_LEN + k_start * GRID_W, NA_KEY_BLOCK)
        kb = NA_KEY_BLOCK
        blk0 = start // kb
        n_loc = NA_KROWS * GRID_W // kb
        n_all = n_loc + 1
        k_rows = [pl.ds(start + j * kb, kb) for j in range(n_loc)] + [ctx_rows]
        v_blocks = [blk0 + j for j in range(n_loc)] + [0]
        rhs = [padded_qt(h) for h in range(NA_HEADS)]
        s_bufs = (s_even_ref, s_odd_ref)

        def scores(h, j):
            s = _dot(k_ref[k_rows[j], pair_cols(h)], rhs[h])
            if j < n_loc:
                s = s + bias_ref[h, j * kb:(j + 1) * kb, :]
            s_bufs[h % 2][j * kb:(j + 1) * kb] = s
            return jnp.max(s, axis=0, keepdims=True)

        m = functools.reduce(jnp.maximum, [scores(0, j) for j in range(n_all)])
        for h in range(NA_HEADS):
            ol = None
            next_max = []
            for j in range(n_all):
                if h + 1 < NA_HEADS:
                    next_max.append(scores(h + 1, j))
                p = jnp.exp2(s_bufs[h % 2][j * kb:(j + 1) * kb] - m).astype(BF16)
                lhs = jnp.concatenate([vt_ref[v_blocks[j], h * hd:(h + 1) * hd, :], ones], axis=0)
                part = _dot(lhs, p)
                ol = part if ol is None else ol + part
            o_ref[h * hd:(h + 1) * hd, :] = (ol[0:hd] / ol[hd:hd + 1]).astype(BF16)
            if next_max:
                m = functools.reduce(jnp.maximum, next_max)


def _na_tile_classes(grid_rows):
    return ((0, 0), (NA_QROWS, 0), (grid_rows - NA_QROWS, grid_rows - NA_KROWS))


def _na_valid(grid_rows):
    kr = np.arange(NA_KROWS)[:, None, None, None]
    kc = np.arange(GRID_W)[None, :, None, None]
    rq = np.arange(NA_QROWS)[None, None, :, None]
    cq = np.arange(GRID_W)[None, None, None, :]
    oks = []
    for r_a, k_start in _na_tile_classes(grid_rows):
        r0 = np.clip(r_a + rq - WIN_H // 2, 0, grid_rows - WIN_H)
        krow = k_start + kr
        cs = np.clip(cq - WIN_W // 2, 0, GRID_W - WIN_W)
        ok = (krow >= r0) & (krow < r0 + WIN_H) & (kc >= cs) & (kc < cs + WIN_W)
        oks.append(ok.reshape(NA_KROWS * GRID_W, TM))
    return np.stack(oks)


NA_DR_MARGIN = NA_QROWS
NA_DR_ROWS = 2 * WIN_H - 1 + 2 * NA_DR_MARGIN + 1


def _na_bias_kernel(w_ref, mask_ref, out_ref, *, grid_rows):
    lo_half = lax.broadcasted_iota(jnp.int32, (GRID_W, LANES), 1) < GRID_W
    cache = {}

    def toeplitz(d, upper):
        if (d, upper) not in cache:
            row = jnp.broadcast_to(w_ref[d:d + 1, :], (GRID_W, LANES))
            shift = GRID_W + 1 + (GRID_W if upper else 0)
            cache[d, upper] = pltpu.roll(row, shift, 1, stride=1, stride_axis=0)
        return cache[d, upper]

    for cls, (r_a, k_start) in enumerate(_na_tile_classes(grid_rows)):
        for kr in range(NA_KROWS):
            for pair in range(NA_QROWS // 2):
                d = k_start + kr - (r_a + 2 * pair) + (WIN_H - 1) + NA_DR_MARGIN
                block = jnp.where(lo_half, toeplitz(d, False), toeplitz(d - 1, True))
                rows = slice(kr * GRID_W, (kr + 1) * GRID_W)
                cols = slice(pair * LANES, (pair + 1) * LANES)
                out_ref[cls, rows, cols] = block + mask_ref[cls, rows, cols]


def _na_bias_tables(rpb, grid_rows):
    depth, heads, n_dr, n_dc = rpb.shape
    n_keys = NA_KROWS * GRID_W
    first_lane = (GRID_W - 1) - (WIN_W - 1)
    w = jnp.pad(jnp.flip(rpb * LOG2_E, axis=-1),
                ((0, 0), (0, 0), (NA_DR_MARGIN, NA_DR_ROWS - n_dr - NA_DR_MARGIN),
                 (first_lane, LANES - first_lane - n_dc)))
    mask = jnp.asarray(np.where(_na_valid(grid_rows), 0.0, NEG_INF).astype(np.float32))
    return pl.pallas_call(
        functools.partial(_na_bias_kernel, grid_rows=grid_rows),
        grid=(depth, heads),
        in_specs=[pl.BlockSpec((None, None, NA_DR_ROWS, LANES), lambda l, h: (l, h, 0, 0)),
                  pl.BlockSpec((3, n_keys, TM), lambda l, h: (0, 0, 0))],
        out_specs=pl.BlockSpec((None, 3, None, n_keys, TM), lambda l, h: (l, 0, h, 0, 0)),
        out_shape=jax.ShapeDtypeStruct((depth, 3, heads, n_keys, TM), F32),
        compiler_params=pltpu.CompilerParams(dimension_semantics=("parallel", "parallel")),
        name="na_bias",
    )(w, mask)


def _attention(qt, k, vt, bias, grid_rows):
    bsz, s, _ = k.shape
    nt = s // TM
    n_keys = NA_KROWS * GRID_W

    def bias_class(b, i):
        return (jnp.where(i <= 1, 0, jnp.where(i == nt - 1, 2, 1)), 0, 0, 0)

    return pl.pallas_call(
        functools.partial(_attention_kernel, grid_rows=grid_rows),
        grid=(bsz, nt),
        in_specs=[_fm_spec(),
                  pl.BlockSpec((None, s, NA_W), lambda b, i: (b, 0, 0)),
                  pl.BlockSpec((None, nt, NA_W, TM), lambda b, i: (b, 0, 0, 0)),
                  pl.BlockSpec((None, NA_HEADS, n_keys, TM), bias_class)],
        out_specs=_fm_spec(),
        out_shape=jax.ShapeDtypeStruct((bsz, nt, NA_W, TM), BF16),
        scratch_shapes=[pltpu.VMEM((n_keys + CTX_LEN, TM), F32),
                        pltpu.VMEM((n_keys + CTX_LEN, TM), F32)],
        compiler_params=_cparams("parallel", "arbitrary"),
        name="attention",
    )(qt, k, vt, bias)


def _gelu_tanh(x):
    return 0.5 * x * (1.0 + jnp.tanh(math.sqrt(2.0 / math.pi) * (x + 0.044715 * (x * x * x))))


def _merge_kernel(x_ref, cg_ref, cprev_ref, cnext_ref, u_ref, yf_ref, yb_ref, o_ref, gate_ref,
                  g1_ref, cw_ref, cout_ref, d_ref, glua_ref, glub_ref, naout_ref, wout_ref,
                  out_ref, *, n_tiles):
    i = pl.program_id(0)
    rows = _batch_rows(pl.program_id(1))

    def slab(ref):
        return jnp.concatenate([ref[q, rows, :] for q in range(SSM_BLOCKS)], axis=1)

    def gated_input(ref):
        return ref[:, 2 * CONV_W:3 * CONV_W].astype(F32) * ref[:, 0:CONV_W].astype(F32)

    z = gated_input(cg_ref)
    has_prev = (i >= 2).astype(F32)
    has_next = jnp.logical_and(i != 0, i != n_tiles - 1).astype(F32)
    z_before = gated_input(cprev_ref)[BF16_ROWS - 1:BF16_ROWS] * has_prev
    z_after = gated_input(cnext_ref)[0:1] * has_next
    row = lax.broadcasted_iota(jnp.int32, z.shape, 0)
    z_prev = jnp.where(row == 0, z_before, pltpu.roll(z, 1, 0))
    z_next = jnp.where(row == TM - 1, z_after, pltpu.roll(z, TM - 1, 0))
    conv = z_prev * cw_ref[0:1, :] + z * cw_ref[1:2, :] + z_next * cw_ref[2:3, :]
    a_pre = cg_ref[:, CONV_W:2 * CONV_W].astype(F32) * conv
    ya = _dot(a_pre.astype(BF16), cout_ref[...])

    y = slab(yf_ref) + slab(yb_ref) + d_ref[...] * slab(u_ref)
    g = _gelu_tanh(y).astype(BF16)
    def twice_logistic(half_x):
        return jnp.tanh(half_x) + 1.0

    yb = _dot(g, glua_ref[...]) * twice_logistic(_dot(g, glub_ref[...]))

    yc = lax.dot_general(o_ref[...], naout_ref[...], (((0,), (0,)), ((), ())),
                         preferred_element_type=F32)

    mix2 = (twice_logistic(gate_ref[:, 0:D_MODEL].astype(F32)) * ya
            + twice_logistic(gate_ref[:, D_MODEL:2 * D_MODEL].astype(F32)) * yb
            + twice_logistic(gate_ref[:, 2 * D_MODEL:3 * D_MODEL].astype(F32)) * yc)
    out_ref[...] = x_ref[...] + g1_ref[...] * _dot(mix2.astype(BF16), wout_ref[...])


def _merge(xc, conv_g, u_tm, yf, yb, attn_o, gates, gate1, conv_w, conv_out, s5_d, glu_a, glu_b,
           na_out, w_out):
    bsz, s, _ = xc.shape
    nt = s // TM
    halo_blocks = s // BF16_ROWS
    per_tile = TM // BF16_ROWS
    tok = lambda width: _tok_spec(width, batch_major=False)
    tm_spec = _slab_spec()
    prev_spec = pl.BlockSpec((None, BF16_ROWS, 3 * CONV_W),
                             lambda i, b: (b, jnp.maximum(i * per_tile - 1, 0), 0))
    next_spec = pl.BlockSpec((None, BF16_ROWS, 3 * CONV_W),
                             lambda i, b: (b, jnp.minimum((i + 1) * per_tile, halo_blocks - 1), 0))
    return pl.pallas_call(
        functools.partial(_merge_kernel, n_tiles=nt),
        grid=(nt, bsz),
        in_specs=[tok(D_MODEL), tok(3 * CONV_W), prev_spec, next_spec, tm_spec, tm_spec, tm_spec,
                  _fm_spec(batch_major=False), tok(3 * D_MODEL), _mod_spec(batch_major=False),
                  _const_spec((3, CONV_W)), _const_spec((CONV_W, D_MODEL)), _const_spec((1, SSM_W)),
                  _const_spec((SSM_W, D_MODEL)), _const_spec((SSM_W, D_MODEL)),
                  _const_spec((NA_W, D_MODEL)), _const_spec((D_MODEL, D_MODEL))],
        out_specs=tok(D_MODEL),
        out_shape=jax.ShapeDtypeStruct(xc.shape, F32),
        compiler_params=_cparams("parallel", "arbitrary"),
        name="merge",
    )(xc, conv_g, conv_g, conv_g, u_tm, yf, yb, attn_o, gates, gate1, conv_w, conv_out, s5_d,
      glu_a, glu_b, na_out, w_out)


def _mlp_kernel(x_ref, sh_ref, sc_ref, gt_ref, g_ref, w1_ref, w2_ref, fg_ref, out_ref, *, final):
    x = x_ref[...]
    h = (_rms(x, g_ref[...]) * (1.0 + sc_ref[...]) + sh_ref[...]).astype(BF16)
    acc = jnp.zeros_like(x)
    for c in range(MLP_HIDDEN // D_MODEL):
        cols = slice(c * D_MODEL, (c + 1) * D_MODEL)
        a = jnp.maximum(_dot(h, w1_ref[:, cols]), 0.0)
        acc = acc + _dot((a * a).astype(BF16), w2_ref[cols, :])
    y = x + gt_ref[...] * acc
    if final:
        y = _rms(y, fg_ref[...])
    out_ref[...] = y


def _mlp(xc, shift, scale, gate, g, w1, w2, final_g, final):
    bsz, s, _ = xc.shape
    first = CTX_LEN // TM if final else 0
    nt = s // TM - first
    tok_in = pl.BlockSpec((None, TM, D_MODEL), lambda b, i: (b, i + first, 0))
    mod = pl.BlockSpec((None, None, 1, D_MODEL), lambda b, i: (b, jnp.minimum(i + first, 1), 0, 0))
    return pl.pallas_call(
        functools.partial(_mlp_kernel, final=final),
        grid=(bsz, nt),
        in_specs=[tok_in, mod, mod, mod, _const_spec((1, D_MODEL)),
                  _const_spec((D_MODEL, MLP_HIDDEN)), _const_spec((MLP_HIDDEN, D_MODEL)),
                  _const_spec((1, D_MODEL))],
        out_specs=_tok_spec(D_MODEL),
        out_shape=jax.ShapeDtypeStruct((bsz, nt * TM, D_MODEL), F32),
        compiler_params=_cparams("parallel", "parallel"),
        name="mlp",
    )(xc, shift, scale, gate, g, w1, w2, final_g)


def kernel(x, c, ctx, c_ctx, w_mod, b_mod, norm1_g, w_in, conv_w, conv_out, s5_lam_re, s5_lam_im,
           s5_log_step, s5_b_re, s5_b_im, s5_c_re, s5_c_im, s5_d, s5_glu_a, s5_glu_b, na_rpb,
           na_out, w_out, norm2_g, mlp_w1, mlp_w2, final_norm_g):
    bsz, seq, _ = x.shape
    depth = w_mod.shape[0]
    assert ctx.shape[1] == CTX_LEN == TM and seq % TM == 0 and bsz == SUBLANES
    grid_rows = seq // GRID_W
    assert grid_rows >= NA_KROWS and grid_rows % NA_QROWS == 0
    s = CTX_LEN + seq

    cond_rows = 2 * SUBLANES
    cond = jnp.zeros((cond_rows, D_MODEL), F32).at[:bsz].set(c).at[bsz].set(c_ctx)
    mods = _adaln(cond, w_mod, b_mod).reshape(depth, cond_rows, N_MOD, D_MODEL)
    lat = mods[:, :bsz]
    cx = jnp.broadcast_to(mods[:, bsz:bsz + 1], lat.shape)
    mod = jnp.stack([cx, lat], axis=2)[:, :, :, :, None, :]

    s5_mats = _s5_matrices(s5_lam_re, s5_lam_im, s5_log_step, s5_b_re, s5_b_im, s5_c_re, s5_c_im)

    bf = lambda w: w.astype(BF16)
    gate_cols = (jnp.arange(IN_PROJ_W) >= IN_OFF[7])
    w_in_b = bf(w_in * jnp.where(gate_cols, 0.5, 1.0).astype(F32))
    conv_out_b, glu_a_b, glu_b_b = bf(conv_out), bf(s5_glu_a * 0.5), bf(s5_glu_b * 0.5)
    na_out_b, w_out_b, w1_b, w2_b = bf(na_out), bf(w_out * 0.5), bf(mlp_w1), bf(mlp_w2)
    w_qv_t = jnp.swapaxes(jnp.concatenate([w_in_b[:, :, IN_OFF[4]:IN_OFF[5]],
                                           w_in_b[:, :, IN_OFF[6]:IN_OFF[7]]], axis=2), 1, 2)

    na_bias = _na_bias_tables(na_rpb, grid_rows)

    xc = jnp.concatenate([ctx, x], axis=1)
    row = lambda v: v.reshape(1, -1)
    for l in range(depth):
        m = mod[l]
        conv_g, u_tm, qt, k, vt, gates = _in_proj(xc, m[:, :, 0], m[:, :, 1], row(norm1_g[l]),
                                                  w_in_b[l], w_qv_t[l])
        yf, yb = _s5_scan(u_tm, *(mat[l] for mat in s5_mats), s)
        attn_o = _attention(qt, k, vt, na_bias[l], grid_rows)
        xc = _merge(xc, conv_g, u_tm, yf, yb, attn_o, gates, m[:, :, 2], conv_w[l], conv_out_b[l],
                    row(s5_d[l]), glu_a_b[l], glu_b_b[l], na_out_b[l], w_out_b[l])
        xc = _mlp(xc, m[:, :, 3], m[:, :, 4], m[:, :, 5], row(norm2_g[l]), w1_b[l], w2_b[l],
                  row(final_norm_g), final=(l == depth - 1))
    return xc
```

```python
import functools
import math

import numpy as np
import jax
import jax.numpy as jnp
from jax import lax
from jax.experimental import pallas as pl
from jax.experimental.pallas import tpu as pltpu

D_MODEL = 1024
CTX_LEN = 256
GRID_W = 64
N_MOD = 6
CONV_W = 512
SSM_W = 512
SSM_GROUP = 16
SSM_GROUPS = SSM_W // SSM_GROUP
SSM_STATE = 64
NA_HEADS = 8
NA_HEAD_DIM = 64
NA_W = NA_HEADS * NA_HEAD_DIM
WIN_H = 8
WIN_W = 16
MLP_HIDDEN = 4 * D_MODEL
IN_SIZES = (CONV_W, CONV_W, CONV_W, SSM_W, NA_W, NA_W, NA_W, D_MODEL, D_MODEL, D_MODEL)
IN_OFF = tuple(sum(IN_SIZES[:i]) for i in range(len(IN_SIZES) + 1))
IN_PROJ_W = IN_OFF[-1]
RMS_EPS = 1e-6
NEG_INF = -1e30
S5_MIN_DECAY = 1e-4
LOG2_E = math.log2(math.e)

F32 = jnp.float32
BF16 = jnp.bfloat16

LANES = 128
SUBLANES = 8
BF16_ROWS = 16
TM = 256
SCAN_LC = 4
SCAN_T = 128
SCAN_CHUNKS = SCAN_T // SCAN_LC
NA_QROWS = TM // GRID_W
NA_KROWS = NA_QROWS + WIN_H
NA_KEY_BLOCK = TM
SSM_LANE_GROUPS = LANES // SSM_GROUP
SSM_BLOCKS = SSM_W // LANES
SSM_BLOCK_STATES = SSM_LANE_GROUPS * SSM_STATE
SSM_STATES = SSM_GROUPS * SSM_STATE
VMEM_LIMIT = 56 * 1024 * 1024


def _cparams(*sem):
    return pltpu.CompilerParams(dimension_semantics=sem, vmem_limit_bytes=VMEM_LIMIT)


def _rms(x, g):
    return x * lax.rsqrt(jnp.mean(x * x, axis=-1, keepdims=True) + RMS_EPS) * g


def _dot(a, b):
    return jnp.dot(a, b, preferred_element_type=F32)


def _dot_nt(a, b):
    return lax.dot_general(a, b, (((1,), (1,)), ((), ())), preferred_element_type=F32)


def _adaln_kernel(c_ref, w_ref, b_ref, o_ref):
    c = c_ref[...]
    s = c * jax.nn.sigmoid(c)
    o_ref[...] = jnp.dot(s, w_ref[...], preferred_element_type=F32,
                         precision=lax.Precision.HIGHEST) + b_ref[...]


def _adaln(cond, w_mod, b_mod):
    depth = w_mod.shape[0]
    rows = cond.shape[0]
    n_tiles = (N_MOD * D_MODEL) // D_MODEL
    return pl.pallas_call(
        _adaln_kernel,
        grid=(depth, n_tiles),
        in_specs=[
            pl.BlockSpec((rows, D_MODEL), lambda l, j: (0, 0)),
            pl.BlockSpec((None, D_MODEL, D_MODEL), lambda l, j: (l, 0, j)),
            pl.BlockSpec((None, 1, D_MODEL), lambda l, j: (l, 0, j)),
        ],
        out_specs=pl.BlockSpec((None, rows, D_MODEL), lambda l, j: (l, 0, j)),
        out_shape=jax.ShapeDtypeStruct((depth, rows, N_MOD * D_MODEL), F32),
        compiler_params=_cparams("parallel", "parallel"),
        name="adaln",
    )(cond, w_mod, b_mod.reshape(depth, 1, N_MOD * D_MODEL))


def _s5_prep_kernel(lre_ref, lim_ref, ls_ref, bre_ref, bim_ref, cre_ref, cim_ref,
                    are_ref, aim_ref, fre_ref, fim_ref, ore_ref, oim_ref):
    lr = jnp.minimum(lre_ref[...], -S5_MIN_DECAY)
    li = lim_ref[...]
    dt = jnp.exp(ls_ref[...])
    xr = lr * dt
    xi = li * dt

    def power(k):
        e = jnp.exp(k * xr)
        return e * jnp.cos(k * xi), e * jnp.sin(k * xi)

    ar, ai = power(1.0)
    nr = ar - 1.0
    den = lr * lr + li * li
    cr = (nr * lr + ai * li) / den
    ci = (ai * lr - nr * li) / den
    bre = bre_ref[...]
    bim = bim_ref[...]
    bbr = cr * bre - ci * bim
    bbi = cr * bim + ci * bre
    cre = cre_ref[...]
    cim = cim_ref[...]
    fre_ref[:, 0] = bbr
    fim_ref[:, 0] = bbi
    ore_ref[:, 0] = cre
    oim_ref[:, 0] = cim
    for k in range(1, SCAN_LC + 1):
        pr, pi = power(float(k))
        if k < SCAN_LC:
            fre_ref[:, k] = pr * bbr - pi * bbi
            fim_ref[:, k] = pr * bbi + pi * bbr
        ore_ref[:, k] = pr * cre - pi * cim
        oim_ref[:, k] = pr * cim + pi * cre
    are_ref[...], aim_ref[...] = power(float(SCAN_LC))


def _s5_prep(lam_re, lam_im, log_step, b_re, b_im, c_re, c_im):
    depth = lam_re.shape[0]
    rows = depth * 2 * SSM_GROUPS
    rb = SSM_GROUPS
    lre = lam_re.reshape(rows, 1, SSM_STATE)
    lim = lam_im.reshape(rows, 1, SSM_STATE)
    ls = jnp.broadcast_to(log_step.reshape(rows, 1, 1), (rows, 1, SSM_STATE))
    bre = jnp.swapaxes(b_re.reshape(rows, SSM_STATE, SSM_GROUP), 1, 2)
    bim = jnp.swapaxes(b_im.reshape(rows, SSM_STATE, SSM_GROUP), 1, 2)
    cre = c_re.reshape(rows, SSM_GROUP, SSM_STATE)
    cim = c_im.reshape(rows, SSM_GROUP, SSM_STATE)
    vec = pl.BlockSpec((rb, 1, SSM_STATE), lambda r: (r, 0, 0))
    mat = pl.BlockSpec((rb, SSM_GROUP, SSM_STATE), lambda r: (r, 0, 0))
    pw = lambda n: pl.BlockSpec((rb, n, SSM_GROUP, SSM_STATE), lambda r: (r, 0, 0, 0))
    v2 = jax.ShapeDtypeStruct((rows, 1, SSM_STATE), F32)
    v4 = lambda n: jax.ShapeDtypeStruct((rows, n, SSM_GROUP, SSM_STATE), F32)
    return pl.pallas_call(
        _s5_prep_kernel,
        grid=(rows // rb,),
        in_specs=[vec, vec, vec, mat, mat, mat, mat],
        out_specs=[vec, vec, pw(SCAN_LC), pw(SCAN_LC), pw(SCAN_LC + 1), pw(SCAN_LC + 1)],
        out_shape=[v2, v2, v4(SCAN_LC), v4(SCAN_LC), v4(SCAN_LC + 1), v4(SCAN_LC + 1)],
        compiler_params=pltpu.CompilerParams(dimension_semantics=("parallel",)),
        name="s5_prep",
    )(lre, lim, ls, bre, bim, cre, cim)


def _s5_toe_kernel(bre_ref, bim_ref, ore_ref, oim_ref, k_ref):
    hi = lax.Precision.HIGHEST
    for k in range(SCAN_LC):
        k_ref[k] = (jnp.dot(bre_ref[...], ore_ref[k], preferred_element_type=F32, precision=hi)
                    - jnp.dot(bim_ref[...], oim_ref[k], preferred_element_type=F32, precision=hi))


def _s5_toe(bd_re, bd_im, oc_re, oc_im):
    nb = bd_re.shape[0]
    bspec = pl.BlockSpec((None, LANES, SSM_BLOCK_STATES), lambda r: (r, 0, 0))
    ospec = pl.BlockSpec((None, SCAN_LC, SSM_BLOCK_STATES, LANES), lambda r: (r, 0, 0, 0))
    return pl.pallas_call(
        _s5_toe_kernel,
        grid=(nb,),
        in_specs=[bspec, bspec, ospec, ospec],
        out_specs=pl.BlockSpec((None, SCAN_LC, LANES, LANES), lambda r: (r, 0, 0, 0)),
        out_shape=jax.ShapeDtypeStruct((nb, SCAN_LC, LANES, LANES), F32),
        compiler_params=pltpu.CompilerParams(dimension_semantics=("parallel",)),
        name="s5_toe",
    )(bd_re, bd_im, oc_re, oc_im)


def _block_diag(m):
    ng, a, b = m.shape[-3:]
    on_diagonal = np.eye(ng, dtype=bool)[:, None, :, None]
    tiled = jnp.broadcast_to(m[..., :, :, None, :], m.shape[:-1] + (ng, b))
    out = jnp.where(on_diagonal, tiled, jnp.zeros((), m.dtype))
    return out.reshape(m.shape[:-3] + (ng * a, ng * b))


def _s5_matrices(lam_re, lam_im, log_step, b_re, b_im, c_re, c_im):
    depth = lam_re.shape[0]
    lc = SCAN_LC
    a_re, a_im, f_re, f_im, o_re, o_im = _s5_prep(lam_re, lam_im, log_step, b_re, b_im, c_re, c_im)
    lead = (depth, 2, SSM_BLOCKS, SSM_LANE_GROUPS)

    def per_step(x, n):
        return jnp.moveaxis(x.reshape(lead + (n, SSM_GROUP, SSM_STATE)), 4, 3)

    def mirror_bwd(x, axis):
        return jnp.stack([x[:, 0], jnp.flip(x[:, 1], axis=axis - 1)], axis=1)

    def mirror_fwd(x, axis):
        return jnp.stack([jnp.flip(x[:, 0], axis=axis - 1), x[:, 1]], axis=1)

    def fold_mat(f):
        m = _block_diag(mirror_fwd(per_step(f, lc).astype(BF16), 3))
        return m.reshape(depth, 2, SSM_BLOCKS, lc * LANES, SSM_BLOCK_STATES)

    def out_blocks(o, first, dtype):
        steps = per_step(o, lc + 1)[:, :, :, first:first + lc].astype(dtype)
        return _block_diag(jnp.swapaxes(steps, -1, -2))

    def out_mat(o):
        m = mirror_bwd(out_blocks(o, 1, BF16), 3)
        m = jnp.moveaxis(m, 3, 4)
        return m.reshape(depth, 2, SSM_BLOCKS, SSM_BLOCK_STATES, lc * LANES)

    nb = depth * 2 * SSM_BLOCKS
    bd = lambda f: _block_diag(per_step(f, lc)[:, :, :, 0]).reshape(nb, LANES, SSM_BLOCK_STATES)
    oc = lambda o: out_blocks(o, 0, F32).reshape(nb, lc, SSM_BLOCK_STATES, LANES)
    kk = _s5_toe(bd(f_re), bd(f_im), oc(o_re), oc(o_im))
    kk = kk.reshape(depth, 2, SSM_BLOCKS, lc, LANES, LANES)
    zero = jnp.zeros((depth, SSM_BLOCKS, LANES, LANES), F32)
    toes = []
    for d in range(2):
        rows = []
        for j_in in range(lc):
            lag = [(j_out - j_in) if d == 0 else (j_in - j_out) for j_out in range(lc)]
            rows.append(jnp.concatenate([kk[:, d, :, g] if g >= 0 else zero for g in lag], axis=-1))
        toes.append(jnp.concatenate(rows, axis=-2))
    toe = jnp.stack(toes, axis=1).astype(BF16)

    a_shape = (depth, 2, 1, SSM_STATES)
    a_re = jnp.broadcast_to(a_re.reshape(a_shape), (depth, 2, SUBLANES, SSM_STATES))
    a_im = jnp.broadcast_to(a_im.reshape(a_shape), (depth, 2, SUBLANES, SSM_STATES))
    return fold_mat(f_re), fold_mat(f_im), out_mat(o_re), out_mat(o_im), toe, a_re, a_im


def _batch_rows(b):
    return pl.ds(b, TM, stride=SUBLANES)


def _stream_tile(x_ref, c_ref):
    if c_ref is None:
        return x_ref[...]
    return jnp.where(pl.program_id(0) < CTX_LEN // TM, c_ref[...], x_ref[...])


def _in_proj_kernel(*refs, first):
    c_ref = refs[1] if first else None
    x_ref = refs[0]
    (sh_ref, sc_ref, g_ref, w_ref, wqv_ref, conv_ref, u_ref, qt_ref, k_ref, vt_ref,
     gate_ref) = refs[2:] if first else refs[1:]
    h = _rms(_stream_tile(x_ref, c_ref), g_ref[...]) * (1.0 + sc_ref[...]) + sh_ref[...]
    hb = h.astype(BF16)
    conv_ref[...] = _dot(hb, w_ref[:, IN_OFF[0]:IN_OFF[3]]).astype(BF16)
    u = _dot(hb, w_ref[:, IN_OFF[3]:IN_OFF[4]])
    rows = _batch_rows(pl.program_id(1))
    for q in range(SSM_BLOCKS):
        u_ref[q, rows, :] = u[:, q * LANES:(q + 1) * LANES]
    qt_ref[...] = _dot_nt(wqv_ref[0:NA_W, :], hb).astype(BF16)
    k_ref[...] = _dot(hb, w_ref[:, IN_OFF[5]:IN_OFF[6]]).astype(BF16)
    vt_ref[...] = _dot_nt(wqv_ref[NA_W:2 * NA_W, :], hb).astype(BF16)
    gate_ref[...] = _dot(hb, w_ref[:, IN_OFF[7]:IN_OFF[10]]).astype(BF16)


def _grid_bi(batch_major):
    return (lambda b, i: (b, i)) if batch_major else (lambda i, b: (b, i))


def _tok_spec(width, batch_major=True):
    bi = _grid_bi(batch_major)
    return pl.BlockSpec((None, TM, width), lambda *g: bi(*g) + (0,))


def _mod_spec(batch_major=True):
    bi = _grid_bi(batch_major)

    def index(*g):
        b, i = bi(*g)
        return (b, jnp.minimum(i, 1), 0, 0)

    return pl.BlockSpec((None, None, 1, D_MODEL), index)


def _const_spec(shape):
    nd = len(shape)
    return pl.BlockSpec(shape, lambda *g: (0,) * nd)


def _slab_spec():
    return pl.BlockSpec((SSM_BLOCKS, TM * SUBLANES, LANES), lambda i, b: (0, i, 0))


def _fm_spec(batch_major=True):
    bi = _grid_bi(batch_major)
    return pl.BlockSpec((None, None, NA_W, TM), lambda *g: bi(*g) + (0, 0))


def _stream_specs(first):
    if not first:
        return [_tok_spec(D_MODEL, batch_major=False)]
    nc = CTX_LEN // TM
    return [pl.BlockSpec((None, TM, D_MODEL), lambda i, b: (b, jnp.maximum(i - nc, 0), 0)),
            pl.BlockSpec((None, TM, D_MODEL), lambda i, b: (b, jnp.minimum(i, nc - 1), 0))]


def _in_proj(stream, shift, scale, g, w, w_qv_t):
    first = len(stream) == 2
    bsz = stream[0].shape[0]
    s = stream[0].shape[1] + (CTX_LEN if first else 0)
    nt = s // TM
    tok = lambda width: _tok_spec(width, batch_major=False)
    mod = _mod_spec(batch_major=False)
    fm = _fm_spec(batch_major=False)
    fm_sds = jax.ShapeDtypeStruct((bsz, nt, NA_W, TM), BF16)
    return pl.pallas_call(
        functools.partial(_in_proj_kernel, first=first),
        grid=(nt, bsz),
        in_specs=_stream_specs(first) + [mod, mod, _const_spec((1, D_MODEL)),
                                         _const_spec((D_MODEL, IN_PROJ_W)),
                                         _const_spec((2 * NA_W, D_MODEL))],
        out_specs=[tok(3 * CONV_W), _slab_spec(), fm, tok(NA_W), fm, tok(3 * D_MODEL)],
        out_shape=[jax.ShapeDtypeStruct((bsz, s, 3 * CONV_W), BF16),
                   jax.ShapeDtypeStruct((SSM_BLOCKS, s * bsz, LANES), F32),
                   fm_sds,
                   jax.ShapeDtypeStruct((bsz, s, NA_W), BF16),
                   fm_sds,
                   jax.ShapeDtypeStruct((bsz, s, 3 * D_MODEL), BF16)],
        compiler_params=_cparams("parallel", "arbitrary"),
        name="in_proj",
    )(*stream, shift, scale, g, w, w_qv_t)


def _s5_scan_kernel(uf_ref, ub_ref, fre_ref, fim_ref, ore_ref, oim_ref, toe_ref, are_ref, aim_ref,
                    yf_ref, yb_ref, sre_ref, sim_ref, hre_ref, him_ref):
    @pl.when(pl.program_id(0) == 0)
    def _():
        hre_ref[...] = jnp.zeros_like(hre_ref)
        him_ref[...] = jnp.zeros_like(him_ref)

    bs = SSM_BLOCK_STATES
    rows_c = SCAN_CHUNKS * SUBLANES

    def put(y_ref, q, y, first):
        for j in range(SCAN_LC):
            yj = y[:, j * LANES:(j + 1) * LANES].reshape(SCAN_CHUNKS, SUBLANES, LANES)
            y_ref[q, :, j] = yj if first else y_ref[q, :, j] + yj

    for d, (u_ref, y_ref) in enumerate(((uf_ref, yf_ref), (ub_ref, yb_ref))):
        for q in range(SSM_BLOCKS):
            uc = jnp.concatenate([u_ref[q, :, j].reshape(rows_c, LANES) for j in range(SCAN_LC)],
                                 axis=1).astype(BF16)
            sre_ref[d, :, q * bs:(q + 1) * bs] = _dot(uc, fre_ref[d, q])
            sim_ref[d, :, q * bs:(q + 1) * bs] = _dot(uc, fim_ref[d, q])
            put(y_ref, q, _dot(uc, toe_ref[d, q]), True)

        for q in range(SSM_BLOCKS):
            cols = slice(q * bs, (q + 1) * bs)
            ar = are_ref[d, :, cols]
            ai = aim_ref[d, :, cols]

            def step(c, carry, d=d, cols=cols, ar=ar, ai=ai):
                hr, hi = carry
                cc = c if d == 0 else SCAN_CHUNKS - 1 - c
                rows = pl.ds(pl.multiple_of(cc * SUBLANES, SUBLANES), SUBLANES)
                inc_r = sre_ref[d, rows, cols]
                inc_i = sim_ref[d, rows, cols]
                sre_ref[d, rows, cols] = hr
                sim_ref[d, rows, cols] = hi
                return ar * hr - ai * hi + inc_r, ar * hi + ai * hr + inc_i

            hr, hi = lax.fori_loop(0, SCAN_CHUNKS, step, (hre_ref[d, :, cols], him_ref[d, :, cols]),
                                   unroll=4)
            hre_ref[d, :, cols] = hr
            him_ref[d, :, cols] = hi

        for q in range(SSM_BLOCKS):
            cols = slice(q * bs, (q + 1) * bs)
            y = (_dot(sre_ref[d, :, cols].astype(BF16), ore_ref[d, q])
                 - _dot(sim_ref[d, :, cols].astype(BF16), oim_ref[d, q]))
            put(y_ref, q, y, False)


def _s5_scan(u_tm, f_re, f_im, o_re, o_im, toe, a_re, a_im, seq):
    rows_total = u_tm.shape[1]
    bsz = rows_total // seq
    assert bsz == SUBLANES and SCAN_T % SCAN_LC == 0 and CTX_LEN % SCAN_T == 0
    nt = seq // SCAN_T
    nc = CTX_LEN // SCAN_T
    chunked = (SSM_BLOCKS, seq // SCAN_LC, SCAN_LC, bsz, LANES)
    block = (SSM_BLOCKS, SCAN_CHUNKS, SCAN_LC, bsz, LANES)

    def bwd_tile(i):
        return jnp.where(i < nc, nc - 1 - i, nt - 1 - (i - nc))

    fwd_spec = pl.BlockSpec(block, lambda i: (0, i, 0, 0, 0))
    bwd_spec = pl.BlockSpec(block, lambda i: (0, bwd_tile(i), 0, 0, 0))
    full = lambda a: pl.BlockSpec(a.shape, lambda i: (0,) * a.ndim, pipeline_mode=pl.Buffered(1))
    y_sds = jax.ShapeDtypeStruct(chunked, F32)
    u5 = u_tm.reshape(chunked)
    yf, yb = pl.pallas_call(
        _s5_scan_kernel,
        grid=(nt,),
        in_specs=[fwd_spec, bwd_spec, full(f_re), full(f_im), full(o_re), full(o_im), full(toe),
                  full(a_re), full(a_im)],
        out_specs=[fwd_spec, bwd_spec],
        out_shape=[y_sds, y_sds],
        scratch_shapes=[pltpu.VMEM((2, SCAN_CHUNKS * bsz, SSM_STATES), F32),
                        pltpu.VMEM((2, SCAN_CHUNKS * bsz, SSM_STATES), F32),
                        pltpu.VMEM((2, bsz, SSM_STATES), F32),
                        pltpu.VMEM((2, bsz, SSM_STATES), F32)],
        compiler_params=_cparams("arbitrary"),
        name="s5_scan",
    )(u5, u5, f_re, f_im, o_re, o_im, toe, a_re, a_im)
    return yf.reshape(u_tm.shape), yb.reshape(u_tm.shape)


def _attention_kernel(qt_ref, k_ref, vt_ref, bias_ref, o_ref, s_even_ref, s_odd_ref, *, grid_rows):
    i = pl.program_id(1)
    hd = NA_HEAD_DIM
    qt = (qt_ref[...].astype(F32) * (hd ** -0.5 * LOG2_E)).astype(BF16)
    zeros = jnp.zeros((hd, TM), BF16)
    ones = jnp.ones((BF16_ROWS, NA_KEY_BLOCK), BF16)
    ctx_rows = slice(0, CTX_LEN)

    def pair_cols(h):
        return slice((h // 2) * LANES, (h // 2 + 1) * LANES)

    def padded_qt(h):
        qh = qt[h * hd:(h + 1) * hd]
        return jnp.concatenate([qh, zeros] if h % 2 == 0 else [zeros, qh], axis=0)

    @pl.when(i == 0)
    def _():
        for h in range(NA_HEADS):
            s = _dot(k_ref[ctx_rows, pair_cols(h)], padded_qt(h))
            p = jnp.exp2(s - jnp.max(s, axis=0, keepdims=True)).astype(BF16)
            ol = _dot(jnp.concatenate([vt_ref[0, h * hd:(h + 1) * hd, :], ones], axis=0), p)
            o_ref[h * hd:(h + 1) * hd, :] = (ol[0:hd] / ol[hd:hd + 1]).astype(BF16)

    @pl.when(i > 0)
    def _():
        r_a = (i - 1) * NA_QROWS
        k_start = jnp.clip(r_a - WIN_H // 2, 0, grid_rows - NA_KROWS)
        start = pl.multiple_of(CTX_LEN + k_start * GRID_W, NA_KEY_BLOCK)
        kb = NA_KEY_BLOCK
        blk0 = start // kb
        n_loc = NA_KROWS * GRID_W // kb
        n_all = n_loc + 1
        k_rows = [pl.ds(start + j * kb, kb) for j in range(n_loc)] + [ctx_rows]
        v_blocks = [blk0 + j for j in range(n_loc)] + [0]
        rhs = [padded_qt(h) for h in range(NA_HEADS)]
        s_bufs = (s_even_ref, s_odd_ref)

        def scores(h, j):
            s = _dot(k_ref[k_rows[j], pair_cols(h)], rhs[h])
            if j < n_loc:
                s = s + bias_ref[h, j * kb:(j + 1) * kb, :]
            s_bufs[h % 2][j * kb:(j + 1) * kb] = s
            return jnp.max(s, axis=0, keepdims=True)

        m = functools.reduce(jnp.maximum, [scores(0, j) for j in range(n_all)])
        for h in range(NA_HEADS):
            ol = None
            next_max = []
            for j in range(n_all):
                if h + 1 < NA_HEADS:
                    next_max.append(scores(h + 1, j))
                p = jnp.exp2(s_bufs[h % 2][j * kb:(j + 1) * kb] - m).astype(BF16)
                lhs = jnp.concatenate([vt_ref[v_blocks[j], h * hd:(h + 1) * hd, :], ones], axis=0)
                part = _dot(lhs, p)
                ol = part if ol is None else ol + part
            o_ref[h * hd:(h + 1) * hd, :] = (ol[0:hd] / ol[hd:hd + 1]).astype(BF16)
            if next_max:
                m = functools.reduce(jnp.maximum, next_max)


def _na_tile_classes(grid_rows):
    return ((0, 0), (NA_QROWS, 0), (grid_rows - NA_QROWS, grid_rows - NA_KROWS))


def _na_valid(grid_rows):
    kr = np.arange(NA_KROWS)[:, None, None, None]
    kc = np.arange(GRID_W)[None, :, None, None]
    rq = np.arange(NA_QROWS)[None, None, :, None]
    cq = np.arange(GRID_W)[None, None, None, :]
    oks = []
    for r_a, k_start in _na_tile_classes(grid_rows):
        r0 = np.clip(r_a + rq - WIN_H // 2, 0, grid_rows - WIN_H)
        krow = k_start + kr
        cs = np.clip(cq - WIN_W // 2, 0, GRID_W - WIN_W)
        ok = (krow >= r0) & (krow < r0 + WIN_H) & (kc >= cs) & (kc < cs + WIN_W)
        oks.append(ok.reshape(NA_KROWS * GRID_W, TM))
    return np.stack(oks)


NA_DR_MARGIN = NA_QROWS
NA_DR_ROWS = 2 * WIN_H - 1 + 2 * NA_DR_MARGIN + 1


def _na_bias_kernel(w_ref, mask_ref, out_ref, *, grid_rows):
    lo_half = lax.broadcasted_iota(jnp.int32, (GRID_W, LANES), 1) < GRID_W
    cache = {}

    def toeplitz(d, upper):
        if (d, upper) not in cache:
            row = jnp.broadcast_to(w_ref[d:d + 1, :], (GRID_W, LANES))
            shift = GRID_W + 1 + (GRID_W if upper else 0)
            cache[d, upper] = pltpu.roll(row, shift, 1, stride=1, stride_axis=0)
        return cache[d, upper]

    for cls, (r_a, k_start) in enumerate(_na_tile_classes(grid_rows)):
        for kr in range(NA_KROWS):
            for pair in range(NA_QROWS // 2):
                d = k_start + kr - (r_a + 2 * pair) + (WIN_H - 1) + NA_DR_MARGIN
                block = jnp.where(lo_half, toeplitz(d, False), toeplitz(d - 1, True))
                rows = slice(kr * GRID_W, (kr + 1) * GRID_W)
                cols = slice(pair * LANES, (pair + 1) * LANES)
                out_ref[cls, rows, cols] = block + mask_ref[cls, rows, cols]


def _na_bias_tables(rpb, grid_rows):
    depth, heads, n_dr, n_dc = rpb.shape
    n_keys = NA_KROWS * GRID_W
    first_lane = (GRID_W - 1) - (WIN_W - 1)
    w = jnp.pad(jnp.flip(rpb * LOG2_E, axis=-1),
                ((0, 0), (0, 0), (NA_DR_MARGIN, NA_DR_ROWS - n_dr - NA_DR_MARGIN),
                 (first_lane, LANES - first_lane - n_dc)))
    mask = jnp.asarray(np.where(_na_valid(grid_rows), 0.0, NEG_INF).astype(np.float32))
    return pl.pallas_call(
        functools.partial(_na_bias_kernel, grid_rows=grid_rows),
        grid=(depth, heads),
        in_specs=[pl.BlockSpec((None, None, NA_DR_ROWS, LANES), lambda l, h: (l, h, 0, 0)),
                  pl.BlockSpec((3, n_keys, TM), lambda l, h: (0, 0, 0))],
        out_specs=pl.BlockSpec((None, 3, None, n_keys, TM), lambda l, h: (l, 0, h, 0, 0)),
        out_shape=jax.ShapeDtypeStruct((depth, 3, heads, n_keys, TM), F32),
        compiler_params=pltpu.CompilerParams(dimension_semantics=("parallel", "parallel")),
        name="na_bias",
    )(w, mask)


def _attention(qt, k, vt, bias, grid_rows):
    bsz, s, _ = k.shape
    nt = s // TM
    n_keys = NA_KROWS * GRID_W

    def bias_class(b, i):
        return (jnp.where(i <= 1, 0, jnp.where(i == nt - 1, 2, 1)), 0, 0, 0)

    return pl.pallas_call(
        functools.partial(_attention_kernel, grid_rows=grid_rows),
        grid=(bsz, nt),
        in_specs=[_fm_spec(),
                  pl.BlockSpec((None, s, NA_W), lambda b, i: (b, 0, 0)),
                  pl.BlockSpec((None, nt, NA_W, TM), lambda b, i: (b, 0, 0, 0)),
                  pl.BlockSpec((None, NA_HEADS, n_keys, TM), bias_class)],
        out_specs=_fm_spec(),
        out_shape=jax.ShapeDtypeStruct((bsz, nt, NA_W, TM), BF16),
        scratch_shapes=[pltpu.VMEM((n_keys + CTX_LEN, TM), F32),
                        pltpu.VMEM((n_keys + CTX_LEN, TM), F32)],
        compiler_params=_cparams("parallel", "arbitrary"),
        name="attention",
    )(qt, k, vt, bias)


def _gelu_tanh(x):
    return 0.5 * x * (1.0 + jnp.tanh(math.sqrt(2.0 / math.pi) * (x + 0.044715 * (x * x * x))))


def _merge_kernel(*refs, n_tiles, first):
    c_ref = refs[1] if first else None
    x_ref = refs[0]
    (cg_ref, cprev_ref, cnext_ref, u_ref, yf_ref, yb_ref, o_ref, gate_ref, g1_ref, cw_ref, cout_ref,
     d_ref, glua_ref, glub_ref, naout_ref, wout_ref, out_ref) = refs[2:] if first else refs[1:]
    i = pl.program_id(0)
    rows = _batch_rows(pl.program_id(1))

    def slab(ref):
        return jnp.concatenate([ref[q, rows, :] for q in range(SSM_BLOCKS)], axis=1)

    def gated_input(ref):
        return ref[:, 2 * CONV_W:3 * CONV_W].astype(F32) * ref[:, 0:CONV_W].astype(F32)

    z = gated_input(cg_ref)
    has_prev = (i >= 2).astype(F32)
    has_next = jnp.logical_and(i != 0, i != n_tiles - 1).astype(F32)
    z_before = gated_input(cprev_ref)[BF16_ROWS - 1:BF16_ROWS] * has_prev
    z_after = gated_input(cnext_ref)[0:1] * has_next
    row = lax.broadcasted_iota(jnp.int32, z.shape, 0)
    z_prev = jnp.where(row == 0, z_before, pltpu.roll(z, 1, 0))
    z_next = jnp.where(row == TM - 1, z_after, pltpu.roll(z, TM - 1, 0))
    conv = z_prev * cw_ref[0:1, :] + z * cw_ref[1:2, :] + z_next * cw_ref[2:3, :]
    a_pre = cg_ref[:, CONV_W:2 * CONV_W].astype(F32) * conv
    ya = _dot(a_pre.astype(BF16), cout_ref[...])

    y = slab(yf_ref) + slab(yb_ref) + d_ref[...] * slab(u_ref)
    g = _gelu_tanh(y).astype(BF16)
    def twice_logistic(half_x):
        return jnp.tanh(half_x) + 1.0

    yb = _dot(g, glua_ref[...]) * twice_logistic(_dot(g, glub_ref[...]))

    yc = lax.dot_general(o_ref[...], naout_ref[...], (((0,), (0,)), ((), ())),
                         preferred_element_type=F32)

    mix2 = (twice_logistic(gate_ref[:, 0:D_MODEL].astype(F32)) * ya
            + twice_logistic(gate_ref[:, D_MODEL:2 * D_MODEL].astype(F32)) * yb
            + twice_logistic(gate_ref[:, 2 * D_MODEL:3 * D_MODEL].astype(F32)) * yc)
    out_ref[...] = _stream_tile(x_ref, c_ref) + g1_ref[...] * _dot(mix2.astype(BF16), wout_ref[...])


def _merge(stream, conv_g, u_tm, yf, yb, attn_o, gates, gate1, conv_w, conv_out, s5_d, glu_a, glu_b,
           na_out, w_out):
    first = len(stream) == 2
    bsz, s, _ = conv_g.shape
    nt = s // TM
    halo_blocks = s // BF16_ROWS
    per_tile = TM // BF16_ROWS
    tok = lambda width: _tok_spec(width, batch_major=False)
    tm_spec = _slab_spec()
    prev_spec = pl.BlockSpec((None, BF16_ROWS, 3 * CONV_W),
                             lambda i, b: (b, jnp.maximum(i * per_tile - 1, 0), 0))
    next_spec = pl.BlockSpec((None, BF16_ROWS, 3 * CONV_W),
                             lambda i, b: (b, jnp.minimum((i + 1) * per_tile, halo_blocks - 1), 0))
    return pl.pallas_call(
        functools.partial(_merge_kernel, n_tiles=nt, first=first),
        grid=(nt, bsz),
        in_specs=_stream_specs(first) + [
            tok(3 * CONV_W), prev_spec, next_spec, tm_spec, tm_spec, tm_spec,
            _fm_spec(batch_major=False), tok(3 * D_MODEL), _mod_spec(batch_major=False),
            _const_spec((3, CONV_W)), _const_spec((CONV_W, D_MODEL)), _const_spec((1, SSM_W)),
            _const_spec((SSM_W, D_MODEL)), _const_spec((SSM_W, D_MODEL)),
            _const_spec((NA_W, D_MODEL)), _const_spec((D_MODEL, D_MODEL))],
        out_specs=tok(D_MODEL),
        out_shape=jax.ShapeDtypeStruct((bsz, s, D_MODEL), F32),
        compiler_params=_cparams("parallel", "arbitrary"),
        name="merge",
    )(*stream, conv_g, conv_g, conv_g, u_tm, yf, yb, attn_o, gates, gate1, conv_w, conv_out, s5_d,
      glu_a, glu_b, na_out, w_out)


def _mlp_kernel(x_ref, sh_ref, sc_ref, gt_ref, g_ref, w1_ref, w2_ref, fg_ref, out_ref, *, final):
    x = x_ref[...]
    h = (_rms(x, g_ref[...]) * (1.0 + sc_ref[...]) + sh_ref[...]).astype(BF16)
    acc = jnp.zeros_like(x)
    for c in range(MLP_HIDDEN // D_MODEL):
        cols = slice(c * D_MODEL, (c + 1) * D_MODEL)
        a = jnp.maximum(_dot(h, w1_ref[:, cols]), 0.0)
        acc = acc + _dot((a * a).astype(BF16), w2_ref[cols, :])
    y = x + gt_ref[...] * acc
    if final:
        y = _rms(y, fg_ref[...])
    out_ref[...] = y


def _mlp(xc, shift, scale, gate, g, w1, w2, final_g, final):
    bsz, s, _ = xc.shape
    first = CTX_LEN // TM if final else 0
    nt = s // TM - first
    tok_in = pl.BlockSpec((None, TM, D_MODEL), lambda b, i: (b, i + first, 0))
    mod = pl.BlockSpec((None, None, 1, D_MODEL), lambda b, i: (b, jnp.minimum(i + first, 1), 0, 0))
    return pl.pallas_call(
        functools.partial(_mlp_kernel, final=final),
        grid=(bsz, nt),
        in_specs=[tok_in, mod, mod, mod, _const_spec((1, D_MODEL)),
                  _const_spec((D_MODEL, MLP_HIDDEN)), _const_spec((MLP_HIDDEN, D_MODEL)),
                  _const_spec((1, D_MODEL))],
        out_specs=_tok_spec(D_MODEL),
        out_shape=jax.ShapeDtypeStruct((bsz, nt * TM, D_MODEL), F32),
        compiler_params=_cparams("parallel", "parallel"),
        name="mlp",
    )(xc, shift, scale, gate, g, w1, w2, final_g)


def kernel(x, c, ctx, c_ctx, w_mod, b_mod, norm1_g, w_in, conv_w, conv_out, s5_lam_re, s5_lam_im,
           s5_log_step, s5_b_re, s5_b_im, s5_c_re, s5_c_im, s5_d, s5_glu_a, s5_glu_b, na_rpb,
           na_out, w_out, norm2_g, mlp_w1, mlp_w2, final_norm_g):
    bsz, seq, _ = x.shape
    depth = w_mod.shape[0]
    assert ctx.shape[1] == CTX_LEN == TM and seq % TM == 0 and bsz == SUBLANES
    grid_rows = seq // GRID_W
    assert grid_rows >= NA_KROWS and grid_rows % NA_QROWS == 0
    s = CTX_LEN + seq

    cond_rows = 2 * SUBLANES
    cond = jnp.zeros((cond_rows, D_MODEL), F32).at[:bsz].set(c).at[bsz].set(c_ctx)
    mods = _adaln(cond, w_mod, b_mod).reshape(depth, cond_rows, N_MOD, D_MODEL)
    lat = mods[:, :bsz]
    cx = jnp.broadcast_to(mods[:, bsz:bsz + 1], lat.shape)
    mod = jnp.stack([cx, lat], axis=2)[:, :, :, :, None, :]

    s5_mats = _s5_matrices(s5_lam_re, s5_lam_im, s5_log_step, s5_b_re, s5_b_im, s5_c_re, s5_c_im)

    bf = lambda w: w.astype(BF16)
    gate_cols = (jnp.arange(IN_PROJ_W) >= IN_OFF[7])
    w_in_b = bf(w_in * jnp.where(gate_cols, 0.5, 1.0).astype(F32))
    conv_out_b, glu_a_b, glu_b_b = bf(conv_out), bf(s5_glu_a * 0.5), bf(s5_glu_b * 0.5)
    na_out_b, w_out_b, w1_b, w2_b = bf(na_out), bf(w_out * 0.5), bf(mlp_w1), bf(mlp_w2)
    w_qv_t = jnp.swapaxes(jnp.concatenate([w_in_b[:, :, IN_OFF[4]:IN_OFF[5]],
                                           w_in_b[:, :, IN_OFF[6]:IN_OFF[7]]], axis=2), 1, 2)

    na_bias = _na_bias_tables(na_rpb, grid_rows)

    stream = (x, ctx)
    row = lambda v: v.reshape(1, -1)
    for l in range(depth):
        m = mod[l]
        conv_g, u_tm, qt, k, vt, gates = _in_proj(stream, m[:, :, 0], m[:, :, 1], row(norm1_g[l]),
                                                  w_in_b[l], w_qv_t[l])
        yf, yb = _s5_scan(u_tm, *(mat[l] for mat in s5_mats), s)
        attn_o = _attention(qt, k, vt, na_bias[l], grid_rows)
        xc = _merge(stream, conv_g, u_tm, yf, yb, attn_o, gates, m[:, :, 2], conv_w[l],
                    conv_out_b[l], row(s5_d[l]), glu_a_b[l], glu_b_b[l], na_out_b[l], w_out_b[l])
        xc = _mlp(xc, m[:, :, 3], m[:, :, 4], m[:, :, 5], row(norm2_g[l]), w1_b[l], w2_b[l],
                  row(final_norm_g), final=(l == depth - 1))
        stream = (xc,)
    return xc
```

```python
import functools
import math

import numpy as np
import jax
import jax.numpy as jnp
from jax import lax
from jax.experimental import pallas as pl
from jax.experimental.pallas import tpu as pltpu

D_MODEL = 1024
CTX_LEN = 256
GRID_W = 64
N_MOD = 6
CONV_W = 512
SSM_W = 512
SSM_GROUP = 16
SSM_GROUPS = SSM_W // SSM_GROUP
SSM_STATE = 64
NA_HEADS = 8
NA_HEAD_DIM = 64
NA_W = NA_HEADS * NA_HEAD_DIM
WIN_H = 8
WIN_W = 16
MLP_HIDDEN = 4 * D_MODEL
IN_SIZES = (CONV_W, CONV_W, CONV_W, SSM_W, NA_W, NA_W, NA_W, D_MODEL, D_MODEL, D_MODEL)
IN_OFF = tuple(sum(IN_SIZES[:i]) for i in range(len(IN_SIZES) + 1))
IN_PROJ_W = IN_OFF[-1]
RMS_EPS = 1e-6
NEG_INF = -1e30
S5_MIN_DECAY = 1e-4
LOG2_E = math.log2(math.e)

F32 = jnp.float32
BF16 = jnp.bfloat16

LANES = 128
SUBLANES = 8
BF16_ROWS = 16
TM = 256
SCAN_LC = 4
SCAN_T = 128
SCAN_CHUNKS = SCAN_T // SCAN_LC
NA_QROWS = TM // GRID_W
NA_KROWS = NA_QROWS + WIN_H
NA_KEY_BLOCK = TM
SSM_LANE_GROUPS = LANES // SSM_GROUP
SSM_BLOCKS = SSM_W // LANES
SSM_BLOCK_STATES = SSM_LANE_GROUPS * SSM_STATE
SSM_STATES = SSM_GROUPS * SSM_STATE
VMEM_LIMIT = 56 * 1024 * 1024


def _cparams(*sem):
    return pltpu.CompilerParams(dimension_semantics=sem, vmem_limit_bytes=VMEM_LIMIT)


def _rms(x, g):
    return x * lax.rsqrt(jnp.mean(x * x, axis=-1, keepdims=True) + RMS_EPS) * g


def _dot(a, b):
    return jnp.dot(a, b, preferred_element_type=F32)


def _dot_nt(a, b):
    return lax.dot_general(a, b, (((1,), (1,)), ((), ())), preferred_element_type=F32)


def _adaln_kernel(c_ref, w_ref, b_ref, o_ref):
    c = c_ref[...]
    s = c * jax.nn.sigmoid(c)
    o_ref[...] = jnp.dot(s, w_ref[...], preferred_element_type=F32,
                         precision=lax.Precision.HIGHEST) + b_ref[...]


def _adaln(cond, w_mod, b_mod):
    depth = w_mod.shape[0]
    rows = cond.shape[0]
    n_tiles = (N_MOD * D_MODEL) // D_MODEL
    return pl.pallas_call(
        _adaln_kernel,
        grid=(depth, n_tiles),
        in_specs=[
            pl.BlockSpec((rows, D_MODEL), lambda l, j: (0, 0)),
            pl.BlockSpec((None, D_MODEL, D_MODEL), lambda l, j: (l, 0, j)),
            pl.BlockSpec((None, 1, D_MODEL), lambda l, j: (l, 0, j)),
        ],
        out_specs=pl.BlockSpec((None, rows, D_MODEL), lambda l, j: (l, 0, j)),
        out_shape=jax.ShapeDtypeStruct((depth, rows, N_MOD * D_MODEL), F32),
        compiler_params=_cparams("parallel", "parallel"),
        name="adaln",
    )(cond, w_mod, b_mod.reshape(depth, 1, N_MOD * D_MODEL))


def _s5_prep_kernel(lre_ref, lim_ref, ls_ref, bre_ref, bim_ref, cre_ref, cim_ref,
                    are_ref, aim_ref, fre_ref, fim_ref, ore_ref, oim_ref):
    lr = jnp.minimum(lre_ref[...], -S5_MIN_DECAY)
    li = lim_ref[...]
    dt = jnp.exp(ls_ref[...])
    xr = lr * dt
    xi = li * dt

    def power(k):
        e = jnp.exp(k * xr)
        return e * jnp.cos(k * xi), e * jnp.sin(k * xi)

    ar, ai = power(1.0)
    nr = ar - 1.0
    den = lr * lr + li * li
    cr = (nr * lr + ai * li) / den
    ci = (ai * lr - nr * li) / den
    bre = bre_ref[...]
    bim = bim_ref[...]
    bbr = cr * bre - ci * bim
    bbi = cr * bim + ci * bre
    cre = cre_ref[...]
    cim = cim_ref[...]
    fre_ref[:, 0] = bbr
    fim_ref[:, 0] = bbi
    ore_ref[:, 0] = cre
    oim_ref[:, 0] = cim
    for k in range(1, SCAN_LC + 1):
        pr, pi = power(float(k))
        if k < SCAN_LC:
            fre_ref[:, k] = pr * bbr - pi * bbi
            fim_ref[:, k] = pr * bbi + pi * bbr
        ore_ref[:, k] = pr * cre - pi * cim
        oim_ref[:, k] = pr * cim + pi * cre
    are_ref[...], aim_ref[...] = power(float(SCAN_LC))


def _s5_prep(lam_re, lam_im, log_step, b_re, b_im, c_re, c_im):
    depth = lam_re.shape[0]
    rows = depth * 2 * SSM_GROUPS
    rb = SSM_GROUPS
    lre = lam_re.reshape(rows, 1, SSM_STATE)
    lim = lam_im.reshape(rows, 1, SSM_STATE)
    ls = jnp.broadcast_to(log_step.reshape(rows, 1, 1), (rows, 1, SSM_STATE))
    bre = jnp.swapaxes(b_re.reshape(rows, SSM_STATE, SSM_GROUP), 1, 2)
    bim = jnp.swapaxes(b_im.reshape(rows, SSM_STATE, SSM_GROUP), 1, 2)
    cre = c_re.reshape(rows, SSM_GROUP, SSM_STATE)
    cim = c_im.reshape(rows, SSM_GROUP, SSM_STATE)
    vec = pl.BlockSpec((rb, 1, SSM_STATE), lambda r: (r, 0, 0))
    mat = pl.BlockSpec((rb, SSM_GROUP, SSM_STATE), lambda r: (r, 0, 0))
    pw = lambda n: pl.BlockSpec((rb, n, SSM_GROUP, SSM_STATE), lambda r: (r, 0, 0, 0))
    v2 = jax.ShapeDtypeStruct((rows, 1, SSM_STATE), F32)
    v4 = lambda n: jax.ShapeDtypeStruct((rows, n, SSM_GROUP, SSM_STATE), F32)
    return pl.pallas_call(
        _s5_prep_kernel,
        grid=(rows // rb,),
        in_specs=[vec, vec, vec, mat, mat, mat, mat],
        out_specs=[vec, vec, pw(SCAN_LC), pw(SCAN_LC), pw(SCAN_LC + 1), pw(SCAN_LC + 1)],
        out_shape=[v2, v2, v4(SCAN_LC), v4(SCAN_LC), v4(SCAN_LC + 1), v4(SCAN_LC + 1)],
        compiler_params=pltpu.CompilerParams(dimension_semantics=("parallel",)),
        name="s5_prep",
    )(lre, lim, ls, bre, bim, cre, cim)


def _s5_toe_kernel(bre_ref, bim_ref, ore_ref, oim_ref, k_ref):
    hi = lax.Precision.HIGHEST
    for k in range(SCAN_LC):
        k_ref[k] = (jnp.dot(bre_ref[...], ore_ref[k], preferred_element_type=F32, precision=hi)
                    - jnp.dot(bim_ref[...], oim_ref[k], preferred_element_type=F32, precision=hi))


def _s5_toe(bd_re, bd_im, oc_re, oc_im):
    nb = bd_re.shape[0]
    bspec = pl.BlockSpec((None, LANES, SSM_BLOCK_STATES), lambda r: (r, 0, 0))
    ospec = pl.BlockSpec((None, SCAN_LC, SSM_BLOCK_STATES, LANES), lambda r: (r, 0, 0, 0))
    return pl.pallas_call(
        _s5_toe_kernel,
        grid=(nb,),
        in_specs=[bspec, bspec, ospec, ospec],
        out_specs=pl.BlockSpec((None, SCAN_LC, LANES, LANES), lambda r: (r, 0, 0, 0)),
        out_shape=jax.ShapeDtypeStruct((nb, SCAN_LC, LANES, LANES), F32),
        compiler_params=pltpu.CompilerParams(dimension_semantics=("parallel",)),
        name="s5_toe",
    )(bd_re, bd_im, oc_re, oc_im)


def _block_diag(m):
    ng, a, b = m.shape[-3:]
    rows = m.reshape(m.shape[:-3] + (ng * a, b))
    tiled = jnp.concatenate([rows] * ng, axis=-1)
    on_diagonal = np.kron(np.eye(ng, dtype=bool), np.ones((a, b), dtype=bool))
    return jnp.where(on_diagonal, tiled, jnp.zeros((), m.dtype))


def _s5_matrices(lam_re, lam_im, log_step, b_re, b_im, c_re, c_im):
    depth = lam_re.shape[0]
    lc = SCAN_LC
    a_re, a_im, f_re, f_im, o_re, o_im = _s5_prep(lam_re, lam_im, log_step, b_re, b_im, c_re, c_im)
    lead = (depth, 2, SSM_BLOCKS, SSM_LANE_GROUPS)

    def per_step(x, n):
        return jnp.moveaxis(x.reshape(lead + (n, SSM_GROUP, SSM_STATE)), 4, 3)

    def mirror_bwd(x, axis):
        return jnp.stack([x[:, 0], jnp.flip(x[:, 1], axis=axis - 1)], axis=1)

    def mirror_fwd(x, axis):
        return jnp.stack([jnp.flip(x[:, 0], axis=axis - 1), x[:, 1]], axis=1)

    def fold_mat(f):
        m = _block_diag(mirror_fwd(per_step(f, lc).astype(BF16), 3))
        return m.reshape(depth, 2, SSM_BLOCKS, lc * LANES, SSM_BLOCK_STATES)

    def out_blocks(o, first, dtype):
        steps = per_step(o, lc + 1)[:, :, :, first:first + lc].astype(dtype)
        return _block_diag(jnp.swapaxes(steps, -1, -2))

    def out_mat(o):
        m = mirror_bwd(out_blocks(o, 1, BF16), 3)
        m = jnp.moveaxis(m, 3, 4)
        return m.reshape(depth, 2, SSM_BLOCKS, SSM_BLOCK_STATES, lc * LANES)

    nb = depth * 2 * SSM_BLOCKS
    bd = lambda f: _block_diag(per_step(f, lc)[:, :, :, 0]).reshape(nb, LANES, SSM_BLOCK_STATES)
    oc = lambda o: out_blocks(o, 0, F32).reshape(nb, lc, SSM_BLOCK_STATES, LANES)
    kk = _s5_toe(bd(f_re), bd(f_im), oc(o_re), oc(o_im))
    kk = kk.reshape(depth, 2, SSM_BLOCKS, lc, LANES, LANES)
    zero = jnp.zeros((depth, SSM_BLOCKS, LANES, LANES), F32)
    toes = []
    for d in range(2):
        rows = []
        for j_in in range(lc):
            lag = [(j_out - j_in) if d == 0 else (j_in - j_out) for j_out in range(lc)]
            rows.append(jnp.concatenate([kk[:, d, :, g] if g >= 0 else zero for g in lag], axis=-1))
        toes.append(jnp.concatenate(rows, axis=-2))
    toe = jnp.stack(toes, axis=1).astype(BF16)

    a_shape = (depth, 2, 1, SSM_STATES)
    a_re = jnp.broadcast_to(a_re.reshape(a_shape), (depth, 2, SUBLANES, SSM_STATES))
    a_im = jnp.broadcast_to(a_im.reshape(a_shape), (depth, 2, SUBLANES, SSM_STATES))
    return fold_mat(f_re), fold_mat(f_im), out_mat(o_re), out_mat(o_im), toe, a_re, a_im


def _batch_rows(b):
    return pl.ds(b, TM, stride=SUBLANES)


def _qv_rows_kernel(w_ref, o_ref):
    o_ref[...] = w_ref[...].T.astype(BF16)


def _qv_rows(w_in):
    depth = w_in.shape[0]
    col_blocks = (IN_OFF[4] // NA_W, IN_OFF[6] // NA_W)
    assert IN_OFF[4] % NA_W == 0 and IN_OFF[6] % NA_W == 0
    return pl.pallas_call(
        _qv_rows_kernel,
        grid=(depth, 2),
        in_specs=[pl.BlockSpec((None, D_MODEL, NA_W),
                               lambda l, j: (l, 0, jnp.where(j == 0, col_blocks[0], col_blocks[1])))],
        out_specs=pl.BlockSpec((None, NA_W, D_MODEL), lambda l, j: (l, j, 0)),
        out_shape=jax.ShapeDtypeStruct((depth, 2 * NA_W, D_MODEL), BF16),
        compiler_params=pltpu.CompilerParams(dimension_semantics=("parallel", "parallel")),
        name="qv_rows",
    )(w_in)


def _stream_tile(x_ref, c_ref):
    if c_ref is None:
        return x_ref[...]
    return jnp.where(pl.program_id(0) < CTX_LEN // TM, c_ref[...], x_ref[...])


def _in_proj_kernel(*refs, first):
    c_ref = refs[1] if first else None
    x_ref = refs[0]
    (sh_ref, sc_ref, g_ref, w_ref, wqv_ref, conv_ref, u_ref, qt_ref, k_ref, vt_ref,
     gate_ref) = refs[2:] if first else refs[1:]
    h = _rms(_stream_tile(x_ref, c_ref), g_ref[...]) * (1.0 + sc_ref[...]) + sh_ref[...]
    hb = h.astype(BF16)
    conv_ref[...] = _dot(hb, w_ref[:, IN_OFF[0]:IN_OFF[3]]).astype(BF16)
    u = _dot(hb, w_ref[:, IN_OFF[3]:IN_OFF[4]])
    rows = _batch_rows(pl.program_id(1))
    for q in range(SSM_BLOCKS):
        u_ref[q, rows, :] = u[:, q * LANES:(q + 1) * LANES]
    qt_ref[...] = _dot_nt(wqv_ref[0:NA_W, :], hb).astype(BF16)
    k_ref[...] = _dot(hb, w_ref[:, IN_OFF[5]:IN_OFF[6]]).astype(BF16)
    vt_ref[...] = _dot_nt(wqv_ref[NA_W:2 * NA_W, :], hb).astype(BF16)
    gate_ref[...] = _dot(hb, w_ref[:, IN_OFF[7]:IN_OFF[10]]).astype(BF16)


def _grid_bi(batch_major):
    return (lambda b, i: (b, i)) if batch_major else (lambda i, b: (b, i))


def _tok_spec(width, batch_major=True):
    bi = _grid_bi(batch_major)
    return pl.BlockSpec((None, TM, width), lambda *g: bi(*g) + (0,))


def _mod_spec(batch_major=True):
    bi = _grid_bi(batch_major)

    def index(*g):
        b, i = bi(*g)
        return (b, jnp.minimum(i, 1), 0, 0)

    return pl.BlockSpec((None, None, 1, D_MODEL), index)


def _const_spec(shape):
    nd = len(shape)
    return pl.BlockSpec(shape, lambda *g: (0,) * nd)


def _slab_spec():
    return pl.BlockSpec((SSM_BLOCKS, TM * SUBLANES, LANES), lambda i, b: (0, i, 0))


def _fm_spec(batch_major=True):
    bi = _grid_bi(batch_major)
    return pl.BlockSpec((None, None, NA_W, TM), lambda *g: bi(*g) + (0, 0))


def _stream_specs(first):
    if not first:
        return [_tok_spec(D_MODEL, batch_major=False)]
    nc = CTX_LEN // TM
    return [pl.BlockSpec((None, TM, D_MODEL), lambda i, b: (b, jnp.maximum(i - nc, 0), 0)),
            pl.BlockSpec((None, TM, D_MODEL), lambda i, b: (b, jnp.minimum(i, nc - 1), 0))]


def _in_proj(stream, shift, scale, g, w, w_qv_t):
    first = len(stream) == 2
    bsz = stream[0].shape[0]
    s = stream[0].shape[1] + (CTX_LEN if first else 0)
    nt = s // TM
    tok = lambda width: _tok_spec(width, batch_major=False)
    mod = _mod_spec(batch_major=False)
    fm = _fm_spec(batch_major=False)
    fm_sds = jax.ShapeDtypeStruct((bsz, nt, NA_W, TM), BF16)
    return pl.pallas_call(
        functools.partial(_in_proj_kernel, first=first),
        grid=(nt, bsz),
        in_specs=_stream_specs(first) + [mod, mod, _const_spec((1, D_MODEL)),
                                         _const_spec((D_MODEL, IN_PROJ_W)),
                                         _const_spec((2 * NA_W, D_MODEL))],
        out_specs=[tok(3 * CONV_W), _slab_spec(), fm, tok(NA_W), fm, tok(3 * D_MODEL)],
        out_shape=[jax.ShapeDtypeStruct((bsz, s, 3 * CONV_W), BF16),
                   jax.ShapeDtypeStruct((SSM_BLOCKS, s * bsz, LANES), F32),
                   fm_sds,
                   jax.ShapeDtypeStruct((bsz, s, NA_W), BF16),
                   fm_sds,
                   jax.ShapeDtypeStruct((bsz, s, 3 * D_MODEL), BF16)],
        compiler_params=_cparams("parallel", "arbitrary"),
        name="in_proj",
    )(*stream, shift, scale, g, w, w_qv_t)


def _s5_scan_kernel(uf_ref, ub_ref, fre_ref, fim_ref, ore_ref, oim_ref, toe_ref, are_ref, aim_ref,
                    yf_ref, yb_ref, sre_ref, sim_ref, hre_ref, him_ref):
    @pl.when(pl.program_id(0) == 0)
    def _():
        hre_ref[...] = jnp.zeros_like(hre_ref)
        him_ref[...] = jnp.zeros_like(him_ref)

    bs = SSM_BLOCK_STATES
    rows_c = SCAN_CHUNKS * SUBLANES

    def put(y_ref, q, y, first):
        for j in range(SCAN_LC):
            yj = y[:, j * LANES:(j + 1) * LANES].reshape(SCAN_CHUNKS, SUBLANES, LANES)
            y_ref[q, :, j] = yj if first else y_ref[q, :, j] + yj

    for d, (u_ref, y_ref) in enumerate(((uf_ref, yf_ref), (ub_ref, yb_ref))):
        for q in range(SSM_BLOCKS):
            uc = jnp.concatenate([u_ref[q, :, j].reshape(rows_c, LANES) for j in range(SCAN_LC)],
                                 axis=1).astype(BF16)
            sre_ref[d, :, q * bs:(q + 1) * bs] = _dot(uc, fre_ref[d, q])
            sim_ref[d, :, q * bs:(q + 1) * bs] = _dot(uc, fim_ref[d, q])
            put(y_ref, q, _dot(uc, toe_ref[d, q]), True)

        for q in range(SSM_BLOCKS):
            cols = slice(q * bs, (q + 1) * bs)
            ar = are_ref[d, :, cols]
            ai = aim_ref[d, :, cols]

            def step(c, carry, d=d, cols=cols, ar=ar, ai=ai):
                hr, hi = carry
                cc = c if d == 0 else SCAN_CHUNKS - 1 - c
                rows = pl.ds(pl.multiple_of(cc * SUBLANES, SUBLANES), SUBLANES)
                inc_r = sre_ref[d, rows, cols]
                inc_i = sim_ref[d, rows, cols]
                sre_ref[d, rows, cols] = hr
                sim_ref[d, rows, cols] = hi
                return ar * hr - ai * hi + inc_r, ar * hi + ai * hr + inc_i

            hr, hi = lax.fori_loop(0, SCAN_CHUNKS, step, (hre_ref[d, :, cols], him_ref[d, :, cols]),
                                   unroll=4)
            hre_ref[d, :, cols] = hr
            him_ref[d, :, cols] = hi

        for q in range(SSM_BLOCKS):
            cols = slice(q * bs, (q + 1) * bs)
            y = (_dot(sre_ref[d, :, cols].astype(BF16), ore_ref[d, q])
                 - _dot(sim_ref[d, :, cols].astype(BF16), oim_ref[d, q]))
            put(y_ref, q, y, False)


def _s5_scan(u_tm, f_re, f_im, o_re, o_im, toe, a_re, a_im, seq):
    rows_total = u_tm.shape[1]
    bsz = rows_total // seq
    assert bsz == SUBLANES and SCAN_T % SCAN_LC == 0 and CTX_LEN % SCAN_T == 0
    nt = seq // SCAN_T
    nc = CTX_LEN // SCAN_T
    chunked = (SSM_BLOCKS, seq // SCAN_LC, SCAN_LC, bsz, LANES)
    block = (SSM_BLOCKS, SCAN_CHUNKS, SCAN_LC, bsz, LANES)

    def bwd_tile(i):
        return jnp.where(i < nc, nc - 1 - i, nt - 1 - (i - nc))

    fwd_spec = pl.BlockSpec(block, lambda i: (0, i, 0, 0, 0))
    bwd_spec = pl.BlockSpec(block, lambda i: (0, bwd_tile(i), 0, 0, 0))
    full = lambda a: pl.BlockSpec(a.shape, lambda i: (0,) * a.ndim, pipeline_mode=pl.Buffered(1))
    y_sds = jax.ShapeDtypeStruct(chunked, F32)
    u5 = u_tm.reshape(chunked)
    yf, yb = pl.pallas_call(
        _s5_scan_kernel,
        grid=(nt,),
        in_specs=[fwd_spec, bwd_spec, full(f_re), full(f_im), full(o_re), full(o_im), full(toe),
                  full(a_re), full(a_im)],
        out_specs=[fwd_spec, bwd_spec],
        out_shape=[y_sds, y_sds],
        scratch_shapes=[pltpu.VMEM((2, SCAN_CHUNKS * bsz, SSM_STATES), F32),
                        pltpu.VMEM((2, SCAN_CHUNKS * bsz, SSM_STATES), F32),
                        pltpu.VMEM((2, bsz, SSM_STATES), F32),
                        pltpu.VMEM((2, bsz, SSM_STATES), F32)],
        compiler_params=_cparams("arbitrary"),
        name="s5_scan",
    )(u5, u5, f_re, f_im, o_re, o_im, toe, a_re, a_im)
    return yf.reshape(u_tm.shape), yb.reshape(u_tm.shape)


def _attention_kernel(qt_ref, k_ref, vt_ref, bias_ref, o_ref, s_even_ref, s_odd_ref, *, grid_rows):
    i = pl.program_id(1)
    hd = NA_HEAD_DIM
    qt = (qt_ref[...].astype(F32) * (hd ** -0.5 * LOG2_E)).astype(BF16)
    zeros = jnp.zeros((hd, TM), BF16)
    ones = jnp.ones((BF16_ROWS, NA_KEY_BLOCK), BF16)
    ctx_rows = slice(0, CTX_LEN)

    def pair_cols(h):
        return slice((h // 2) * LANES, (h // 2 + 1) * LANES)

    def padded_qt(h):
        qh = qt[h * hd:(h + 1) * hd]
        return jnp.concatenate([qh, zeros] if h % 2 == 0 else [zeros, qh], axis=0)

    @pl.when(i == 0)
    def _():
        for h in range(NA_HEADS):
            s = _dot(k_ref[ctx_rows, pair_cols(h)], padded_qt(h))
            p = jnp.exp2(s - jnp.max(s, axis=0, keepdims=True)).astype(BF16)
            ol = _dot(jnp.concatenate([vt_ref[0, h * hd:(h + 1) * hd, :], ones], axis=0), p)
            o_ref[h * hd:(h + 1) * hd, :] = (ol[0:hd] / ol[hd:hd + 1]).astype(BF16)

    @pl.when(i > 0)
    def _():
        r_a = (i - 1) * NA_QROWS
        k_start = jnp.clip(r_a - WIN_H // 2, 0, grid_rows - NA_KROWS)
        start = pl.multiple_of(CTX_LEN + k_start * GRID_W, NA_KEY_BLOCK)
        kb = NA_KEY_BLOCK
        blk0 = start // kb
        n_loc = NA_KROWS * GRID_W // kb
        n_all = n_loc + 1
        k_rows = [pl.ds(start + j * kb, kb) for j in range(n_loc)] + [ctx_rows]
        v_blocks = [blk0 + j for j in range(n_loc)] + [0]
        rhs = [padded_qt(h) for h in range(NA_HEADS)]
        s_bufs = (s_even_ref, s_odd_ref)

        def scores(h, j):
            s = _dot(k_ref[k_rows[j], pair_cols(h)], rhs[h])
            if j < n_loc:
                s = s + bias_ref[h, j * kb:(j + 1) * kb, :]
            s_bufs[h % 2][j * kb:(j + 1) * kb] = s
            return jnp.max(s, axis=0, keepdims=True)

        m = functools.reduce(jnp.maximum, [scores(0, j) for j in range(n_all)])
        for h in range(NA_HEADS):
            ol = None
            next_max = []
            for j in range(n_all):
                if h + 1 < NA_HEADS:
                    next_max.append(scores(h + 1, j))
                p = jnp.exp2(s_bufs[h % 2][j * kb:(j + 1) * kb] - m).astype(BF16)
                lhs = jnp.concatenate([vt_ref[v_blocks[j], h * hd:(h + 1) * hd, :], ones], axis=0)
                part = _dot(lhs, p)
                ol = part if ol is None else ol + part
            o_ref[h * hd:(h + 1) * hd, :] = (ol[0:hd] / ol[hd:hd + 1]).astype(BF16)
            if next_max:
                m = functools.reduce(jnp.maximum, next_max)


def _na_tile_classes(grid_rows):
    return ((0, 0), (NA_QROWS, 0), (grid_rows - NA_QROWS, grid_rows - NA_KROWS))


def _na_valid(grid_rows):
    kr = np.arange(NA_KROWS)[:, None, None, None]
    kc = np.arange(GRID_W)[None, :, None, None]
    rq = np.arange(NA_QROWS)[None, None, :, None]
    cq = np.arange(GRID_W)[None, None, None, :]
    oks = []
    for r_a, k_start in _na_tile_classes(grid_rows):
        r0 = np.clip(r_a + rq - WIN_H // 2, 0, grid_rows - WIN_H)
        krow = k_start + kr
        cs = np.clip(cq - WIN_W // 2, 0, GRID_W - WIN_W)
        ok = (krow >= r0) & (krow < r0 + WIN_H) & (kc >= cs) & (kc < cs + WIN_W)
        oks.append(ok.reshape(NA_KROWS * GRID_W, TM))
    return np.stack(oks)


NA_DR_MARGIN = NA_QROWS
NA_DR_ROWS = 2 * WIN_H - 1 + 2 * NA_DR_MARGIN + 1


def _na_bias_kernel(w_ref, mask_ref, out_ref, *, grid_rows):
    lo_half = lax.broadcasted_iota(jnp.int32, (GRID_W, LANES), 1) < GRID_W
    cache = {}

    def toeplitz(d, upper):
        if (d, upper) not in cache:
            row = jnp.broadcast_to(w_ref[d:d + 1, :], (GRID_W, LANES))
            shift = GRID_W + 1 + (GRID_W if upper else 0)
            cache[d, upper] = pltpu.roll(row, shift, 1, stride=1, stride_axis=0)
        return cache[d, upper]

    for cls, (r_a, k_start) in enumerate(_na_tile_classes(grid_rows)):
        for kr in range(NA_KROWS):
            for pair in range(NA_QROWS // 2):
                d = k_start + kr - (r_a + 2 * pair) + (WIN_H - 1) + NA_DR_MARGIN
                block = jnp.where(lo_half, toeplitz(d, False), toeplitz(d - 1, True))
                rows = slice(kr * GRID_W, (kr + 1) * GRID_W)
                cols = slice(pair * LANES, (pair + 1) * LANES)
                out_ref[cls, rows, cols] = block + mask_ref[cls, rows, cols]


def _na_bias_tables(rpb, grid_rows):
    depth, heads, n_dr, n_dc = rpb.shape
    n_keys = NA_KROWS * GRID_W
    first_lane = (GRID_W - 1) - (WIN_W - 1)
    w = jnp.pad(jnp.flip(rpb * LOG2_E, axis=-1),
                ((0, 0), (0, 0), (NA_DR_MARGIN, NA_DR_ROWS - n_dr - NA_DR_MARGIN),
                 (first_lane, LANES - first_lane - n_dc)))
    mask = jnp.asarray(np.where(_na_valid(grid_rows), 0.0, NEG_INF).astype(np.float32))
    return pl.pallas_call(
        functools.partial(_na_bias_kernel, grid_rows=grid_rows),
        grid=(depth, heads),
        in_specs=[pl.BlockSpec((None, None, NA_DR_ROWS, LANES), lambda l, h: (l, h, 0, 0)),
                  pl.BlockSpec((3, n_keys, TM), lambda l, h: (0, 0, 0))],
        out_specs=pl.BlockSpec((None, 3, None, n_keys, TM), lambda l, h: (l, 0, h, 0, 0)),
        out_shape=jax.ShapeDtypeStruct((depth, 3, heads, n_keys, TM), F32),
        compiler_params=pltpu.CompilerParams(dimension_semantics=("parallel", "parallel")),
        name="na_bias",
    )(w, mask)


def _attention(qt, k, vt, bias, grid_rows):
    bsz, s, _ = k.shape
    nt = s // TM
    n_keys = NA_KROWS * GRID_W

    def bias_class(b, i):
        return (jnp.where(i <= 1, 0, jnp.where(i == nt - 1, 2, 1)), 0, 0, 0)

    return pl.pallas_call(
        functools.partial(_attention_kernel, grid_rows=grid_rows),
        grid=(bsz, nt),
        in_specs=[_fm_spec(),
                  pl.BlockSpec((None, s, NA_W), lambda b, i: (b, 0, 0)),
                  pl.BlockSpec((None, nt, NA_W, TM), lambda b, i: (b, 0, 0, 0)),
                  pl.BlockSpec((None, NA_HEADS, n_keys, TM), bias_class)],
        out_specs=_fm_spec(),
        out_shape=jax.ShapeDtypeStruct((bsz, nt, NA_W, TM), BF16),
        scratch_shapes=[pltpu.VMEM((n_keys + CTX_LEN, TM), F32),
                        pltpu.VMEM((n_keys + CTX_LEN, TM), F32)],
        compiler_params=_cparams("parallel", "arbitrary"),
        name="attention",
    )(qt, k, vt, bias)


def _gelu_tanh(x):
    return 0.5 * x * (1.0 + jnp.tanh(math.sqrt(2.0 / math.pi) * (x + 0.044715 * (x * x * x))))


def _merge_kernel(*refs, n_tiles, first):
    c_ref = refs[1] if first else None
    x_ref = refs[0]
    (cg_ref, cprev_ref, cnext_ref, u_ref, yf_ref, yb_ref, o_ref, gate_ref, g1_ref, cw_ref, cout_ref,
     d_ref, glua_ref, glub_ref, naout_ref, wout_ref, out_ref) = refs[2:] if first else refs[1:]
    i = pl.program_id(0)
    rows = _batch_rows(pl.program_id(1))

    def slab(ref):
        return jnp.concatenate([ref[q, rows, :] for q in range(SSM_BLOCKS)], axis=1)

    def gated_input(ref):
        return ref[:, 2 * CONV_W:3 * CONV_W].astype(F32) * ref[:, 0:CONV_W].astype(F32)

    z = gated_input(cg_ref)
    has_prev = (i >= 2).astype(F32)
    has_next = jnp.logical_and(i != 0, i != n_tiles - 1).astype(F32)
    z_before = gated_input(cprev_ref)[BF16_ROWS - 1:BF16_ROWS] * has_prev
    z_after = gated_input(cnext_ref)[0:1] * has_next
    row = lax.broadcasted_iota(jnp.int32, z.shape, 0)
    z_prev = jnp.where(row == 0, z_before, pltpu.roll(z, 1, 0))
    z_next = jnp.where(row == TM - 1, z_after, pltpu.roll(z, TM - 1, 0))
    conv = z_prev * cw_ref[0:1, :] + z * cw_ref[1:2, :] + z_next * cw_ref[2:3, :]
    a_pre = cg_ref[:, CONV_W:2 * CONV_W].astype(F32) * conv
    ya = _dot(a_pre.astype(BF16), cout_ref[...])

    y = slab(yf_ref) + slab(yb_ref) + d_ref[...] * slab(u_ref)
    g = _gelu_tanh(y).astype(BF16)
    def twice_logistic(half_x):
        return jnp.tanh(half_x) + 1.0

    yb = _dot(g, glua_ref[...]) * twice_logistic(_dot(g, glub_ref[...]))

    yc = lax.dot_general(o_ref[...], naout_ref[...], (((0,), (0,)), ((), ())),
                         preferred_element_type=F32)

    mix2 = (twice_logistic(gate_ref[:, 0:D_MODEL].astype(F32)) * ya
            + twice_logistic(gate_ref[:, D_MODEL:2 * D_MODEL].astype(F32)) * yb
            + twice_logistic(gate_ref[:, 2 * D_MODEL:3 * D_MODEL].astype(F32)) * yc)
    out_ref[...] = _stream_tile(x_ref, c_ref) + g1_ref[...] * _dot(mix2.astype(BF16), wout_ref[...])


def _merge(stream, conv_g, u_tm, yf, yb, attn_o, gates, gate1, conv_w, conv_out, s5_d, glu_a, glu_b,
           na_out, w_out):
    first = len(stream) == 2
    bsz, s, _ = conv_g.shape
    nt = s // TM
    halo_blocks = s // BF16_ROWS
    per_tile = TM // BF16_ROWS
    tok = lambda width: _tok_spec(width, batch_major=False)
    tm_spec = _slab_spec()
    prev_spec = pl.BlockSpec((None, BF16_ROWS, 3 * CONV_W),
                             lambda i, b: (b, jnp.maximum(i * per_tile - 1, 0), 0))
    next_spec = pl.BlockSpec((None, BF16_ROWS, 3 * CONV_W),
                             lambda i, b: (b, jnp.minimum((i + 1) * per_tile, halo_blocks - 1), 0))
    return pl.pallas_call(
        functools.partial(_merge_kernel, n_tiles=nt, first=first),
        grid=(nt, bsz),
        in_specs=_stream_specs(first) + [
            tok(3 * CONV_W), prev_spec, next_spec, tm_spec, tm_spec, tm_spec,
            _fm_spec(batch_major=False), tok(3 * D_MODEL), _mod_spec(batch_major=False),
            _const_spec((3, CONV_W)), _const_spec((CONV_W, D_MODEL)), _const_spec((1, SSM_W)),
            _const_spec((SSM_W, D_MODEL)), _const_spec((SSM_W, D_MODEL)),
            _const_spec((NA_W, D_MODEL)), _const_spec((D_MODEL, D_MODEL))],
        out_specs=tok(D_MODEL),
        out_shape=jax.ShapeDtypeStruct((bsz, s, D_MODEL), F32),
        compiler_params=_cparams("parallel", "arbitrary"),
        name="merge",
    )(*stream, conv_g, conv_g, conv_g, u_tm, yf, yb, attn_o, gates, gate1, conv_w, conv_out, s5_d,
      glu_a, glu_b, na_out, w_out)


def _mlp_kernel(x_ref, sh_ref, sc_ref, gt_ref, g_ref, w1_ref, w2_ref, fg_ref, out_ref, *, final):
    x = x_ref[...]
    h = (_rms(x, g_ref[...]) * (1.0 + sc_ref[...]) + sh_ref[...]).astype(BF16)
    acc = jnp.zeros_like(x)
    for c in range(MLP_HIDDEN // D_MODEL):
        cols = slice(c * D_MODEL, (c + 1) * D_MODEL)
        a = jnp.maximum(_dot(h, w1_ref[:, cols]), 0.0)
        acc = acc + _dot((a * a).astype(BF16), w2_ref[cols, :])
    y = x + gt_ref[...] * acc
    if final:
        y = _rms(y, fg_ref[...])
    out_ref[...] = y


def _mlp(xc, shift, scale, gate, g, w1, w2, final_g, final):
    bsz, s, _ = xc.shape
    first = CTX_LEN // TM if final else 0
    nt = s // TM - first
    tok_in = pl.BlockSpec((None, TM, D_MODEL), lambda b, i: (b, i + first, 0))
    mod = pl.BlockSpec((None, None, 1, D_MODEL), lambda b, i: (b, jnp.minimum(i + first, 1), 0, 0))
    return pl.pallas_call(
        functools.partial(_mlp_kernel, final=final),
        grid=(bsz, nt),
        in_specs=[tok_in, mod, mod, mod, _const_spec((1, D_MODEL)),
                  _const_spec((D_MODEL, MLP_HIDDEN)), _const_spec((MLP_HIDDEN, D_MODEL)),
                  _const_spec((1, D_MODEL))],
        out_specs=_tok_spec(D_MODEL),
        out_shape=jax.ShapeDtypeStruct((bsz, nt * TM, D_MODEL), F32),
        compiler_params=_cparams("parallel", "parallel"),
        name="mlp",
    )(xc, shift, scale, gate, g, w1, w2, final_g)


def kernel(x, c, ctx, c_ctx, w_mod, b_mod, norm1_g, w_in, conv_w, conv_out, s5_lam_re, s5_lam_im,
           s5_log_step, s5_b_re, s5_b_im, s5_c_re, s5_c_im, s5_d, s5_glu_a, s5_glu_b, na_rpb,
           na_out, w_out, norm2_g, mlp_w1, mlp_w2, final_norm_g):
    bsz, seq, _ = x.shape
    depth = w_mod.shape[0]
    assert ctx.shape[1] == CTX_LEN == TM and seq % TM == 0 and bsz == SUBLANES
    grid_rows = seq // GRID_W
    assert grid_rows >= NA_KROWS and grid_rows % NA_QROWS == 0
    s = CTX_LEN + seq

    cond_rows = 2 * SUBLANES
    cond = jnp.zeros((cond_rows, D_MODEL), F32).at[:bsz].set(c).at[bsz].set(c_ctx)
    mods = _adaln(cond, w_mod, b_mod).reshape(depth, cond_rows, N_MOD, D_MODEL)
    lat = mods[:, :bsz]
    cx = jnp.broadcast_to(mods[:, bsz:bsz + 1], lat.shape)
    mod = jnp.stack([cx, lat], axis=2)[:, :, :, :, None, :]

    s5_mats = _s5_matrices(s5_lam_re, s5_lam_im, s5_log_step, s5_b_re, s5_b_im, s5_c_re, s5_c_im)

    bf = lambda w: w.astype(BF16)
    gate_cols = (jnp.arange(IN_PROJ_W) >= IN_OFF[7])
    w_in_b = bf(w_in * jnp.where(gate_cols, 0.5, 1.0).astype(F32))
    conv_out_b, glu_a_b, glu_b_b = bf(conv_out), bf(s5_glu_a * 0.5), bf(s5_glu_b * 0.5)
    na_out_b, w_out_b, w1_b, w2_b = bf(na_out), bf(w_out * 0.5), bf(mlp_w1), bf(mlp_w2)
    w_qv_t = _qv_rows(w_in)

    na_bias = _na_bias_tables(na_rpb, grid_rows)

    stream = (x, ctx)
    row = lambda v: v.reshape(1, -1)
    for l in range(depth):
        m = mod[l]
        conv_g, u_tm, qt, k, vt, gates = _in_proj(stream, m[:, :, 0], m[:, :, 1], row(norm1_g[l]),
                                                  w_in_b[l], w_qv_t[l])
        yf, yb = _s5_scan(u_tm, *(mat[l] for mat in s5_mats), s)
        attn_o = _attention(qt, k, vt, na_bias[l], grid_rows)
        xc = _merge(stream, conv_g, u_tm, yf, yb, attn_o, gates, m[:, :, 2], conv_w[l],
                    conv_out_b[l], row(s5_d[l]), glu_a_b[l], glu_b_b[l], na_out_b[l], w_out_b[l])
        xc = _mlp(xc, m[:, :, 3], m[:, :, 4], m[:, :, 5], row(norm2_g[l]), w1_b[l], w2_b[l],
                  row(final_norm_g), final=(l == depth - 1))
        stream = (xc,)
    return xc
```

```python
import functools
import math

import numpy as np
import jax
import jax.numpy as jnp
from jax import lax
from jax.experimental import pallas as pl
from jax.experimental.pallas import tpu as pltpu

D_MODEL = 1024
CTX_LEN = 256
GRID_W = 64
N_MOD = 6
CONV_W = 512
SSM_W = 512
SSM_GROUP = 16
SSM_GROUPS = SSM_W // SSM_GROUP
SSM_STATE = 64
NA_HEADS = 8
NA_HEAD_DIM = 64
NA_W = NA_HEADS * NA_HEAD_DIM
WIN_H = 8
WIN_W = 16
MLP_HIDDEN = 4 * D_MODEL
IN_SIZES = (CONV_W, CONV_W, CONV_W, SSM_W, NA_W, NA_W, NA_W, D_MODEL, D_MODEL, D_MODEL)
IN_OFF = tuple(sum(IN_SIZES[:i]) for i in range(len(IN_SIZES) + 1))
IN_PROJ_W = IN_OFF[-1]
RMS_EPS = 1e-6
NEG_INF = -1e30
S5_MIN_DECAY = 1e-4
LOG2_E = math.log2(math.e)

F32 = jnp.float32
BF16 = jnp.bfloat16

LANES = 128
SUBLANES = 8
BF16_ROWS = 16
TM = 256
SCAN_LC = 4
SCAN_T = 128
SCAN_CHUNKS = SCAN_T // SCAN_LC
NA_QROWS = TM // GRID_W
NA_KROWS = NA_QROWS + WIN_H
NA_KEY_BLOCK = TM
SSM_LANE_GROUPS = LANES // SSM_GROUP
SSM_BLOCKS = SSM_W // LANES
SSM_BLOCK_STATES = SSM_LANE_GROUPS * SSM_STATE
SSM_STATES = SSM_GROUPS * SSM_STATE
VMEM_LIMIT = 56 * 1024 * 1024


def _cparams(*sem):
    return pltpu.CompilerParams(dimension_semantics=sem, vmem_limit_bytes=VMEM_LIMIT)


def _rms(x, g):
    return x * lax.rsqrt(jnp.mean(x * x, axis=-1, keepdims=True) + RMS_EPS) * g


def _dot(a, b):
    return jnp.dot(a, b, preferred_element_type=F32)


def _dot_nt(a, b):
    return lax.dot_general(a, b, (((1,), (1,)), ((), ())), preferred_element_type=F32)


def _adaln_kernel(c_ref, w_ref, b_ref, o_ref):
    c = c_ref[...]
    s = c * jax.nn.sigmoid(c)
    o_ref[...] = jnp.dot(s, w_ref[...], preferred_element_type=F32,
                         precision=lax.Precision.HIGHEST) + b_ref[...]


def _adaln(cond, w_mod, b_mod):
    depth = w_mod.shape[0]
    rows = cond.shape[0]
    n_tiles = (N_MOD * D_MODEL) // D_MODEL
    return pl.pallas_call(
        _adaln_kernel,
        grid=(depth, n_tiles),
        in_specs=[
            pl.BlockSpec((rows, D_MODEL), lambda l, j: (0, 0)),
            pl.BlockSpec((None, D_MODEL, D_MODEL), lambda l, j: (l, 0, j)),
            pl.BlockSpec((None, 1, D_MODEL), lambda l, j: (l, 0, j)),
        ],
        out_specs=pl.BlockSpec((None, rows, D_MODEL), lambda l, j: (l, 0, j)),
        out_shape=jax.ShapeDtypeStruct((depth, rows, N_MOD * D_MODEL), F32),
        compiler_params=_cparams("parallel", "parallel"),
        name="adaln",
    )(cond, w_mod, b_mod.reshape(depth, 1, N_MOD * D_MODEL))


def _s5_prep_kernel(lre_ref, lim_ref, ls_ref, bre_ref, bim_ref, cre_ref, cim_ref,
                    are_ref, aim_ref, fre_ref, fim_ref, ore_ref, oim_ref):
    lr = jnp.minimum(lre_ref[...], -S5_MIN_DECAY)
    li = lim_ref[...]
    dt = jnp.exp(ls_ref[...])
    xr = lr * dt
    xi = li * dt

    def power(k):
        e = jnp.exp(k * xr)
        return e * jnp.cos(k * xi), e * jnp.sin(k * xi)

    ar, ai = power(1.0)
    nr = ar - 1.0
    den = lr * lr + li * li
    cr = (nr * lr + ai * li) / den
    ci = (ai * lr - nr * li) / den
    bre = bre_ref[...]
    bim = bim_ref[...]
    bbr = cr * bre - ci * bim
    bbi = cr * bim + ci * bre
    cre = cre_ref[...]
    cim = cim_ref[...]
    fre_ref[:, 0] = bbr
    fim_ref[:, 0] = bbi
    ore_ref[:, 0] = cre
    oim_ref[:, 0] = cim
    for k in range(1, SCAN_LC + 1):
        pr, pi = power(float(k))
        if k < SCAN_LC:
            fre_ref[:, k] = pr * bbr - pi * bbi
            fim_ref[:, k] = pr * bbi + pi * bbr
        ore_ref[:, k] = pr * cre - pi * cim
        oim_ref[:, k] = pr * cim + pi * cre
    are_ref[...], aim_ref[...] = power(float(SCAN_LC))


def _s5_prep(lam_re, lam_im, log_step, b_re, b_im, c_re, c_im):
    depth = lam_re.shape[0]
    rows = depth * 2 * SSM_GROUPS
    rb = SSM_GROUPS
    lre = lam_re.reshape(rows, 1, SSM_STATE)
    lim = lam_im.reshape(rows, 1, SSM_STATE)
    ls = jnp.broadcast_to(log_step.reshape(rows, 1, 1), (rows, 1, SSM_STATE))
    bre = jnp.swapaxes(b_re.reshape(rows, SSM_STATE, SSM_GROUP), 1, 2)
    bim = jnp.swapaxes(b_im.reshape(rows, SSM_STATE, SSM_GROUP), 1, 2)
    cre = c_re.reshape(rows, SSM_GROUP, SSM_STATE)
    cim = c_im.reshape(rows, SSM_GROUP, SSM_STATE)
    vec = pl.BlockSpec((rb, 1, SSM_STATE), lambda r: (r, 0, 0))
    mat = pl.BlockSpec((rb, SSM_GROUP, SSM_STATE), lambda r: (r, 0, 0))
    pw = lambda n: pl.BlockSpec((rb, n, SSM_GROUP, SSM_STATE), lambda r: (r, 0, 0, 0))
    v2 = jax.ShapeDtypeStruct((rows, 1, SSM_STATE), F32)
    v4 = lambda n: jax.ShapeDtypeStruct((rows, n, SSM_GROUP, SSM_STATE), F32)
    return pl.pallas_call(
        _s5_prep_kernel,
        grid=(rows // rb,),
        in_specs=[vec, vec, vec, mat, mat, mat, mat],
        out_specs=[vec, vec, pw(SCAN_LC), pw(SCAN_LC), pw(SCAN_LC + 1), pw(SCAN_LC + 1)],
        out_shape=[v2, v2, v4(SCAN_LC), v4(SCAN_LC), v4(SCAN_LC + 1), v4(SCAN_LC + 1)],
        compiler_params=pltpu.CompilerParams(dimension_semantics=("parallel",)),
        name="s5_prep",
    )(lre, lim, ls, bre, bim, cre, cim)


def _s5_toe_kernel(bre_ref, bim_ref, ore_ref, oim_ref, k_ref):
    hi = lax.Precision.HIGHEST
    for k in range(SCAN_LC):
        k_ref[k] = (jnp.dot(bre_ref[...], ore_ref[k], preferred_element_type=F32, precision=hi)
                    - jnp.dot(bim_ref[...], oim_ref[k], preferred_element_type=F32, precision=hi))


def _s5_toe(bd_re, bd_im, oc_re, oc_im):
    nb = bd_re.shape[0]
    bspec = pl.BlockSpec((None, LANES, SSM_BLOCK_STATES), lambda r: (r, 0, 0))
    ospec = pl.BlockSpec((None, SCAN_LC, SSM_BLOCK_STATES, LANES), lambda r: (r, 0, 0, 0))
    return pl.pallas_call(
        _s5_toe_kernel,
        grid=(nb,),
        in_specs=[bspec, bspec, ospec, ospec],
        out_specs=pl.BlockSpec((None, SCAN_LC, LANES, LANES), lambda r: (r, 0, 0, 0)),
        out_shape=jax.ShapeDtypeStruct((nb, SCAN_LC, LANES, LANES), F32),
        compiler_params=pltpu.CompilerParams(dimension_semantics=("parallel",)),
        name="s5_toe",
    )(bd_re, bd_im, oc_re, oc_im)


def _block_diag(m):
    ng, a, b = m.shape[-3:]
    rows = m.reshape(m.shape[:-3] + (ng * a, b))
    spread = jnp.asarray(np.tile(np.eye(b, dtype=np.float32), (1, ng)), m.dtype)
    tiled = jnp.dot(rows, spread, precision=lax.Precision.HIGHEST, preferred_element_type=m.dtype)
    on_diagonal = np.kron(np.eye(ng, dtype=bool), np.ones((a, b), dtype=bool))
    return jnp.where(on_diagonal, tiled, jnp.zeros((), m.dtype))


def _s5_matrices(lam_re, lam_im, log_step, b_re, b_im, c_re, c_im):
    depth = lam_re.shape[0]
    lc = SCAN_LC
    a_re, a_im, f_re, f_im, o_re, o_im = _s5_prep(lam_re, lam_im, log_step, b_re, b_im, c_re, c_im)
    lead = (depth, 2, SSM_BLOCKS, SSM_LANE_GROUPS)

    def per_step(x, n):
        return jnp.moveaxis(x.reshape(lead + (n, SSM_GROUP, SSM_STATE)), 4, 3)

    def mirror_bwd(x, axis):
        return jnp.stack([x[:, 0], jnp.flip(x[:, 1], axis=axis - 1)], axis=1)

    def mirror_fwd(x, axis):
        return jnp.stack([jnp.flip(x[:, 0], axis=axis - 1), x[:, 1]], axis=1)

    def fold_mat(f):
        m = _block_diag(mirror_fwd(per_step(f, lc).astype(BF16), 3))
        return m.reshape(depth, 2, SSM_BLOCKS, lc * LANES, SSM_BLOCK_STATES)

    def out_blocks(o, first, dtype):
        steps = per_step(o, lc + 1)[:, :, :, first:first + lc].astype(dtype)
        return _block_diag(jnp.swapaxes(steps, -1, -2))

    def out_mat(o):
        steps = mirror_bwd(per_step(o, lc + 1)[:, :, :, 1:].astype(BF16), 3)
        rows = jnp.transpose(steps, (0, 1, 2, 4, 6, 3, 5))
        rows = rows.reshape(depth, 2, SSM_BLOCKS, SSM_BLOCK_STATES, lc * SSM_GROUP)
        spread = np.kron(np.eye(lc, dtype=np.float32),
                         np.tile(np.eye(SSM_GROUP, dtype=np.float32), (1, SSM_LANE_GROUPS)))
        tiled = jnp.dot(rows, jnp.asarray(spread, BF16), preferred_element_type=BF16)
        row_group = np.arange(SSM_BLOCK_STATES)[:, None] // SSM_STATE
        col_group = (np.arange(lc * LANES)[None, :] % LANES) // SSM_GROUP
        return jnp.where(row_group == col_group, tiled, jnp.zeros((), BF16))

    nb = depth * 2 * SSM_BLOCKS
    bd = lambda f: _block_diag(per_step(f, lc)[:, :, :, 0]).reshape(nb, LANES, SSM_BLOCK_STATES)
    oc = lambda o: out_blocks(o, 0, F32).reshape(nb, lc, SSM_BLOCK_STATES, LANES)
    kk = _s5_toe(bd(f_re), bd(f_im), oc(o_re), oc(o_im))
    kk = kk.reshape(depth, 2, SSM_BLOCKS, lc, LANES, LANES)
    zero = jnp.zeros((depth, SSM_BLOCKS, LANES, LANES), F32)
    toes = []
    for d in range(2):
        rows = []
        for j_in in range(lc):
            lag = [(j_out - j_in) if d == 0 else (j_in - j_out) for j_out in range(lc)]
            rows.append(jnp.concatenate([kk[:, d, :, g] if g >= 0 else zero for g in lag], axis=-1))
        toes.append(jnp.concatenate(rows, axis=-2))
    toe = jnp.stack(toes, axis=1).astype(BF16)

    a_shape = (depth, 2, 1, SSM_STATES)
    a_re = jnp.broadcast_to(a_re.reshape(a_shape), (depth, 2, SUBLANES, SSM_STATES))
    a_im = jnp.broadcast_to(a_im.reshape(a_shape), (depth, 2, SUBLANES, SSM_STATES))
    return fold_mat(f_re), fold_mat(f_im), out_mat(o_re), out_mat(o_im), toe, a_re, a_im


def _batch_rows(b):
    return pl.ds(b, TM, stride=SUBLANES)


def _qv_rows_kernel(w_ref, o_ref):
    o_ref[...] = w_ref[...].T.astype(BF16)


def _qv_rows(w_in):
    depth = w_in.shape[0]
    col_blocks = (IN_OFF[4] // NA_W, IN_OFF[6] // NA_W)
    assert IN_OFF[4] % NA_W == 0 and IN_OFF[6] % NA_W == 0
    return pl.pallas_call(
        _qv_rows_kernel,
        grid=(depth, 2),
        in_specs=[pl.BlockSpec((None, D_MODEL, NA_W),
                               lambda l, j: (l, 0, jnp.where(j == 0, col_blocks[0], col_blocks[1])))],
        out_specs=pl.BlockSpec((None, NA_W, D_MODEL), lambda l, j: (l, j, 0)),
        out_shape=jax.ShapeDtypeStruct((depth, 2 * NA_W, D_MODEL), BF16),
        compiler_params=pltpu.CompilerParams(dimension_semantics=("parallel", "parallel")),
        name="qv_rows",
    )(w_in)


def _stream_tile(x_ref, c_ref):
    if c_ref is None:
        return x_ref[...]
    return jnp.where(pl.program_id(0) < CTX_LEN // TM, c_ref[...], x_ref[...])


def _in_proj_kernel(*refs, first):
    c_ref = refs[1] if first else None
    x_ref = refs[0]
    (sh_ref, sc_ref, g_ref, w_ref, wqv_ref, conv_ref, u_ref, qt_ref, k_ref, vt_ref,
     gate_ref) = refs[2:] if first else refs[1:]
    h = _rms(_stream_tile(x_ref, c_ref), g_ref[...]) * (1.0 + sc_ref[...]) + sh_ref[...]
    hb = h.astype(BF16)
    conv_ref[...] = _dot(hb, w_ref[:, IN_OFF[0]:IN_OFF[3]]).astype(BF16)
    u = _dot(hb, w_ref[:, IN_OFF[3]:IN_OFF[4]])
    rows = _batch_rows(pl.program_id(1))
    for q in range(SSM_BLOCKS):
        u_ref[q, rows, :] = u[:, q * LANES:(q + 1) * LANES]
    qt_ref[...] = _dot_nt(wqv_ref[0:NA_W, :], hb).astype(BF16)
    k_ref[...] = _dot(hb, w_ref[:, IN_OFF[5]:IN_OFF[6]]).astype(BF16)
    vt_ref[...] = _dot_nt(wqv_ref[NA_W:2 * NA_W, :], hb).astype(BF16)
    gate_ref[...] = _dot(hb, w_ref[:, IN_OFF[7]:IN_OFF[10]]).astype(BF16)


def _grid_bi(batch_major):
    return (lambda b, i: (b, i)) if batch_major else (lambda i, b: (b, i))


def _tok_spec(width, batch_major=True):
    bi = _grid_bi(batch_major)
    return pl.BlockSpec((None, TM, width), lambda *g: bi(*g) + (0,))


def _mod_spec(batch_major=True):
    bi = _grid_bi(batch_major)

    def index(*g):
        b, i = bi(*g)
        return (b, jnp.minimum(i, 1), 0, 0)

    return pl.BlockSpec((None, None, 1, D_MODEL), index)


def _const_spec(shape):
    nd = len(shape)
    return pl.BlockSpec(shape, lambda *g: (0,) * nd)


def _slab_spec():
    return pl.BlockSpec((SSM_BLOCKS, TM * SUBLANES, LANES), lambda i, b: (0, i, 0))


def _fm_spec(batch_major=True):
    bi = _grid_bi(batch_major)
    return pl.BlockSpec((None, None, NA_W, TM), lambda *g: bi(*g) + (0, 0))


def _stream_specs(first):
    if not first:
        return [_tok_spec(D_MODEL, batch_major=False)]
    nc = CTX_LEN // TM
    return [pl.BlockSpec((None, TM, D_MODEL), lambda i, b: (b, jnp.maximum(i - nc, 0), 0)),
            pl.BlockSpec((None, TM, D_MODEL), lambda i, b: (b, jnp.minimum(i, nc - 1), 0))]


def _in_proj(stream, shift, scale, g, w, w_qv_t):
    first = len(stream) == 2
    bsz = stream[0].shape[0]
    s = stream[0].shape[1] + (CTX_LEN if first else 0)
    nt = s // TM
    tok = lambda width: _tok_spec(width, batch_major=False)
    mod = _mod_spec(batch_major=False)
    fm = _fm_spec(batch_major=False)
    fm_sds = jax.ShapeDtypeStruct((bsz, nt, NA_W, TM), BF16)
    return pl.pallas_call(
        functools.partial(_in_proj_kernel, first=first),
        grid=(nt, bsz),
        in_specs=_stream_specs(first) + [mod, mod, _const_spec((1, D_MODEL)),
                                         _const_spec((D_MODEL, IN_PROJ_W)),
                                         _const_spec((2 * NA_W, D_MODEL))],
        out_specs=[tok(3 * CONV_W), _slab_spec(), fm, tok(NA_W), fm, tok(3 * D_MODEL)],
        out_shape=[jax.ShapeDtypeStruct((bsz, s, 3 * CONV_W), BF16),
                   jax.ShapeDtypeStruct((SSM_BLOCKS, s * bsz, LANES), F32),
                   fm_sds,
                   jax.ShapeDtypeStruct((bsz, s, NA_W), BF16),
                   fm_sds,
                   jax.ShapeDtypeStruct((bsz, s, 3 * D_MODEL), BF16)],
        compiler_params=_cparams("parallel", "arbitrary"),
        name="in_proj",
    )(*stream, shift, scale, g, w, w_qv_t)


def _s5_scan_kernel(uf_ref, ub_ref, fre_ref, fim_ref, ore_ref, oim_ref, toe_ref, are_ref, aim_ref,
                    yf_ref, yb_ref, sre_ref, sim_ref, hre_ref, him_ref):
    @pl.when(pl.program_id(0) == 0)
    def _():
        hre_ref[...] = jnp.zeros_like(hre_ref)
        him_ref[...] = jnp.zeros_like(him_ref)

    bs = SSM_BLOCK_STATES
    rows_c = SCAN_CHUNKS * SUBLANES

    def put(y_ref, q, y, first):
        for j in range(SCAN_LC):
            yj = y[:, j * LANES:(j + 1) * LANES].reshape(SCAN_CHUNKS, SUBLANES, LANES)
            y_ref[q, :, j] = yj if first else y_ref[q, :, j] + yj

    for d, (u_ref, y_ref) in enumerate(((uf_ref, yf_ref), (ub_ref, yb_ref))):
        for q in range(SSM_BLOCKS):
            uc = jnp.concatenate([u_ref[q, :, j].reshape(rows_c, LANES) for j in range(SCAN_LC)],
                                 axis=1).astype(BF16)
            sre_ref[d, :, q * bs:(q + 1) * bs] = _dot(uc, fre_ref[d, q])
            sim_ref[d, :, q * bs:(q + 1) * bs] = _dot(uc, fim_ref[d, q])
            put(y_ref, q, _dot(uc, toe_ref[d, q]), True)

        for q in range(SSM_BLOCKS):
            cols = slice(q * bs, (q + 1) * bs)
            ar = are_ref[d, :, cols]
            ai = aim_ref[d, :, cols]

            def step(c, carry, d=d, cols=cols, ar=ar, ai=ai):
                hr, hi = carry
                cc = c if d == 0 else SCAN_CHUNKS - 1 - c
                rows = pl.ds(pl.multiple_of(cc * SUBLANES, SUBLANES), SUBLANES)
                inc_r = sre_ref[d, rows, cols]
                inc_i = sim_ref[d, rows, cols]
                sre_ref[d, rows, cols] = hr
                sim_ref[d, rows, cols] = hi
                return ar * hr - ai * hi + inc_r, ar * hi + ai * hr + inc_i

            hr, hi = lax.fori_loop(0, SCAN_CHUNKS, step, (hre_ref[d, :, cols], him_ref[d, :, cols]),
                                   unroll=4)
            hre_ref[d, :, cols] = hr
            him_ref[d, :, cols] = hi

        for q in range(SSM_BLOCKS):
            cols = slice(q * bs, (q + 1) * bs)
            y = (_dot(sre_ref[d, :, cols].astype(BF16), ore_ref[d, q])
                 - _dot(sim_ref[d, :, cols].astype(BF16), oim_ref[d, q]))
            put(y_ref, q, y, False)


def _s5_scan(u_tm, f_re, f_im, o_re, o_im, toe, a_re, a_im, seq):
    rows_total = u_tm.shape[1]
    bsz = rows_total // seq
    assert bsz == SUBLANES and SCAN_T % SCAN_LC == 0 and CTX_LEN % SCAN_T == 0
    nt = seq // SCAN_T
    nc = CTX_LEN // SCAN_T
    chunked = (SSM_BLOCKS, seq // SCAN_LC, SCAN_LC, bsz, LANES)
    block = (SSM_BLOCKS, SCAN_CHUNKS, SCAN_LC, bsz, LANES)

    def bwd_tile(i):
        return jnp.where(i < nc, nc - 1 - i, nt - 1 - (i - nc))

    fwd_spec = pl.BlockSpec(block, lambda i: (0, i, 0, 0, 0))
    bwd_spec = pl.BlockSpec(block, lambda i: (0, bwd_tile(i), 0, 0, 0))
    full = lambda a: pl.BlockSpec(a.shape, lambda i: (0,) * a.ndim, pipeline_mode=pl.Buffered(1))
    y_sds = jax.ShapeDtypeStruct(chunked, F32)
    u5 = u_tm.reshape(chunked)
    yf, yb = pl.pallas_call(
        _s5_scan_kernel,
        grid=(nt,),
        in_specs=[fwd_spec, bwd_spec, full(f_re), full(f_im), full(o_re), full(o_im), full(toe),
                  full(a_re), full(a_im)],
        out_specs=[fwd_spec, bwd_spec],
        out_shape=[y_sds, y_sds],
        scratch_shapes=[pltpu.VMEM((2, SCAN_CHUNKS * bsz, SSM_STATES), F32),
                        pltpu.VMEM((2, SCAN_CHUNKS * bsz, SSM_STATES), F32),
                        pltpu.VMEM((2, bsz, SSM_STATES), F32),
                        pltpu.VMEM((2, bsz, SSM_STATES), F32)],
        compiler_params=_cparams("arbitrary"),
        name="s5_scan",
    )(u5, u5, f_re, f_im, o_re, o_im, toe, a_re, a_im)
    return yf.reshape(u_tm.shape), yb.reshape(u_tm.shape)


def _attention_kernel(qt_ref, k_ref, vt_ref, bias_ref, o_ref, s_even_ref, s_odd_ref, *, grid_rows):
    i = pl.program_id(1)
    hd = NA_HEAD_DIM
    qt = (qt_ref[...].astype(F32) * (hd ** -0.5 * LOG2_E)).astype(BF16)
    zeros = jnp.zeros((hd, TM), BF16)
    ones = jnp.ones((BF16_ROWS, NA_KEY_BLOCK), BF16)
    ctx_rows = slice(0, CTX_LEN)

    def pair_cols(h):
        return slice((h // 2) * LANES, (h // 2 + 1) * LANES)

    def padded_qt(h):
        qh = qt[h * hd:(h + 1) * hd]
        return jnp.concatenate([qh, zeros] if h % 2 == 0 else [zeros, qh], axis=0)

    @pl.when(i == 0)
    def _():
        for h in range(NA_HEADS):
            s = _dot(k_ref[ctx_rows, pair_cols(h)], padded_qt(h))
            p = jnp.exp2(s - jnp.max(s, axis=0, keepdims=True)).astype(BF16)
            ol = _dot(jnp.concatenate([vt_ref[0, h * hd:(h + 1) * hd, :], ones], axis=0), p)
            o_ref[h * hd:(h + 1) * hd, :] = (ol[0:hd] / ol[hd:hd + 1]).astype(BF16)

    @pl.when(i > 0)
    def _():
        r_a = (i - 1) * NA_QROWS
        k_start = jnp.clip(r_a - WIN_H // 2, 0, grid_rows - NA_KROWS)
        start = pl.multiple_of(CTX_LEN + k_start * GRID_W, NA_KEY_BLOCK)
        kb = NA_KEY_BLOCK
        blk0 = start // kb
        n_loc = NA_KROWS * GRID_W // kb
        n_all = n_loc + 1
        k_rows = [pl.ds(start + j * kb, kb) for j in range(n_loc)] + [ctx_rows]
        v_blocks = [blk0 + j for j in range(n_loc)] + [0]
        rhs = [padded_qt(h) for h in range(NA_HEADS)]
        s_bufs = (s_even_ref, s_odd_ref)

        def scores(h, j):
            s = _dot(k_ref[k_rows[j], pair_cols(h)], rhs[h])
            if j < n_loc:
                s = s + bias_ref[h, j * kb:(j + 1) * kb, :]
            s_bufs[h % 2][j * kb:(j + 1) * kb] = s
            return jnp.max(s, axis=0, keepdims=True)

        m = functools.reduce(jnp.maximum, [scores(0, j) for j in range(n_all)])
        for h in range(NA_HEADS):
            ol = None
            next_max = []
            for j in range(n_all):
                if h + 1 < NA_HEADS:
                    next_max.append(scores(h + 1, j))
                p = jnp.exp2(s_bufs[h % 2][j * kb:(j + 1) * kb] - m).astype(BF16)
                lhs = jnp.concatenate([vt_ref[v_blocks[j], h * hd:(h + 1) * hd, :], ones], axis=0)
                part = _dot(lhs, p)
                ol = part if ol is None else ol + part
            o_ref[h * hd:(h + 1) * hd, :] = (ol[0:hd] / ol[hd:hd + 1]).astype(BF16)
            if next_max:
                m = functools.reduce(jnp.maximum, next_max)


def _na_tile_classes(grid_rows):
    return ((0, 0), (NA_QROWS, 0), (grid_rows - NA_QROWS, grid_rows - NA_KROWS))


def _na_valid(grid_rows):
    kr = np.arange(NA_KROWS)[:, None, None, None]
    kc = np.arange(GRID_W)[None, :, None, None]
    rq = np.arange(NA_QROWS)[None, None, :, None]
    cq = np.arange(GRID_W)[None, None, None, :]
    oks = []
    for r_a, k_start in _na_tile_classes(grid_rows):
        r0 = np.clip(r_a + rq - WIN_H // 2, 0, grid_rows - WIN_H)
        krow = k_start + kr
        cs = np.clip(cq - WIN_W // 2, 0, GRID_W - WIN_W)
        ok = (krow >= r0) & (krow < r0 + WIN_H) & (kc >= cs) & (kc < cs + WIN_W)
        oks.append(ok.reshape(NA_KROWS * GRID_W, TM))
    return np.stack(oks)


NA_DR_MARGIN = NA_QROWS
NA_DR_ROWS = 2 * WIN_H - 1 + 2 * NA_DR_MARGIN + 1


def _na_bias_kernel(w_ref, mask_ref, out_ref, *, grid_rows):
    lo_half = lax.broadcasted_iota(jnp.int32, (GRID_W, LANES), 1) < GRID_W
    cache = {}

    def toeplitz(d, upper):
        if (d, upper) not in cache:
            row = jnp.broadcast_to(w_ref[d:d + 1, :], (GRID_W, LANES))
            shift = GRID_W + 1 + (GRID_W if upper else 0)
            cache[d, upper] = pltpu.roll(row, shift, 1, stride=1, stride_axis=0)
        return cache[d, upper]

    for cls, (r_a, k_start) in enumerate(_na_tile_classes(grid_rows)):
        for kr in range(NA_KROWS):
            for pair in range(NA_QROWS // 2):
                d = k_start + kr - (r_a + 2 * pair) + (WIN_H - 1) + NA_DR_MARGIN
                block = jnp.where(lo_half, toeplitz(d, False), toeplitz(d - 1, True))
                rows = slice(kr * GRID_W, (kr + 1) * GRID_W)
                cols = slice(pair * LANES, (pair + 1) * LANES)
                out_ref[cls, rows, cols] = block + mask_ref[cls, rows, cols]


def _na_bias_tables(rpb, grid_rows):
    depth, heads, n_dr, n_dc = rpb.shape
    n_keys = NA_KROWS * GRID_W
    first_lane = (GRID_W - 1) - (WIN_W - 1)
    w = jnp.pad(jnp.flip(rpb * LOG2_E, axis=-1),
                ((0, 0), (0, 0), (NA_DR_MARGIN, NA_DR_ROWS - n_dr - NA_DR_MARGIN),
                 (first_lane, LANES - first_lane - n_dc)))
    mask = jnp.asarray(np.where(_na_valid(grid_rows), 0.0, NEG_INF).astype(np.float32))
    return pl.pallas_call(
        functools.partial(_na_bias_kernel, grid_rows=grid_rows),
        grid=(depth, heads),
        in_specs=[pl.BlockSpec((None, None, NA_DR_ROWS, LANES), lambda l, h: (l, h, 0, 0)),
                  pl.BlockSpec((3, n_keys, TM), lambda l, h: (0, 0, 0))],
        out_specs=pl.BlockSpec((None, 3, None, n_keys, TM), lambda l, h: (l, 0, h, 0, 0)),
        out_shape=jax.ShapeDtypeStruct((depth, 3, heads, n_keys, TM), F32),
        compiler_params=pltpu.CompilerParams(dimension_semantics=("parallel", "parallel")),
        name="na_bias",
    )(w, mask)


def _attention(qt, k, vt, bias, grid_rows):
    bsz, s, _ = k.shape
    nt = s // TM
    n_keys = NA_KROWS * GRID_W

    def bias_class(b, i):
        return (jnp.where(i <= 1, 0, jnp.where(i == nt - 1, 2, 1)), 0, 0, 0)

    return pl.pallas_call(
        functools.partial(_attention_kernel, grid_rows=grid_rows),
        grid=(bsz, nt),
        in_specs=[_fm_spec(),
                  pl.BlockSpec((None, s, NA_W), lambda b, i: (b, 0, 0)),
                  pl.BlockSpec((None, nt, NA_W, TM), lambda b, i: (b, 0, 0, 0)),
                  pl.BlockSpec((None, NA_HEADS, n_keys, TM), bias_class)],
        out_specs=_fm_spec(),
        out_shape=jax.ShapeDtypeStruct((bsz, nt, NA_W, TM), BF16),
        scratch_shapes=[pltpu.VMEM((n_keys + CTX_LEN, TM), F32),
                        pltpu.VMEM((n_keys + CTX_LEN, TM), F32)],
        compiler_params=_cparams("parallel", "arbitrary"),
        name="attention",
    )(qt, k, vt, bias)


def _gelu_tanh(x):
    return 0.5 * x * (1.0 + jnp.tanh(math.sqrt(2.0 / math.pi) * (x + 0.044715 * (x * x * x))))


def _merge_kernel(*refs, n_tiles, first):
    c_ref = refs[1] if first else None
    x_ref = refs[0]
    (cg_ref, cprev_ref, cnext_ref, u_ref, yf_ref, yb_ref, o_ref, gate_ref, g1_ref, cw_ref, cout_ref,
     d_ref, glua_ref, glub_ref, naout_ref, wout_ref, out_ref) = refs[2:] if first else refs[1:]
    i = pl.program_id(0)
    rows = _batch_rows(pl.program_id(1))

    def slab(ref):
        return jnp.concatenate([ref[q, rows, :] for q in range(SSM_BLOCKS)], axis=1)

    def gated_input(ref):
        return ref[:, 2 * CONV_W:3 * CONV_W].astype(F32) * ref[:, 0:CONV_W].astype(F32)

    z = gated_input(cg_ref)
    has_prev = (i >= 2).astype(F32)
    has_next = jnp.logical_and(i != 0, i != n_tiles - 1).astype(F32)
    z_before = gated_input(cprev_ref)[BF16_ROWS - 1:BF16_ROWS] * has_prev
    z_after = gated_input(cnext_ref)[0:1] * has_next
    row = lax.broadcasted_iota(jnp.int32, z.shape, 0)
    z_prev = jnp.where(row == 0, z_before, pltpu.roll(z, 1, 0))
    z_next = jnp.where(row == TM - 1, z_after, pltpu.roll(z, TM - 1, 0))
    conv = z_prev * cw_ref[0:1, :] + z * cw_ref[1:2, :] + z_next * cw_ref[2:3, :]
    a_pre = cg_ref[:, CONV_W:2 * CONV_W].astype(F32) * conv
    ya = _dot(a_pre.astype(BF16), cout_ref[...])

    y = slab(yf_ref) + slab(yb_ref) + d_ref[...] * slab(u_ref)
    g = _gelu_tanh(y).astype(BF16)
    def twice_logistic(half_x):
        return jnp.tanh(half_x) + 1.0

    yb = _dot(g, glua_ref[...]) * twice_logistic(_dot(g, glub_ref[...]))

    yc = lax.dot_general(o_ref[...], naout_ref[...], (((0,), (0,)), ((), ())),
                         preferred_element_type=F32)

    mix2 = (twice_logistic(gate_ref[:, 0:D_MODEL].astype(F32)) * ya
            + twice_logistic(gate_ref[:, D_MODEL:2 * D_MODEL].astype(F32)) * yb
            + twice_logistic(gate_ref[:, 2 * D_MODEL:3 * D_MODEL].astype(F32)) * yc)
    out_ref[...] = _stream_tile(x_ref, c_ref) + g1_ref[...] * _dot(mix2.astype(BF16), wout_ref[...])


def _merge(stream, conv_g, u_tm, yf, yb, attn_o, gates, gate1, conv_w, conv_out, s5_d, glu_a, glu_b,
           na_out, w_out):
    first = len(stream) == 2
    bsz, s, _ = conv_g.shape
    nt = s // TM
    halo_blocks = s // BF16_ROWS
    per_tile = TM // BF16_ROWS
    tok = lambda width: _tok_spec(width, batch_major=False)
    tm_spec = _slab_spec()
    prev_spec = pl.BlockSpec((None, BF16_ROWS, 3 * CONV_W),
                             lambda i, b: (b, jnp.maximum(i * per_tile - 1, 0), 0))
    next_spec = pl.BlockSpec((None, BF16_ROWS, 3 * CONV_W),
                             lambda i, b: (b, jnp.minimum((i + 1) * per_tile, halo_blocks - 1), 0))
    return pl.pallas_call(
        functools.partial(_merge_kernel, n_tiles=nt, first=first),
        grid=(nt, bsz),
        in_specs=_stream_specs(first) + [
            tok(3 * CONV_W), prev_spec, next_spec, tm_spec, tm_spec, tm_spec,
            _fm_spec(batch_major=False), tok(3 * D_MODEL), _mod_spec(batch_major=False),
            _const_spec((3, CONV_W)), _const_spec((CONV_W, D_MODEL)), _const_spec((1, SSM_W)),
            _const_spec((SSM_W, D_MODEL)), _const_spec((SSM_W, D_MODEL)),
            _const_spec((NA_W, D_MODEL)), _const_spec((D_MODEL, D_MODEL))],
        out_specs=tok(D_MODEL),
        out_shape=jax.ShapeDtypeStruct((bsz, s, D_MODEL), F32),
        compiler_params=_cparams("parallel", "arbitrary"),
        name="merge",
    )(*stream, conv_g, conv_g, conv_g, u_tm, yf, yb, attn_o, gates, gate1, conv_w, conv_out, s5_d,
      glu_a, glu_b, na_out, w_out)


def _mlp_kernel(x_ref, sh_ref, sc_ref, gt_ref, g_ref, w1_ref, w2_ref, fg_ref, out_ref, *, final):
    x = x_ref[...]
    h = (_rms(x, g_ref[...]) * (1.0 + sc_ref[...]) + sh_ref[...]).astype(BF16)
    acc = jnp.zeros_like(x)
    for c in range(MLP_HIDDEN // D_MODEL):
        cols = slice(c * D_MODEL, (c + 1) * D_MODEL)
        a = jnp.maximum(_dot(h, w1_ref[:, cols]), 0.0)
        acc = acc + _dot((a * a).astype(BF16), w2_ref[cols, :])
    y = x + gt_ref[...] * acc
    if final:
        y = _rms(y, fg_ref[...])
    out_ref[...] = y


def _mlp(xc, shift, scale, gate, g, w1, w2, final_g, final):
    bsz, s, _ = xc.shape
    first = CTX_LEN // TM if final else 0
    nt = s // TM - first
    tok_in = pl.BlockSpec((None, TM, D_MODEL), lambda b, i: (b, i + first, 0))
    mod = pl.BlockSpec((None, None, 1, D_MODEL), lambda b, i: (b, jnp.minimum(i + first, 1), 0, 0))
    return pl.pallas_call(
        functools.partial(_mlp_kernel, final=final),
        grid=(bsz, nt),
        in_specs=[tok_in, mod, mod, mod, _const_spec((1, D_MODEL)),
                  _const_spec((D_MODEL, MLP_HIDDEN)), _const_spec((MLP_HIDDEN, D_MODEL)),
                  _const_spec((1, D_MODEL))],
        out_specs=_tok_spec(D_MODEL),
        out_shape=jax.ShapeDtypeStruct((bsz, nt * TM, D_MODEL), F32),
        compiler_params=_cparams("parallel", "parallel"),
        name="mlp",
    )(xc, shift, scale, gate, g, w1, w2, final_g)


def kernel(x, c, ctx, c_ctx, w_mod, b_mod, norm1_g, w_in, conv_w, conv_out, s5_lam_re, s5_lam_im,
           s5_log_step, s5_b_re, s5_b_im, s5_c_re, s5_c_im, s5_d, s5_glu_a, s5_glu_b, na_rpb,
           na_out, w_out, norm2_g, mlp_w1, mlp_w2, final_norm_g):
    bsz, seq, _ = x.shape
    depth = w_mod.shape[0]
    assert ctx.shape[1] == CTX_LEN == TM and seq % TM == 0 and bsz == SUBLANES
    grid_rows = seq // GRID_W
    assert grid_rows >= NA_KROWS and grid_rows % NA_QROWS == 0
    s = CTX_LEN + seq

    cond_rows = 2 * SUBLANES
    cond = jnp.zeros((cond_rows, D_MODEL), F32).at[:bsz].set(c).at[bsz].set(c_ctx)
    mods = _adaln(cond, w_mod, b_mod).reshape(depth, cond_rows, N_MOD, D_MODEL)
    lat = mods[:, :bsz]
    cx = jnp.broadcast_to(mods[:, bsz:bsz + 1], lat.shape)
    mod = jnp.stack([cx, lat], axis=2)[:, :, :, :, None, :]

    s5_mats = _s5_matrices(s5_lam_re, s5_lam_im, s5_log_step, s5_b_re, s5_b_im, s5_c_re, s5_c_im)

    bf = lambda w: w.astype(BF16)
    gate_cols = (jnp.arange(IN_PROJ_W) >= IN_OFF[7])
    w_in_b = bf(w_in * jnp.where(gate_cols, 0.5, 1.0).astype(F32))
    conv_out_b, glu_a_b, glu_b_b = bf(conv_out), bf(s5_glu_a * 0.5), bf(s5_glu_b * 0.5)
    na_out_b, w_out_b, w1_b, w2_b = bf(na_out), bf(w_out * 0.5), bf(mlp_w1), bf(mlp_w2)
    w_qv_t = _qv_rows(w_in)

    na_bias = _na_bias_tables(na_rpb, grid_rows)

    stream = (x, ctx)
    row = lambda v: v.reshape(1, -1)
    for l in range(depth):
        m = mod[l]
        conv_g, u_tm, qt, k, vt, gates = _in_proj(stream, m[:, :, 0], m[:, :, 1], row(norm1_g[l]),
                                                  w_in_b[l], w_qv_t[l])
        yf, yb = _s5_scan(u_tm, *(mat[l] for mat in s5_mats), s)
        attn_o = _attention(qt, k, vt, na_bias[l], grid_rows)
        xc = _merge(stream, conv_g, u_tm, yf, yb, attn_o, gates, m[:, :, 2], conv_w[l],
                    conv_out_b[l], row(s5_d[l]), glu_a_b[l], glu_b_b[l], na_out_b[l], w_out_b[l])
        xc = _mlp(xc, m[:, :, 3], m[:, :, 4], m[:, :, 5], row(norm2_g[l]), w1_b[l], w2_b[l],
                  row(final_norm_g), final=(l == depth - 1))
        stream = (xc,)
    return xc
```

```python
import functools
import math

import numpy as np
import jax
import jax.numpy as jnp
from jax import lax
from jax.experimental import pallas as pl
from jax.experimental.pallas import tpu as pltpu

D_MODEL = 1024
CTX_LEN = 256
GRID_W = 64
N_MOD = 6
CONV_W = 512
SSM_W = 512
SSM_GROUP = 16
SSM_GROUPS = SSM_W // SSM_GROUP
SSM_STATE = 64
NA_HEADS = 8
NA_HEAD_DIM = 64
NA_W = NA_HEADS * NA_HEAD_DIM
WIN_H = 8
WIN_W = 16
MLP_HIDDEN = 4 * D_MODEL
IN_SIZES = (CONV_W, CONV_W, CONV_W, SSM_W, NA_W, NA_W, NA_W, D_MODEL, D_MODEL, D_MODEL)
IN_OFF = tuple(sum(IN_SIZES[:i]) for i in range(len(IN_SIZES) + 1))
IN_PROJ_W = IN_OFF[-1]
RMS_EPS = 1e-6
NEG_INF = -1e30
S5_MIN_DECAY = 1e-4
LOG2_E = math.log2(math.e)

F32 = jnp.float32
BF16 = jnp.bfloat16

LANES = 128
SUBLANES = 8
BF16_ROWS = 16
TM = 256
SCAN_LC = 4
SCAN_T = 128
SCAN_CHUNKS = SCAN_T // SCAN_LC
NA_QROWS = TM // GRID_W
NA_KROWS = NA_QROWS + WIN_H
NA_KEY_BLOCK = TM
SSM_LANE_GROUPS = LANES // SSM_GROUP
SSM_BLOCKS = SSM_W // LANES
SSM_BLOCK_STATES = SSM_LANE_GROUPS * SSM_STATE
SSM_STATES = SSM_GROUPS * SSM_STATE
VMEM_LIMIT = 56 * 1024 * 1024


def _cparams(*sem):
    return pltpu.CompilerParams(dimension_semantics=sem, vmem_limit_bytes=VMEM_LIMIT)


def _rms(x, g):
    return x * lax.rsqrt(jnp.mean(x * x, axis=-1, keepdims=True) + RMS_EPS) * g


def _dot(a, b):
    return jnp.dot(a, b, preferred_element_type=F32)


def _dot_nt(a, b):
    return lax.dot_general(a, b, (((1,), (1,)), ((), ())), preferred_element_type=F32)


def _adaln_kernel(c_ref, w_ref, b_ref, o_ref):
    c = c_ref[...]
    s = c * jax.nn.sigmoid(c)
    o_ref[...] = jnp.dot(s, w_ref[...], preferred_element_type=F32,
                         precision=lax.Precision.HIGHEST) + b_ref[...]


def _adaln(cond, w_mod, b_mod):
    depth = w_mod.shape[0]
    rows = cond.shape[0]
    n_tiles = (N_MOD * D_MODEL) // D_MODEL
    return pl.pallas_call(
        _adaln_kernel,
        grid=(depth, n_tiles),
        in_specs=[
            pl.BlockSpec((rows, D_MODEL), lambda l, j: (0, 0)),
            pl.BlockSpec((None, D_MODEL, D_MODEL), lambda l, j: (l, 0, j)),
            pl.BlockSpec((None, 1, D_MODEL), lambda l, j: (l, 0, j)),
        ],
        out_specs=pl.BlockSpec((None, rows, D_MODEL), lambda l, j: (l, 0, j)),
        out_shape=jax.ShapeDtypeStruct((depth, rows, N_MOD * D_MODEL), F32),
        compiler_params=_cparams("parallel", "parallel"),
        name="adaln",
    )(cond, w_mod, b_mod.reshape(depth, 1, N_MOD * D_MODEL))


def _s5_prep_kernel(lre_ref, lim_ref, ls_ref, bre_ref, bim_ref, cre_ref, cim_ref,
                    are_ref, aim_ref, fre_ref, fim_ref, ore_ref, oim_ref):
    lr = jnp.minimum(lre_ref[...], -S5_MIN_DECAY)
    li = lim_ref[...]
    dt = jnp.exp(ls_ref[...])
    xr = lr * dt
    xi = li * dt

    def power(k):
        e = jnp.exp(k * xr)
        return e * jnp.cos(k * xi), e * jnp.sin(k * xi)

    ar, ai = power(1.0)
    nr = ar - 1.0
    den = lr * lr + li * li
    cr = (nr * lr + ai * li) / den
    ci = (ai * lr - nr * li) / den
    bre = bre_ref[...]
    bim = bim_ref[...]
    bbr = cr * bre - ci * bim
    bbi = cr * bim + ci * bre
    cre = cre_ref[...]
    cim = cim_ref[...]
    fre_ref[:, 0] = bbr
    fim_ref[:, 0] = bbi
    ore_ref[:, 0] = cre
    oim_ref[:, 0] = cim
    for k in range(1, SCAN_LC + 1):
        pr, pi = power(float(k))
        if k < SCAN_LC:
            fre_ref[:, k] = pr * bbr - pi * bbi
            fim_ref[:, k] = pr * bbi + pi * bbr
        ore_ref[:, k] = pr * cre - pi * cim
        oim_ref[:, k] = pr * cim + pi * cre
    are_ref[...], aim_ref[...] = power(float(SCAN_LC))


def _s5_prep(lam_re, lam_im, log_step, b_re, b_im, c_re, c_im):
    depth = lam_re.shape[0]
    rows = depth * 2 * SSM_GROUPS
    rb = SSM_GROUPS
    lre = lam_re.reshape(rows, 1, SSM_STATE)
    lim = lam_im.reshape(rows, 1, SSM_STATE)
    ls = jnp.broadcast_to(log_step.reshape(rows, 1, 1), (rows, 1, SSM_STATE))
    bre = jnp.swapaxes(b_re.reshape(rows, SSM_STATE, SSM_GROUP), 1, 2)
    bim = jnp.swapaxes(b_im.reshape(rows, SSM_STATE, SSM_GROUP), 1, 2)
    cre = c_re.reshape(rows, SSM_GROUP, SSM_STATE)
    cim = c_im.reshape(rows, SSM_GROUP, SSM_STATE)
    vec = pl.BlockSpec((rb, 1, SSM_STATE), lambda r: (r, 0, 0))
    mat = pl.BlockSpec((rb, SSM_GROUP, SSM_STATE), lambda r: (r, 0, 0))
    pw = lambda n: pl.BlockSpec((rb, n, SSM_GROUP, SSM_STATE), lambda r: (r, 0, 0, 0))
    v2 = jax.ShapeDtypeStruct((rows, 1, SSM_STATE), F32)
    v4 = lambda n: jax.ShapeDtypeStruct((rows, n, SSM_GROUP, SSM_STATE), F32)
    return pl.pallas_call(
        _s5_prep_kernel,
        grid=(rows // rb,),
        in_specs=[vec, vec, vec, mat, mat, mat, mat],
        out_specs=[vec, vec, pw(SCAN_LC), pw(SCAN_LC), pw(SCAN_LC + 1), pw(SCAN_LC + 1)],
        out_shape=[v2, v2, v4(SCAN_LC), v4(SCAN_LC), v4(SCAN_LC + 1), v4(SCAN_LC + 1)],
        compiler_params=pltpu.CompilerParams(dimension_semantics=("parallel",)),
        name="s5_prep",
    )(lre, lim, ls, bre, bim, cre, cim)


def _s5_toe_kernel(bre_ref, bim_ref, ore_ref, oim_ref, k_ref):
    hi = lax.Precision.HIGHEST
    for k in range(SCAN_LC):
        k_ref[k] = (jnp.dot(bre_ref[...], ore_ref[k], preferred_element_type=F32, precision=hi)
                    - jnp.dot(bim_ref[...], oim_ref[k], preferred_element_type=F32, precision=hi))


def _s5_toe(bd_re, bd_im, oc_re, oc_im):
    nb = bd_re.shape[0]
    bspec = pl.BlockSpec((None, LANES, SSM_BLOCK_STATES), lambda r: (r, 0, 0))
    ospec = pl.BlockSpec((None, SCAN_LC, SSM_BLOCK_STATES, LANES), lambda r: (r, 0, 0, 0))
    return pl.pallas_call(
        _s5_toe_kernel,
        grid=(nb,),
        in_specs=[bspec, bspec, ospec, ospec],
        out_specs=pl.BlockSpec((None, SCAN_LC, LANES, LANES), lambda r: (r, 0, 0, 0)),
        out_shape=jax.ShapeDtypeStruct((nb, SCAN_LC, LANES, LANES), F32),
        compiler_params=pltpu.CompilerParams(dimension_semantics=("parallel",)),
        name="s5_toe",
    )(bd_re, bd_im, oc_re, oc_im)


def _block_diag(m):
    ng, a, b = m.shape[-3:]
    rows = m.reshape(m.shape[:-3] + (ng * a, b))
    spread = jnp.asarray(np.tile(np.eye(b, dtype=np.float32), (1, ng)), m.dtype)
    tiled = jnp.dot(rows, spread, precision=lax.Precision.HIGHEST, preferred_element_type=m.dtype)
    on_diagonal = np.kron(np.eye(ng, dtype=bool), np.ones((a, b), dtype=bool))
    return jnp.where(on_diagonal, tiled, jnp.zeros((), m.dtype))


def _s5_matrices(lam_re, lam_im, log_step, b_re, b_im, c_re, c_im):
    depth = lam_re.shape[0]
    lc = SCAN_LC
    a_re, a_im, f_re, f_im, o_re, o_im = _s5_prep(lam_re, lam_im, log_step, b_re, b_im, c_re, c_im)
    lead = (depth, 2, SSM_BLOCKS, SSM_LANE_GROUPS)

    def per_step(x, n):
        return jnp.moveaxis(x.reshape(lead + (n, SSM_GROUP, SSM_STATE)), 4, 3)

    def mirror_bwd(x, axis):
        return jnp.stack([x[:, 0], jnp.flip(x[:, 1], axis=axis - 1)], axis=1)

    def mirror_fwd(x, axis):
        return jnp.stack([jnp.flip(x[:, 0], axis=axis - 1), x[:, 1]], axis=1)

    def fold_mat(f):
        m = _block_diag(mirror_fwd(per_step(f, lc).astype(BF16), 3))
        return m.reshape(depth, 2, SSM_BLOCKS, lc * LANES, SSM_BLOCK_STATES)

    def out_blocks(o, first, dtype):
        steps = per_step(o, lc + 1)[:, :, :, first:first + lc].astype(dtype)
        return _block_diag(jnp.swapaxes(steps, -1, -2))

    def out_mat(o):
        steps = mirror_bwd(per_step(o, lc + 1)[:, :, :, 1:].astype(BF16), 3)
        rows = jnp.transpose(steps, (0, 1, 2, 4, 6, 3, 5))
        rows = rows.reshape(depth, 2, SSM_BLOCKS, SSM_BLOCK_STATES, lc * SSM_GROUP)
        spread = np.kron(np.eye(lc, dtype=np.float32),
                         np.tile(np.eye(SSM_GROUP, dtype=np.float32), (1, SSM_LANE_GROUPS)))
        tiled = jnp.dot(rows, jnp.asarray(spread, BF16), preferred_element_type=BF16)
        row_group = np.arange(SSM_BLOCK_STATES)[:, None] // SSM_STATE
        col_group = (np.arange(lc * LANES)[None, :] % LANES) // SSM_GROUP
        return jnp.where(row_group == col_group, tiled, jnp.zeros((), BF16))

    nb = depth * 2 * SSM_BLOCKS
    bd = lambda f: _block_diag(per_step(f, lc)[:, :, :, 0]).reshape(nb, LANES, SSM_BLOCK_STATES)
    oc = lambda o: out_blocks(o, 0, F32).reshape(nb, lc, SSM_BLOCK_STATES, LANES)
    kk = _s5_toe(bd(f_re), bd(f_im), oc(o_re), oc(o_im))
    kk = kk.reshape(depth, 2, SSM_BLOCKS, lc, LANES, LANES)
    zero = jnp.zeros((depth, SSM_BLOCKS, LANES, LANES), F32)
    toes = []
    for d in range(2):
        rows = []
        for j_in in range(lc):
            lag = [(j_out - j_in) if d == 0 else (j_in - j_out) for j_out in range(lc)]
            rows.append(jnp.concatenate([kk[:, d, :, g] if g >= 0 else zero for g in lag], axis=-1))
        toes.append(jnp.concatenate(rows, axis=-2))
    toe = jnp.stack(toes, axis=1).astype(BF16)

    a_shape = (depth, 2, 1, SSM_STATES)
    a_re = jnp.broadcast_to(a_re.reshape(a_shape), (depth, 2, SUBLANES, SSM_STATES))
    a_im = jnp.broadcast_to(a_im.reshape(a_shape), (depth, 2, SUBLANES, SSM_STATES))
    return fold_mat(f_re), fold_mat(f_im), out_mat(o_re), out_mat(o_im), toe, a_re, a_im


def _batch_rows(b):
    return pl.ds(b, TM, stride=SUBLANES)


def _qv_rows_kernel(w_ref, o_ref):
    o_ref[...] = w_ref[...].T.astype(BF16)


def _qv_rows(w_in):
    depth = w_in.shape[0]
    col_blocks = (IN_OFF[4] // NA_W, IN_OFF[6] // NA_W)
    assert IN_OFF[4] % NA_W == 0 and IN_OFF[6] % NA_W == 0
    return pl.pallas_call(
        _qv_rows_kernel,
        grid=(depth, 2),
        in_specs=[pl.BlockSpec((None, D_MODEL, NA_W),
                               lambda l, j: (l, 0, jnp.where(j == 0, col_blocks[0], col_blocks[1])))],
        out_specs=pl.BlockSpec((None, NA_W, D_MODEL), lambda l, j: (l, j, 0)),
        out_shape=jax.ShapeDtypeStruct((depth, 2 * NA_W, D_MODEL), BF16),
        compiler_params=pltpu.CompilerParams(dimension_semantics=("parallel", "parallel")),
        name="qv_rows",
    )(w_in)


def _stream_tile(x_ref, c_ref):
    if c_ref is None:
        return x_ref[...]
    return jnp.where(pl.program_id(0) < CTX_LEN // TM, c_ref[...], x_ref[...])


def _in_proj_kernel(*refs, first):
    c_ref = refs[1] if first else None
    x_ref = refs[0]
    (sh_ref, sc_ref, g_ref, w_ref, wqv_ref, conv_ref, u_ref, ub_ref, qt_ref, k_ref, vt_ref,
     gate_ref) = refs[2:] if first else refs[1:]
    h = _rms(_stream_tile(x_ref, c_ref), g_ref[...]) * (1.0 + sc_ref[...]) + sh_ref[...]
    hb = h.astype(BF16)
    conv_ref[...] = _dot(hb, w_ref[:, IN_OFF[0]:IN_OFF[3]]).astype(BF16)
    u = _dot(hb, w_ref[:, IN_OFF[3]:IN_OFF[4]])
    rows = _batch_rows(pl.program_id(1))
    for q in range(SSM_BLOCKS):
        u_ref[q, rows, :] = u[:, q * LANES:(q + 1) * LANES]
    ub_ref[...] = u.astype(BF16)
    qt_ref[...] = _dot_nt(wqv_ref[0:NA_W, :], hb).astype(BF16)
    k_ref[...] = _dot(hb, w_ref[:, IN_OFF[5]:IN_OFF[6]]).astype(BF16)
    vt_ref[...] = _dot_nt(wqv_ref[NA_W:2 * NA_W, :], hb).astype(BF16)
    gate_ref[...] = _dot(hb, w_ref[:, IN_OFF[7]:IN_OFF[10]]).astype(BF16)


def _grid_bi(batch_major):
    return (lambda b, i: (b, i)) if batch_major else (lambda i, b: (b, i))


def _tok_spec(width, batch_major=True):
    bi = _grid_bi(batch_major)
    return pl.BlockSpec((None, TM, width), lambda *g: bi(*g) + (0,))


def _mod_spec(batch_major=True):
    bi = _grid_bi(batch_major)

    def index(*g):
        b, i = bi(*g)
        return (b, jnp.minimum(i, 1), 0, 0)

    return pl.BlockSpec((None, None, 1, D_MODEL), index)


def _const_spec(shape):
    nd = len(shape)
    return pl.BlockSpec(shape, lambda *g: (0,) * nd)


def _slab_spec():
    return pl.BlockSpec((SSM_BLOCKS, TM * SUBLANES, LANES), lambda i, b: (0, i, 0))


def _fm_spec(batch_major=True):
    bi = _grid_bi(batch_major)
    return pl.BlockSpec((None, None, NA_W, TM), lambda *g: bi(*g) + (0, 0))


def _stream_specs(first):
    if not first:
        return [_tok_spec(D_MODEL, batch_major=False)]
    nc = CTX_LEN // TM
    return [pl.BlockSpec((None, TM, D_MODEL), lambda i, b: (b, jnp.maximum(i - nc, 0), 0)),
            pl.BlockSpec((None, TM, D_MODEL), lambda i, b: (b, jnp.minimum(i, nc - 1), 0))]


def _in_proj(stream, shift, scale, g, w, w_qv_t):
    first = len(stream) == 2
    bsz = stream[0].shape[0]
    s = stream[0].shape[1] + (CTX_LEN if first else 0)
    nt = s // TM
    tok = lambda width: _tok_spec(width, batch_major=False)
    mod = _mod_spec(batch_major=False)
    fm = _fm_spec(batch_major=False)
    fm_sds = jax.ShapeDtypeStruct((bsz, nt, NA_W, TM), BF16)
    return pl.pallas_call(
        functools.partial(_in_proj_kernel, first=first),
        grid=(nt, bsz),
        in_specs=_stream_specs(first) + [mod, mod, _const_spec((1, D_MODEL)),
                                         _const_spec((D_MODEL, IN_PROJ_W)),
                                         _const_spec((2 * NA_W, D_MODEL))],
        out_specs=[tok(3 * CONV_W), _slab_spec(), tok(SSM_W), fm, tok(NA_W), fm, tok(3 * D_MODEL)],
        out_shape=[jax.ShapeDtypeStruct((bsz, s, 3 * CONV_W), BF16),
                   jax.ShapeDtypeStruct((SSM_BLOCKS, s * bsz, LANES), F32),
                   jax.ShapeDtypeStruct((bsz, s, SSM_W), BF16),
                   fm_sds,
                   jax.ShapeDtypeStruct((bsz, s, NA_W), BF16),
                   fm_sds,
                   jax.ShapeDtypeStruct((bsz, s, 3 * D_MODEL), BF16)],
        compiler_params=_cparams("parallel", "arbitrary"),
        name="in_proj",
    )(*stream, shift, scale, g, w, w_qv_t)


def _s5_scan_kernel(uf_ref, ub_ref, fre_ref, fim_ref, ore_ref, oim_ref, toe_ref, are_ref, aim_ref,
                    yf_ref, yb_ref, sre_ref, sim_ref, hre_ref, him_ref, ytoe_ref, yflat_ref):
    @pl.when(pl.program_id(0) == 0)
    def _():
        hre_ref[...] = jnp.zeros_like(hre_ref)
        him_ref[...] = jnp.zeros_like(him_ref)

    bs = SSM_BLOCK_STATES
    rows_c = SCAN_CHUNKS * SUBLANES

    for d, (u_ref, y_ref) in enumerate(((uf_ref, yf_ref), (ub_ref, yb_ref))):
        for q in range(SSM_BLOCKS):
            uc = jnp.concatenate([u_ref[q, :, j].reshape(rows_c, LANES) for j in range(SCAN_LC)],
                                 axis=1).astype(BF16)
            sre_ref[d, :, q * bs:(q + 1) * bs] = _dot(uc, fre_ref[d, q])
            sim_ref[d, :, q * bs:(q + 1) * bs] = _dot(uc, fim_ref[d, q])
            ytoe_ref[q] = _dot(uc, toe_ref[d, q])

        for q in range(SSM_BLOCKS):
            cols = slice(q * bs, (q + 1) * bs)
            ar = are_ref[d, :, cols]
            ai = aim_ref[d, :, cols]

            def step(c, carry, d=d, cols=cols, ar=ar, ai=ai):
                hr, hi = carry
                cc = c if d == 0 else SCAN_CHUNKS - 1 - c
                rows = pl.ds(pl.multiple_of(cc * SUBLANES, SUBLANES), SUBLANES)
                inc_r = sre_ref[d, rows, cols]
                inc_i = sim_ref[d, rows, cols]
                sre_ref[d, rows, cols] = hr
                sim_ref[d, rows, cols] = hi
                return ar * hr - ai * hi + inc_r, ar * hi + ai * hr + inc_i

            hr, hi = lax.fori_loop(0, SCAN_CHUNKS, step, (hre_ref[d, :, cols], him_ref[d, :, cols]),
                                   unroll=4)
            hre_ref[d, :, cols] = hr
            him_ref[d, :, cols] = hi

        for q in range(SSM_BLOCKS):
            cols = slice(q * bs, (q + 1) * bs)
            y = (ytoe_ref[q] + _dot(sre_ref[d, :, cols].astype(BF16), ore_ref[d, q])
                 - _dot(sim_ref[d, :, cols].astype(BF16), oim_ref[d, q]))
            for c in range(SCAN_CHUNKS):
                for j in range(SCAN_LC):
                    yflat_ref[q, pl.ds((c * SCAN_LC + j) * SUBLANES, SUBLANES), :] = (
                        y[c * SUBLANES:(c + 1) * SUBLANES, j * LANES:(j + 1) * LANES])

        for b in range(SUBLANES):
            for q in range(SSM_BLOCKS):
                y_ref[b, :, q * LANES:(q + 1) * LANES] = (
                    yflat_ref[q, pl.ds(b, SCAN_T, stride=SUBLANES), :].astype(BF16))


def _s5_scan(u_tm, f_re, f_im, o_re, o_im, toe, a_re, a_im, seq):
    rows_total = u_tm.shape[1]
    bsz = rows_total // seq
    assert bsz == SUBLANES and SCAN_T % SCAN_LC == 0 and CTX_LEN % SCAN_T == 0
    nt = seq // SCAN_T
    nc = CTX_LEN // SCAN_T
    chunked = (SSM_BLOCKS, seq // SCAN_LC, SCAN_LC, bsz, LANES)
    block = (SSM_BLOCKS, SCAN_CHUNKS, SCAN_LC, bsz, LANES)

    def bwd_tile(i):
        return jnp.where(i < nc, nc - 1 - i, nt - 1 - (i - nc))

    fwd_spec = pl.BlockSpec(block, lambda i: (0, i, 0, 0, 0))
    bwd_spec = pl.BlockSpec(block, lambda i: (0, bwd_tile(i), 0, 0, 0))
    full = lambda a: pl.BlockSpec(a.shape, lambda i: (0,) * a.ndim, pipeline_mode=pl.Buffered(1))
    y_sds = jax.ShapeDtypeStruct((bsz, seq, SSM_W), BF16)
    y_block = (bsz, SCAN_T, SSM_W)
    u5 = u_tm.reshape(chunked)
    return pl.pallas_call(
        _s5_scan_kernel,
        grid=(nt,),
        in_specs=[fwd_spec, bwd_spec, full(f_re), full(f_im), full(o_re), full(o_im), full(toe),
                  full(a_re), full(a_im)],
        out_specs=[pl.BlockSpec(y_block, lambda i: (0, i, 0)),
                   pl.BlockSpec(y_block, lambda i: (0, bwd_tile(i), 0))],
        out_shape=[y_sds, y_sds],
        scratch_shapes=[pltpu.VMEM((2, SCAN_CHUNKS * bsz, SSM_STATES), F32),
                        pltpu.VMEM((2, SCAN_CHUNKS * bsz, SSM_STATES), F32),
                        pltpu.VMEM((2, bsz, SSM_STATES), F32),
                        pltpu.VMEM((2, bsz, SSM_STATES), F32),
                        pltpu.VMEM((SSM_BLOCKS, SCAN_CHUNKS * bsz, SCAN_LC * LANES), F32),
                        pltpu.VMEM((SSM_BLOCKS, SCAN_T * bsz, LANES), F32)],
        compiler_params=_cparams("arbitrary"),
        name="s5_scan",
    )(u5, u5, f_re, f_im, o_re, o_im, toe, a_re, a_im)


def _attention_kernel(qt_ref, k_ref, vt_ref, bias_ref, o_ref, s_even_ref, s_odd_ref, *, grid_rows):
    i = pl.program_id(1)
    hd = NA_HEAD_DIM
    qt = (qt_ref[...].astype(F32) * (hd ** -0.5 * LOG2_E)).astype(BF16)
    zeros = jnp.zeros((hd, TM), BF16)
    ones = jnp.ones((BF16_ROWS, NA_KEY_BLOCK), BF16)
    ctx_rows = slice(0, CTX_LEN)

    def pair_cols(h):
        return slice((h // 2) * LANES, (h // 2 + 1) * LANES)

    def padded_qt(h):
        qh = qt[h * hd:(h + 1) * hd]
        return jnp.concatenate([qh, zeros] if h % 2 == 0 else [zeros, qh], axis=0)

    @pl.when(i == 0)
    def _():
        for h in range(NA_HEADS):
            s = _dot(k_ref[ctx_rows, pair_cols(h)], padded_qt(h))
            p = jnp.exp2(s - jnp.max(s, axis=0, keepdims=True)).astype(BF16)
            ol = _dot(jnp.concatenate([vt_ref[0, h * hd:(h + 1) * hd, :], ones], axis=0), p)
            o_ref[h * hd:(h + 1) * hd, :] = (ol[0:hd] / ol[hd:hd + 1]).astype(BF16)

    @pl.when(i > 0)
    def _():
        r_a = (i - 1) * NA_QROWS
        k_start = jnp.clip(r_a - WIN_H // 2, 0, grid_rows - NA_KROWS)
        start = pl.multiple_of(CTX_LEN + k_start * GRID_W, NA_KEY_BLOCK)
        kb = NA_KEY_BLOCK
        blk0 = start // kb
        n_loc = NA_KROWS * GRID_W // kb
        n_all = n_loc + 1
        k_rows = [pl.ds(start + j * kb, kb) for j in range(n_loc)] + [ctx_rows]
        v_blocks = [blk0 + j for j in range(n_loc)] + [0]
        rhs = [padded_qt(h) for h in range(NA_HEADS)]
        s_bufs = (s_even_ref, s_odd_ref)

        def scores(h, j):
            s = _dot(k_ref[k_rows[j], pair_cols(h)], rhs[h])
            if j < n_loc:
                s = s + bias_ref[h, j * kb:(j + 1) * kb, :]
            s_bufs[h % 2][j * kb:(j + 1) * kb] = s
            return jnp.max(s, axis=0, keepdims=True)

        m = functools.reduce(jnp.maximum, [scores(0, j) for j in range(n_all)])
        for h in range(NA_HEADS):
            ol = None
            next_max = []
            for j in range(n_all):
                if h + 1 < NA_HEADS:
                    next_max.append(scores(h + 1, j))
                p = jnp.exp2(s_bufs[h % 2][j * kb:(j + 1) * kb] - m).astype(BF16)
                lhs = jnp.concatenate([vt_ref[v_blocks[j], h * hd:(h + 1) * hd, :], ones], axis=0)
                part = _dot(lhs, p)
                ol = part if ol is None else ol + part
            o_ref[h * hd:(h + 1) * hd, :] = (ol[0:hd] / ol[hd:hd + 1]).astype(BF16)
            if next_max:
                m = functools.reduce(jnp.maximum, next_max)


def _na_tile_classes(grid_rows):
    return ((0, 0), (NA_QROWS, 0), (grid_rows - NA_QROWS, grid_rows - NA_KROWS))


def _na_valid(grid_rows):
    kr = np.arange(NA_KROWS)[:, None, None, None]
    kc = np.arange(GRID_W)[None, :, None, None]
    rq = np.arange(NA_QROWS)[None, None, :, None]
    cq = np.arange(GRID_W)[None, None, None, :]
    oks = []
    for r_a, k_start in _na_tile_classes(grid_rows):
        r0 = np.clip(r_a + rq - WIN_H // 2, 0, grid_rows - WIN_H)
        krow = k_start + kr
        cs = np.clip(cq - WIN_W // 2, 0, GRID_W - WIN_W)
        ok = (krow >= r0) & (krow < r0 + WIN_H) & (kc >= cs) & (kc < cs + WIN_W)
        oks.append(ok.reshape(NA_KROWS * GRID_W, TM))
    return np.stack(oks)


NA_DR_MARGIN = NA_QROWS
NA_DR_ROWS = 2 * WIN_H - 1 + 2 * NA_DR_MARGIN + 1


def _na_bias_kernel(w_ref, mask_ref, out_ref, *, grid_rows):
    lo_half = lax.broadcasted_iota(jnp.int32, (GRID_W, LANES), 1) < GRID_W
    cache = {}

    def toeplitz(d, upper):
        if (d, upper) not in cache:
            row = jnp.broadcast_to(w_ref[d:d + 1, :], (GRID_W, LANES))
            shift = GRID_W + 1 + (GRID_W if upper else 0)
            cache[d, upper] = pltpu.roll(row, shift, 1, stride=1, stride_axis=0)
        return cache[d, upper]

    for cls, (r_a, k_start) in enumerate(_na_tile_classes(grid_rows)):
        for kr in range(NA_KROWS):
            for pair in range(NA_QROWS // 2):
                d = k_start + kr - (r_a + 2 * pair) + (WIN_H - 1) + NA_DR_MARGIN
                block = jnp.where(lo_half, toeplitz(d, False), toeplitz(d - 1, True))
                rows = slice(kr * GRID_W, (kr + 1) * GRID_W)
                cols = slice(pair * LANES, (pair + 1) * LANES)
                out_ref[cls, rows, cols] = block + mask_ref[cls, rows, cols]


def _na_bias_tables(rpb, grid_rows):
    depth, heads, n_dr, n_dc = rpb.shape
    n_keys = NA_KROWS * GRID_W
    first_lane = (GRID_W - 1) - (WIN_W - 1)
    w = jnp.pad(jnp.flip(rpb * LOG2_E, axis=-1),
                ((0, 0), (0, 0), (NA_DR_MARGIN, NA_DR_ROWS - n_dr - NA_DR_MARGIN),
                 (first_lane, LANES - first_lane - n_dc)))
    mask = jnp.asarray(np.where(_na_valid(grid_rows), 0.0, NEG_INF).astype(np.float32))
    return pl.pallas_call(
        functools.partial(_na_bias_kernel, grid_rows=grid_rows),
        grid=(depth, heads),
        in_specs=[pl.BlockSpec((None, None, NA_DR_ROWS, LANES), lambda l, h: (l, h, 0, 0)),
                  pl.BlockSpec((3, n_keys, TM), lambda l, h: (0, 0, 0))],
        out_specs=pl.BlockSpec((None, 3, None, n_keys, TM), lambda l, h: (l, 0, h, 0, 0)),
        out_shape=jax.ShapeDtypeStruct((depth, 3, heads, n_keys, TM), F32),
        compiler_params=pltpu.CompilerParams(dimension_semantics=("parallel", "parallel")),
        name="na_bias",
    )(w, mask)


def _attention(qt, k, vt, bias, grid_rows):
    bsz, s, _ = k.shape
    nt = s // TM
    n_keys = NA_KROWS * GRID_W

    def bias_class(b, i):
        return (jnp.where(i <= 1, 0, jnp.where(i == nt - 1, 2, 1)), 0, 0, 0)

    return pl.pallas_call(
        functools.partial(_attention_kernel, grid_rows=grid_rows),
        grid=(bsz, nt),
        in_specs=[_fm_spec(),
                  pl.BlockSpec((None, s, NA_W), lambda b, i: (b, 0, 0)),
                  pl.BlockSpec((None, nt, NA_W, TM), lambda b, i: (b, 0, 0, 0)),
                  pl.BlockSpec((None, NA_HEADS, n_keys, TM), bias_class)],
        out_specs=_fm_spec(),
        out_shape=jax.ShapeDtypeStruct((bsz, nt, NA_W, TM), BF16),
        scratch_shapes=[pltpu.VMEM((n_keys + CTX_LEN, TM), F32),
                        pltpu.VMEM((n_keys + CTX_LEN, TM), F32)],
        compiler_params=_cparams("parallel", "arbitrary"),
        name="attention",
    )(qt, k, vt, bias)


def _gelu_tanh(x):
    return 0.5 * x * (1.0 + jnp.tanh(math.sqrt(2.0 / math.pi) * (x + 0.044715 * (x * x * x))))


def _merge_kernel(*refs, n_tiles, first):
    c_ref = refs[1] if first else None
    x_ref = refs[0]
    (cg_ref, cprev_ref, cnext_ref, u_ref, yf_ref, yb_ref, o_ref, gate_ref, g1_ref, cw_ref, cout_ref,
     d_ref, glua_ref, glub_ref, naout_ref, wout_ref, out_ref) = refs[2:] if first else refs[1:]
    i = pl.program_id(0)

    def gated_input(ref):
        return ref[:, 2 * CONV_W:3 * CONV_W].astype(F32) * ref[:, 0:CONV_W].astype(F32)

    z = gated_input(cg_ref)
    has_prev = (i >= 2).astype(F32)
    has_next = jnp.logical_and(i != 0, i != n_tiles - 1).astype(F32)
    z_before = gated_input(cprev_ref)[BF16_ROWS - 1:BF16_ROWS] * has_prev
    z_after = gated_input(cnext_ref)[0:1] * has_next
    row = lax.broadcasted_iota(jnp.int32, z.shape, 0)
    z_prev = jnp.where(row == 0, z_before, pltpu.roll(z, 1, 0))
    z_next = jnp.where(row == TM - 1, z_after, pltpu.roll(z, TM - 1, 0))
    conv = z_prev * cw_ref[0:1, :] + z * cw_ref[1:2, :] + z_next * cw_ref[2:3, :]
    a_pre = cg_ref[:, CONV_W:2 * CONV_W].astype(F32) * conv
    ya = _dot(a_pre.astype(BF16), cout_ref[...])

    y = (yf_ref[...].astype(F32) + yb_ref[...].astype(F32)
         + d_ref[...] * u_ref[...].astype(F32))
    g = _gelu_tanh(y).astype(BF16)
    def twice_logistic(half_x):
        return jnp.tanh(half_x) + 1.0

    yb = _dot(g, glua_ref[...]) * twice_logistic(_dot(g, glub_ref[...]))

    yc = lax.dot_general(o_ref[...], naout_ref[...], (((0,), (0,)), ((), ())),
                         preferred_element_type=F32)

    mix2 = (twice_logistic(gate_ref[:, 0:D_MODEL].astype(F32)) * ya
            + twice_logistic(gate_ref[:, D_MODEL:2 * D_MODEL].astype(F32)) * yb
            + twice_logistic(gate_ref[:, 2 * D_MODEL:3 * D_MODEL].astype(F32)) * yc)
    out_ref[...] = _stream_tile(x_ref, c_ref) + g1_ref[...] * _dot(mix2.astype(BF16), wout_ref[...])


def _merge(stream, conv_g, u_tm, yf, yb, attn_o, gates, gate1, conv_w, conv_out, s5_d, glu_a, glu_b,
           na_out, w_out):
    first = len(stream) == 2
    bsz, s, _ = conv_g.shape
    nt = s // TM
    halo_blocks = s // BF16_ROWS
    per_tile = TM // BF16_ROWS
    tok = lambda width: _tok_spec(width, batch_major=False)
    tm_spec = tok(SSM_W)
    prev_spec = pl.BlockSpec((None, BF16_ROWS, 3 * CONV_W),
                             lambda i, b: (b, jnp.maximum(i * per_tile - 1, 0), 0))
    next_spec = pl.BlockSpec((None, BF16_ROWS, 3 * CONV_W),
                             lambda i, b: (b, jnp.minimum((i + 1) * per_tile, halo_blocks - 1), 0))
    return pl.pallas_call(
        functools.partial(_merge_kernel, n_tiles=nt, first=first),
        grid=(nt, bsz),
        in_specs=_stream_specs(first) + [
            tok(3 * CONV_W), prev_spec, next_spec, tm_spec, tm_spec, tm_spec,
            _fm_spec(batch_major=False), tok(3 * D_MODEL), _mod_spec(batch_major=False),
            _const_spec((3, CONV_W)), _const_spec((CONV_W, D_MODEL)), _const_spec((1, SSM_W)),
            _const_spec((SSM_W, D_MODEL)), _const_spec((SSM_W, D_MODEL)),
            _const_spec((NA_W, D_MODEL)), _const_spec((D_MODEL, D_MODEL))],
        out_specs=tok(D_MODEL),
        out_shape=jax.ShapeDtypeStruct((bsz, s, D_MODEL), F32),
        compiler_params=_cparams("parallel", "arbitrary"),
        name="merge",
    )(*stream, conv_g, conv_g, conv_g, u_tm, yf, yb, attn_o, gates, gate1, conv_w, conv_out, s5_d,
      glu_a, glu_b, na_out, w_out)


def _mlp_kernel(x_ref, sh_ref, sc_ref, gt_ref, g_ref, w1_ref, w2_ref, fg_ref, out_ref, *, final):
    x = x_ref[...]
    h = (_rms(x, g_ref[...]) * (1.0 + sc_ref[...]) + sh_ref[...]).astype(BF16)
    acc = jnp.zeros_like(x)
    for c in range(MLP_HIDDEN // D_MODEL):
        cols = slice(c * D_MODEL, (c + 1) * D_MODEL)
        a = jnp.maximum(_dot(h, w1_ref[:, cols]), 0.0)
        acc = acc + _dot((a * a).astype(BF16), w2_ref[cols, :])
    y = x + gt_ref[...] * acc
    if final:
        y = _rms(y, fg_ref[...])
    out_ref[...] = y


def _mlp(xc, shift, scale, gate, g, w1, w2, final_g, final):
    bsz, s, _ = xc.shape
    first = CTX_LEN // TM if final else 0
    nt = s // TM - first
    tok_in = pl.BlockSpec((None, TM, D_MODEL), lambda b, i: (b, i + first, 0))
    mod = pl.BlockSpec((None, None, 1, D_MODEL), lambda b, i: (b, jnp.minimum(i + first, 1), 0, 0))
    return pl.pallas_call(
        functools.partial(_mlp_kernel, final=final),
        grid=(bsz, nt),
        in_specs=[tok_in, mod, mod, mod, _const_spec((1, D_MODEL)),
                  _const_spec((D_MODEL, MLP_HIDDEN)), _const_spec((MLP_HIDDEN, D_MODEL)),
                  _const_spec((1, D_MODEL))],
        out_specs=_tok_spec(D_MODEL),
        out_shape=jax.ShapeDtypeStruct((bsz, nt * TM, D_MODEL), F32),
        compiler_params=_cparams("parallel", "parallel"),
        name="mlp",
    )(xc, shift, scale, gate, g, w1, w2, final_g)


def kernel(x, c, ctx, c_ctx, w_mod, b_mod, norm1_g, w_in, conv_w, conv_out, s5_lam_re, s5_lam_im,
           s5_log_step, s5_b_re, s5_b_im, s5_c_re, s5_c_im, s5_d, s5_glu_a, s5_glu_b, na_rpb,
           na_out, w_out, norm2_g, mlp_w1, mlp_w2, final_norm_g):
    bsz, seq, _ = x.shape
    depth = w_mod.shape[0]
    assert ctx.shape[1] == CTX_LEN == TM and seq % TM == 0 and bsz == SUBLANES
    grid_rows = seq // GRID_W
    assert grid_rows >= NA_KROWS and grid_rows % NA_QROWS == 0
    s = CTX_LEN + seq

    cond_rows = 2 * SUBLANES
    cond = jnp.zeros((cond_rows, D_MODEL), F32).at[:bsz].set(c).at[bsz].set(c_ctx)
    mods = _adaln(cond, w_mod, b_mod).reshape(depth, cond_rows, N_MOD, D_MODEL)
    lat = mods[:, :bsz]
    cx = jnp.broadcast_to(mods[:, bsz:bsz + 1], lat.shape)
    mod = jnp.stack([cx, lat], axis=2)[:, :, :, :, None, :]

    s5_mats = _s5_matrices(s5_lam_re, s5_lam_im, s5_log_step, s5_b_re, s5_b_im, s5_c_re, s5_c_im)

    bf = lambda w: w.astype(BF16)
    gate_cols = (jnp.arange(IN_PROJ_W) >= IN_OFF[7])
    w_in_b = bf(w_in * jnp.where(gate_cols, 0.5, 1.0).astype(F32))
    conv_out_b, glu_a_b, glu_b_b = bf(conv_out), bf(s5_glu_a * 0.5), bf(s5_glu_b * 0.5)
    na_out_b, w_out_b, w1_b, w2_b = bf(na_out), bf(w_out * 0.5), bf(mlp_w1), bf(mlp_w2)
    w_qv_t = _qv_rows(w_in)

    na_bias = _na_bias_tables(na_rpb, grid_rows)

    stream = (x, ctx)
    row = lambda v: v.reshape(1, -1)
    for l in range(depth):
        m = mod[l]
        conv_g, u_tm, u_bm, qt, k, vt, gates = _in_proj(stream, m[:, :, 0], m[:, :, 1],
                                                        row(norm1_g[l]), w_in_b[l], w_qv_t[l])
        yf, yb = _s5_scan(u_tm, *(mat[l] for mat in s5_mats), s)
        attn_o = _attention(qt, k, vt, na_bias[l], grid_rows)
        xc = _merge(stream, conv_g, u_bm, yf, yb, attn_o, gates, m[:, :, 2], conv_w[l],
                    conv_out_b[l], row(s5_d[l]), glu_a_b[l], glu_b_b[l], na_out_b[l], w_out_b[l])
        xc = _mlp(xc, m[:, :, 3], m[:, :, 4], m[:, :, 5], row(norm2_g[l]), w1_b[l], w2_b[l],
                  row(final_norm_g), final=(l == depth - 1))
        stream = (xc,)
    return xc
```

```python
import functools
import math

import numpy as np
import jax
import jax.numpy as jnp
from jax import lax
from jax.experimental import pallas as pl
from jax.experimental.pallas import tpu as pltpu

D_MODEL = 1024
CTX_LEN = 256
GRID_W = 64
N_MOD = 6
CONV_W = 512
SSM_W = 512
SSM_GROUP = 16
SSM_GROUPS = SSM_W // SSM_GROUP
SSM_STATE = 64
NA_HEADS = 8
NA_HEAD_DIM = 64
NA_W = NA_HEADS * NA_HEAD_DIM
WIN_H = 8
WIN_W = 16
MLP_HIDDEN = 4 * D_MODEL
IN_SIZES = (CONV_W, CONV_W, CONV_W, SSM_W, NA_W, NA_W, NA_W, D_MODEL, D_MODEL, D_MODEL)
IN_OFF = tuple(sum(IN_SIZES[:i]) for i in range(len(IN_SIZES) + 1))
IN_PROJ_W = IN_OFF[-1]
RMS_EPS = 1e-6
NEG_INF = -1e30
S5_MIN_DECAY = 1e-4
LOG2_E = math.log2(math.e)

F32 = jnp.float32
BF16 = jnp.bfloat16

LANES = 128
SUBLANES = 8
BF16_ROWS = 16
TM = 256
SCAN_LC = 4
SCAN_T = 128
SCAN_CHUNKS = SCAN_T // SCAN_LC
NA_QROWS = TM // GRID_W
NA_KROWS = NA_QROWS + WIN_H
NA_KEY_BLOCK = TM
SSM_LANE_GROUPS = LANES // SSM_GROUP
SSM_BLOCKS = SSM_W // LANES
SSM_BLOCK_STATES = SSM_LANE_GROUPS * SSM_STATE
SSM_STATES = SSM_GROUPS * SSM_STATE
VMEM_LIMIT = 56 * 1024 * 1024


def _cparams(*sem):
    return pltpu.CompilerParams(dimension_semantics=sem, vmem_limit_bytes=VMEM_LIMIT)


def _rms(x, g):
    return x * lax.rsqrt(jnp.mean(x * x, axis=-1, keepdims=True) + RMS_EPS) * g


def _dot(a, b):
    return jnp.dot(a, b, preferred_element_type=F32)


def _dot_nt(a, b):
    return lax.dot_general(a, b, (((1,), (1,)), ((), ())), preferred_element_type=F32)


def _adaln_kernel(c_ref, w_ref, b_ref, o_ref):
    c = c_ref[...]
    s = c * jax.nn.sigmoid(c)
    o_ref[...] = jnp.dot(s, w_ref[...], preferred_element_type=F32,
                         precision=lax.Precision.HIGHEST) + b_ref[...]


def _adaln(cond, w_mod, b_mod):
    depth = w_mod.shape[0]
    rows = cond.shape[0]
    n_tiles = (N_MOD * D_MODEL) // D_MODEL
    return pl.pallas_call(
        _adaln_kernel,
        grid=(depth, n_tiles),
        in_specs=[
            pl.BlockSpec((rows, D_MODEL), lambda l, j: (0, 0)),
            pl.BlockSpec((None, D_MODEL, D_MODEL), lambda l, j: (l, 0, j)),
            pl.BlockSpec((None, 1, D_MODEL), lambda l, j: (l, 0, j)),
        ],
        out_specs=pl.BlockSpec((None, rows, D_MODEL), lambda l, j: (l, 0, j)),
        out_shape=jax.ShapeDtypeStruct((depth, rows, N_MOD * D_MODEL), F32),
        compiler_params=_cparams("parallel", "parallel"),
        name="adaln",
    )(cond, w_mod, b_mod.reshape(depth, 1, N_MOD * D_MODEL))


def _s5_prep_kernel(lre_ref, lim_ref, ls_ref, bre_ref, bim_ref, cre_ref, cim_ref,
                    are_ref, aim_ref, fre_ref, fim_ref, ore_ref, oim_ref):
    lr = jnp.minimum(lre_ref[...], -S5_MIN_DECAY)
    li = lim_ref[...]
    dt = jnp.exp(ls_ref[...])
    xr = lr * dt
    xi = li * dt

    def power(k):
        e = jnp.exp(k * xr)
        return e * jnp.cos(k * xi), e * jnp.sin(k * xi)

    ar, ai = power(1.0)
    nr = ar - 1.0
    den = lr * lr + li * li
    cr = (nr * lr + ai * li) / den
    ci = (ai * lr - nr * li) / den
    bre = bre_ref[...]
    bim = bim_ref[...]
    bbr = cr * bre - ci * bim
    bbi = cr * bim + ci * bre
    cre = cre_ref[...]
    cim = cim_ref[...]
    fre_ref[:, 0] = bbr
    fim_ref[:, 0] = bbi
    ore_ref[:, 0] = cre
    oim_ref[:, 0] = cim
    for k in range(1, SCAN_LC + 1):
        pr, pi = power(float(k))
        if k < SCAN_LC:
            fre_ref[:, k] = pr * bbr - pi * bbi
            fim_ref[:, k] = pr * bbi + pi * bbr
        ore_ref[:, k] = pr * cre - pi * cim
        oim_ref[:, k] = pr * cim + pi * cre
    are_ref[...], aim_ref[...] = power(float(SCAN_LC))


def _s5_prep(lam_re, lam_im, log_step, b_re, b_im, c_re, c_im):
    depth = lam_re.shape[0]
    rows = depth * 2 * SSM_GROUPS
    rb = SSM_GROUPS
    lre = lam_re.reshape(rows, 1, SSM_STATE)
    lim = lam_im.reshape(rows, 1, SSM_STATE)
    ls = jnp.broadcast_to(log_step.reshape(rows, 1, 1), (rows, 1, SSM_STATE))
    bre = jnp.swapaxes(b_re.reshape(rows, SSM_STATE, SSM_GROUP), 1, 2)
    bim = jnp.swapaxes(b_im.reshape(rows, SSM_STATE, SSM_GROUP), 1, 2)
    cre = c_re.reshape(rows, SSM_GROUP, SSM_STATE)
    cim = c_im.reshape(rows, SSM_GROUP, SSM_STATE)
    vec = pl.BlockSpec((rb, 1, SSM_STATE), lambda r: (r, 0, 0))
    mat = pl.BlockSpec((rb, SSM_GROUP, SSM_STATE), lambda r: (r, 0, 0))
    pw = lambda n: pl.BlockSpec((rb, n, SSM_GROUP, SSM_STATE), lambda r: (r, 0, 0, 0))
    v2 = jax.ShapeDtypeStruct((rows, 1, SSM_STATE), F32)
    v4 = lambda n: jax.ShapeDtypeStruct((rows, n, SSM_GROUP, SSM_STATE), F32)
    return pl.pallas_call(
        _s5_prep_kernel,
        grid=(rows // rb,),
        in_specs=[vec, vec, vec, mat, mat, mat, mat],
        out_specs=[vec, vec, pw(SCAN_LC), pw(SCAN_LC), pw(SCAN_LC + 1), pw(SCAN_LC + 1)],
        out_shape=[v2, v2, v4(SCAN_LC), v4(SCAN_LC), v4(SCAN_LC + 1), v4(SCAN_LC + 1)],
        compiler_params=pltpu.CompilerParams(dimension_semantics=("parallel",)),
        name="s5_prep",
    )(lre, lim, ls, bre, bim, cre, cim)


def _s5_operands_kernel(fre_ref, fim_ref, ore_ref, oim_ref, spread_ref, diag_ref,
                        fold_re_ref, fold_im_ref, out_re_ref, out_im_ref, toe_ref):
    lc = SCAN_LC
    hi = lax.Precision.HIGHEST
    forward = pl.program_id(0) % 2 == 0

    def block_diag(x):
        rows = x.reshape(LANES, SSM_STATE)
        tiled = jnp.dot(rows, spread_ref[...], preferred_element_type=F32, precision=hi)
        return tiled * diag_ref[...]

    for j in range(lc):
        for f_ref, fold_ref in ((fre_ref, fold_re_ref), (fim_ref, fold_im_ref)):
            f = jnp.where(forward, f_ref[:, lc - 1 - j], f_ref[:, j])
            fold_ref[j * LANES:(j + 1) * LANES, :] = block_diag(f).astype(BF16)

    oc_re = [block_diag(ore_ref[:, k]).T for k in range(lc + 1)]
    oc_im = [block_diag(oim_ref[:, k]).T for k in range(lc + 1)]
    for j in range(lc):
        cols = slice(j * LANES, (j + 1) * LANES)
        out_re_ref[:, cols] = jnp.where(forward, oc_re[j + 1], oc_re[lc - j]).astype(BF16)
        out_im_ref[:, cols] = jnp.where(forward, oc_im[j + 1], oc_im[lc - j]).astype(BF16)

    bd_re = block_diag(fre_ref[:, 0])
    bd_im = block_diag(fim_ref[:, 0])
    kk = [jnp.dot(bd_re, oc_re[k], preferred_element_type=F32, precision=hi)
          - jnp.dot(bd_im, oc_im[k], preferred_element_type=F32, precision=hi) for k in range(lc)]
    zero = jnp.zeros((LANES, LANES), F32)
    for a in range(lc):
        for b in range(lc):
            fwd = kk[b - a] if b >= a else zero
            bwd = kk[a - b] if a >= b else zero
            toe_ref[a * LANES:(a + 1) * LANES, b * LANES:(b + 1) * LANES] = (
                jnp.where(forward, fwd, bwd).astype(BF16))


def _s5_operands(f_re, f_im, o_re, o_im, depth):
    lc = SCAN_LC
    ng = SSM_LANE_GROUPS
    spread = jnp.asarray(np.tile(np.eye(SSM_STATE, dtype=np.float32), (1, ng)))
    diag = jnp.asarray(np.kron(np.eye(ng, dtype=np.float32),
                               np.ones((SSM_GROUP, SSM_STATE), np.float32)))
    group_block = lambda n: pl.BlockSpec((ng, n, SSM_GROUP, SSM_STATE),
                                         lambda ld, q: (ld * SSM_BLOCKS + q, 0, 0, 0))
    const = lambda a: pl.BlockSpec(a.shape, lambda ld, q: (0, 0))
    out = lambda r, c: pl.BlockSpec((None, None, r, c), lambda ld, q: (ld, q, 0, 0))
    sds = lambda r, c: jax.ShapeDtypeStruct((2 * depth, SSM_BLOCKS, r, c), BF16)
    fold_shape = (lc * LANES, SSM_BLOCK_STATES)
    out_shape = (SSM_BLOCK_STATES, lc * LANES)
    toe_shape = (lc * LANES, lc * LANES)
    results = pl.pallas_call(
        _s5_operands_kernel,
        grid=(2 * depth, SSM_BLOCKS),
        in_specs=[group_block(lc), group_block(lc), group_block(lc + 1), group_block(lc + 1),
                  const(spread), const(diag)],
        out_specs=[out(*fold_shape), out(*fold_shape), out(*out_shape), out(*out_shape),
                   out(*toe_shape)],
        out_shape=[sds(*fold_shape), sds(*fold_shape), sds(*out_shape), sds(*out_shape),
                   sds(*toe_shape)],
        compiler_params=pltpu.CompilerParams(dimension_semantics=("parallel", "parallel")),
        name="s5_operands",
    )(f_re, f_im, o_re, o_im, spread, diag)
    return [r.reshape((depth, 2) + r.shape[1:]) for r in results]


def _s5_matrices(lam_re, lam_im, log_step, b_re, b_im, c_re, c_im):
    depth = lam_re.shape[0]
    a_re, a_im, f_re, f_im, o_re, o_im = _s5_prep(lam_re, lam_im, log_step, b_re, b_im, c_re, c_im)
    fold_re, fold_im, out_re, out_im, toe = _s5_operands(f_re, f_im, o_re, o_im, depth)
    a_shape = (depth, 2, 1, SSM_STATES)
    a_re = jnp.broadcast_to(a_re.reshape(a_shape), (depth, 2, SUBLANES, SSM_STATES))
    a_im = jnp.broadcast_to(a_im.reshape(a_shape), (depth, 2, SUBLANES, SSM_STATES))
    return fold_re, fold_im, out_re, out_im, toe, a_re, a_im


def _batch_rows(b):
    return pl.ds(b, TM, stride=SUBLANES)


def _qv_rows_kernel(w_ref, o_ref):
    o_ref[...] = w_ref[...].T.astype(BF16)


def _qv_rows(w_in):
    depth = w_in.shape[0]
    col_blocks = (IN_OFF[4] // NA_W, IN_OFF[6] // NA_W)
    assert IN_OFF[4] % NA_W == 0 and IN_OFF[6] % NA_W == 0
    return pl.pallas_call(
        _qv_rows_kernel,
        grid=(depth, 2),
        in_specs=[pl.BlockSpec((None, D_MODEL, NA_W),
                               lambda l, j: (l, 0, jnp.where(j == 0, col_blocks[0], col_blocks[1])))],
        out_specs=pl.BlockSpec((None, NA_W, D_MODEL), lambda l, j: (l, j, 0)),
        out_shape=jax.ShapeDtypeStruct((depth, 2 * NA_W, D_MODEL), BF16),
        compiler_params=pltpu.CompilerParams(dimension_semantics=("parallel", "parallel")),
        name="qv_rows",
    )(w_in)


def _stream_tile(x_ref, c_ref):
    if c_ref is None:
        return x_ref[...]
    return jnp.where(pl.program_id(0) < CTX_LEN // TM, c_ref[...], x_ref[...])


def _in_proj_kernel(*refs, first):
    c_ref = refs[1] if first else None
    x_ref = refs[0]
    (sh_ref, sc_ref, g_ref, w_ref, wqv_ref, conv_ref, u_ref, ub_ref, qt_ref, k_ref, vt_ref,
     gate_ref) = refs[2:] if first else refs[1:]
    h = _rms(_stream_tile(x_ref, c_ref), g_ref[...]) * (1.0 + sc_ref[...]) + sh_ref[...]
    hb = h.astype(BF16)
    conv_ref[...] = _dot(hb, w_ref[:, IN_OFF[0]:IN_OFF[3]]).astype(BF16)
    u = _dot(hb, w_ref[:, IN_OFF[3]:IN_OFF[4]])
    rows = _batch_rows(pl.program_id(1))
    for q in range(SSM_BLOCKS):
        u_ref[q, rows, :] = u[:, q * LANES:(q + 1) * LANES]
    ub_ref[...] = u.astype(BF16)
    qt_ref[...] = _dot_nt(wqv_ref[0:NA_W, :], hb).astype(BF16)
    k_ref[...] = _dot(hb, w_ref[:, IN_OFF[5]:IN_OFF[6]]).astype(BF16)
    vt_ref[...] = _dot_nt(wqv_ref[NA_W:2 * NA_W, :], hb).astype(BF16)
    gate_ref[...] = _dot(hb, w_ref[:, IN_OFF[7]:IN_OFF[10]]).astype(BF16)


def _grid_bi(batch_major):
    return (lambda b, i: (b, i)) if batch_major else (lambda i, b: (b, i))


def _tok_spec(width, batch_major=True):
    bi = _grid_bi(batch_major)
    return pl.BlockSpec((None, TM, width), lambda *g: bi(*g) + (0,))


def _mod_spec(batch_major=True):
    bi = _grid_bi(batch_major)

    def index(*g):
        b, i = bi(*g)
        return (b, jnp.minimum(i, 1), 0, 0)

    return pl.BlockSpec((None, None, 1, D_MODEL), index)


def _const_spec(shape):
    nd = len(shape)
    return pl.BlockSpec(shape, lambda *g: (0,) * nd)


def _slab_spec():
    return pl.BlockSpec((SSM_BLOCKS, TM * SUBLANES, LANES), lambda i, b: (0, i, 0))


def _fm_spec(batch_major=True):
    bi = _grid_bi(batch_major)
    return pl.BlockSpec((None, None, NA_W, TM), lambda *g: bi(*g) + (0, 0))


def _stream_specs(first):
    if not first:
        return [_tok_spec(D_MODEL, batch_major=False)]
    nc = CTX_LEN // TM
    return [pl.BlockSpec((None, TM, D_MODEL), lambda i, b: (b, jnp.maximum(i - nc, 0), 0)),
            pl.BlockSpec((None, TM, D_MODEL), lambda i, b: (b, jnp.minimum(i, nc - 1), 0))]


def _in_proj(stream, shift, scale, g, w, w_qv_t):
    first = len(stream) == 2
    bsz = stream[0].shape[0]
    s = stream[0].shape[1] + (CTX_LEN if first else 0)
    nt = s // TM
    tok = lambda width: _tok_spec(width, batch_major=False)
    mod = _mod_spec(batch_major=False)
    fm = _fm_spec(batch_major=False)
    fm_sds = jax.ShapeDtypeStruct((bsz, nt, NA_W, TM), BF16)
    return pl.pallas_call(
        functools.partial(_in_proj_kernel, first=first),
        grid=(nt, bsz),
        in_specs=_stream_specs(first) + [mod, mod, _const_spec((1, D_MODEL)),
                                         _const_spec((D_MODEL, IN_PROJ_W)),
                                         _const_spec((2 * NA_W, D_MODEL))],
        out_specs=[tok(3 * CONV_W), _slab_spec(), tok(SSM_W), fm, tok(NA_W), fm, tok(3 * D_MODEL)],
        out_shape=[jax.ShapeDtypeStruct((bsz, s, 3 * CONV_W), BF16),
                   jax.ShapeDtypeStruct((SSM_BLOCKS, s * bsz, LANES), F32),
                   jax.ShapeDtypeStruct((bsz, s, SSM_W), BF16),
                   fm_sds,
                   jax.ShapeDtypeStruct((bsz, s, NA_W), BF16),
                   fm_sds,
                   jax.ShapeDtypeStruct((bsz, s, 3 * D_MODEL), BF16)],
        compiler_params=_cparams("parallel", "arbitrary"),
        name="in_proj",
    )(*stream, shift, scale, g, w, w_qv_t)


def _s5_scan_kernel(uf_ref, ub_ref, fre_ref, fim_ref, ore_ref, oim_ref, toe_ref, are_ref, aim_ref,
                    yf_ref, yb_ref, sre_ref, sim_ref, hre_ref, him_ref, ytoe_ref, yflat_ref):
    @pl.when(pl.program_id(0) == 0)
    def _():
        hre_ref[...] = jnp.zeros_like(hre_ref)
        him_ref[...] = jnp.zeros_like(him_ref)

    bs = SSM_BLOCK_STATES
    rows_c = SCAN_CHUNKS * SUBLANES

    for d, (u_ref, y_ref) in enumerate(((uf_ref, yf_ref), (ub_ref, yb_ref))):
        for q in range(SSM_BLOCKS):
            uc = jnp.concatenate([u_ref[q, :, j].reshape(rows_c, LANES) for j in range(SCAN_LC)],
                                 axis=1).astype(BF16)
            sre_ref[d, :, q * bs:(q + 1) * bs] = _dot(uc, fre_ref[d, q])
            sim_ref[d, :, q * bs:(q + 1) * bs] = _dot(uc, fim_ref[d, q])
            ytoe_ref[q] = _dot(uc, toe_ref[d, q])

        for q in range(SSM_BLOCKS):
            cols = slice(q * bs, (q + 1) * bs)
            ar = are_ref[d, :, cols]
            ai = aim_ref[d, :, cols]

            def step(c, carry, d=d, cols=cols, ar=ar, ai=ai):
                hr, hi = carry
                cc = c if d == 0 else SCAN_CHUNKS - 1 - c
                rows = pl.ds(pl.multiple_of(cc * SUBLANES, SUBLANES), SUBLANES)
                inc_r = sre_ref[d, rows, cols]
                inc_i = sim_ref[d, rows, cols]
                sre_ref[d, rows, cols] = hr
                sim_ref[d, rows, cols] = hi
                return ar * hr - ai * hi + inc_r, ar * hi + ai * hr + inc_i

            hr, hi = lax.fori_loop(0, SCAN_CHUNKS, step, (hre_ref[d, :, cols], him_ref[d, :, cols]),
                                   unroll=4)
            hre_ref[d, :, cols] = hr
            him_ref[d, :, cols] = hi

        for q in range(SSM_BLOCKS):
            cols = slice(q * bs, (q + 1) * bs)
            y = (ytoe_ref[q] + _dot(sre_ref[d, :, cols].astype(BF16), ore_ref[d, q])
                 - _dot(sim_ref[d, :, cols].astype(BF16), oim_ref[d, q]))
            for c in range(SCAN_CHUNKS):
                for j in range(SCAN_LC):
                    yflat_ref[q, pl.ds((c * SCAN_LC + j) * SUBLANES, SUBLANES), :] = (
                        y[c * SUBLANES:(c + 1) * SUBLANES, j * LANES:(j + 1) * LANES])

        for b in range(SUBLANES):
            for q in range(SSM_BLOCKS):
                y_ref[b, :, q * LANES:(q + 1) * LANES] = (
                    yflat_ref[q, pl.ds(b, SCAN_T, stride=SUBLANES), :].astype(BF16))


def _s5_scan(u_tm, f_re, f_im, o_re, o_im, toe, a_re, a_im, seq):
    rows_total = u_tm.shape[1]
    bsz = rows_total // seq
    assert bsz == SUBLANES and SCAN_T % SCAN_LC == 0 and CTX_LEN % SCAN_T == 0
    nt = seq // SCAN_T
    nc = CTX_LEN // SCAN_T
    chunked = (SSM_BLOCKS, seq // SCAN_LC, SCAN_LC, bsz, LANES)
    block = (SSM_BLOCKS, SCAN_CHUNKS, SCAN_LC, bsz, LANES)

    def bwd_tile(i):
        return jnp.where(i < nc, nc - 1 - i, nt - 1 - (i - nc))

    fwd_spec = pl.BlockSpec(block, lambda i: (0, i, 0, 0, 0))
    bwd_spec = pl.BlockSpec(block, lambda i: (0, bwd_tile(i), 0, 0, 0))
    full = lambda a: pl.BlockSpec(a.shape, lambda i: (0,) * a.ndim, pipeline_mode=pl.Buffered(1))
    y_sds = jax.ShapeDtypeStruct((bsz, seq, SSM_W), BF16)
    y_block = (bsz, SCAN_T, SSM_W)
    u5 = u_tm.reshape(chunked)
    return pl.pallas_call(
        _s5_scan_kernel,
        grid=(nt,),
        in_specs=[fwd_spec, bwd_spec, full(f_re), full(f_im), full(o_re), full(o_im), full(toe),
                  full(a_re), full(a_im)],
        out_specs=[pl.BlockSpec(y_block, lambda i: (0, i, 0)),
                   pl.BlockSpec(y_block, lambda i: (0, bwd_tile(i), 0))],
        out_shape=[y_sds, y_sds],
        scratch_shapes=[pltpu.VMEM((2, SCAN_CHUNKS * bsz, SSM_STATES), F32),
                        pltpu.VMEM((2, SCAN_CHUNKS * bsz, SSM_STATES), F32),
                        pltpu.VMEM((2, bsz, SSM_STATES), F32),
                        pltpu.VMEM((2, bsz, SSM_STATES), F32),
                        pltpu.VMEM((SSM_BLOCKS, SCAN_CHUNKS * bsz, SCAN_LC * LANES), F32),
                        pltpu.VMEM((SSM_BLOCKS, SCAN_T * bsz, LANES), F32)],
        compiler_params=_cparams("arbitrary"),
        name="s5_scan",
    )(u5, u5, f_re, f_im, o_re, o_im, toe, a_re, a_im)


def _attention_kernel(qt_ref, k_ref, vt_ref, bias_ref, o_ref, s_even_ref, s_odd_ref, *, grid_rows):
    i = pl.program_id(1)
    hd = NA_HEAD_DIM
    qt = (qt_ref[...].astype(F32) * (hd ** -0.5 * LOG2_E)).astype(BF16)
    zeros = jnp.zeros((hd, TM), BF16)
    ones = jnp.ones((BF16_ROWS, NA_KEY_BLOCK), BF16)
    ctx_rows = slice(0, CTX_LEN)

    def pair_cols(h):
        return slice((h // 2) * LANES, (h // 2 + 1) * LANES)

    def padded_qt(h):
        qh = qt[h * hd:(h + 1) * hd]
        return jnp.concatenate([qh, zeros] if h % 2 == 0 else [zeros, qh], axis=0)

    @pl.when(i == 0)
    def _():
        for h in range(NA_HEADS):
            s = _dot(k_ref[ctx_rows, pair_cols(h)], padded_qt(h))
            p = jnp.exp2(s - jnp.max(s, axis=0, keepdims=True)).astype(BF16)
            ol = _dot(jnp.concatenate([vt_ref[0, h * hd:(h + 1) * hd, :], ones], axis=0), p)
            o_ref[h * hd:(h + 1) * hd, :] = (ol[0:hd] / ol[hd:hd + 1]).astype(BF16)

    @pl.when(i > 0)
    def _():
        r_a = (i - 1) * NA_QROWS
        k_start = jnp.clip(r_a - WIN_H // 2, 0, grid_rows - NA_KROWS)
        start = pl.multiple_of(CTX_LEN + k_start * GRID_W, NA_KEY_BLOCK)
        kb = NA_KEY_BLOCK
        blk0 = start // kb
        n_loc = NA_KROWS * GRID_W // kb
        n_all = n_loc + 1
        k_rows = [pl.ds(start + j * kb, kb) for j in range(n_loc)] + [ctx_rows]
        v_blocks = [blk0 + j for j in range(n_loc)] + [0]
        rhs = [padded_qt(h) for h in range(NA_HEADS)]
        s_bufs = (s_even_ref, s_odd_ref)

        def scores(h, j):
            s = _dot(k_ref[k_rows[j], pair_cols(h)], rhs[h])
            if j < n_loc:
                s = s + bias_ref[h, j * kb:(j + 1) * kb, :]
            s_bufs[h % 2][j * kb:(j + 1) * kb] = s
            return jnp.max(s, axis=0, keepdims=True)

        m = functools.reduce(jnp.maximum, [scores(0, j) for j in range(n_all)])
        for h in range(NA_HEADS):
            ol = None
            next_max = []
            for j in range(n_all):
                if h + 1 < NA_HEADS:
                    next_max.append(scores(h + 1, j))
                p = jnp.exp2(s_bufs[h % 2][j * kb:(j + 1) * kb] - m).astype(BF16)
                lhs = jnp.concatenate([vt_ref[v_blocks[j], h * hd:(h + 1) * hd, :], ones], axis=0)
                part = _dot(lhs, p)
                ol = part if ol is None else ol + part
            o_ref[h * hd:(h + 1) * hd, :] = (ol[0:hd] / ol[hd:hd + 1]).astype(BF16)
            if next_max:
                m = functools.reduce(jnp.maximum, next_max)


def _na_tile_classes(grid_rows):
    return ((0, 0), (NA_QROWS, 0), (grid_rows - NA_QROWS, grid_rows - NA_KROWS))


def _na_valid(grid_rows):
    kr = np.arange(NA_KROWS)[:, None, None, None]
    kc = np.arange(GRID_W)[None, :, None, None]
    rq = np.arange(NA_QROWS)[None, None, :, None]
    cq = np.arange(GRID_W)[None, None, None, :]
    oks = []
    for r_a, k_start in _na_tile_classes(grid_rows):
        r0 = np.clip(r_a + rq - WIN_H // 2, 0, grid_rows - WIN_H)
        krow = k_start + kr
        cs = np.clip(cq - WIN_W // 2, 0, GRID_W - WIN_W)
        ok = (krow >= r0) & (krow < r0 + WIN_H) & (kc >= cs) & (kc < cs + WIN_W)
        oks.append(ok.reshape(NA_KROWS * GRID_W, TM))
    return np.stack(oks)


NA_DR_MARGIN = NA_QROWS
NA_DR_ROWS = 2 * WIN_H - 1 + 2 * NA_DR_MARGIN + 1


def _na_bias_kernel(w_ref, mask_ref, out_ref, *, grid_rows):
    lo_half = lax.broadcasted_iota(jnp.int32, (GRID_W, LANES), 1) < GRID_W
    cache = {}

    def toeplitz(d, upper):
        if (d, upper) not in cache:
            row = jnp.broadcast_to(w_ref[d:d + 1, :], (GRID_W, LANES))
            shift = GRID_W + 1 + (GRID_W if upper else 0)
            cache[d, upper] = pltpu.roll(row, shift, 1, stride=1, stride_axis=0)
        return cache[d, upper]

    for cls, (r_a, k_start) in enumerate(_na_tile_classes(grid_rows)):
        for kr in range(NA_KROWS):
            for pair in range(NA_QROWS // 2):
                d = k_start + kr - (r_a + 2 * pair) + (WIN_H - 1) + NA_DR_MARGIN
                block = jnp.where(lo_half, toeplitz(d, False), toeplitz(d - 1, True))
                rows = slice(kr * GRID_W, (kr + 1) * GRID_W)
                cols = slice(pair * LANES, (pair + 1) * LANES)
                out_ref[cls, rows, cols] = block + mask_ref[cls, rows, cols]


def _na_bias_tables(rpb, grid_rows):
    depth, heads, n_dr, n_dc = rpb.shape
    n_keys = NA_KROWS * GRID_W
    first_lane = (GRID_W - 1) - (WIN_W - 1)
    w = jnp.pad(jnp.flip(rpb * LOG2_E, axis=-1),
                ((0, 0), (0, 0), (NA_DR_MARGIN, NA_DR_ROWS - n_dr - NA_DR_MARGIN),
                 (first_lane, LANES - first_lane - n_dc)))
    mask = jnp.asarray(np.where(_na_valid(grid_rows), 0.0, NEG_INF).astype(np.float32))
    return pl.pallas_call(
        functools.partial(_na_bias_kernel, grid_rows=grid_rows),
        grid=(depth, heads),
        in_specs=[pl.BlockSpec((None, None, NA_DR_ROWS, LANES), lambda l, h: (l, h, 0, 0)),
                  pl.BlockSpec((3, n_keys, TM), lambda l, h: (0, 0, 0))],
        out_specs=pl.BlockSpec((None, 3, None, n_keys, TM), lambda l, h: (l, 0, h, 0, 0)),
        out_shape=jax.ShapeDtypeStruct((depth, 3, heads, n_keys, TM), F32),
        compiler_params=pltpu.CompilerParams(dimension_semantics=("parallel", "parallel")),
        name="na_bias",
    )(w, mask)


def _attention(qt, k, vt, bias, grid_rows):
    bsz, s, _ = k.shape
    nt = s // TM
    n_keys = NA_KROWS * GRID_W

    def bias_class(b, i):
        return (jnp.where(i <= 1, 0, jnp.where(i == nt - 1, 2, 1)), 0, 0, 0)

    return pl.pallas_call(
        functools.partial(_attention_kernel, grid_rows=grid_rows),
        grid=(bsz, nt),
        in_specs=[_fm_spec(),
                  pl.BlockSpec((None, s, NA_W), lambda b, i: (b, 0, 0)),
                  pl.BlockSpec((None, nt, NA_W, TM), lambda b, i: (b, 0, 0, 0)),
                  pl.BlockSpec((None, NA_HEADS, n_keys, TM), bias_class)],
        out_specs=_fm_spec(),
        out_shape=jax.ShapeDtypeStruct((bsz, nt, NA_W, TM), BF16),
        scratch_shapes=[pltpu.VMEM((n_keys + CTX_LEN, TM), F32),
                        pltpu.VMEM((n_keys + CTX_LEN, TM), F32)],
        compiler_params=_cparams("parallel", "arbitrary"),
        name="attention",
    )(qt, k, vt, bias)


def _gelu_tanh(x):
    return 0.5 * x * (1.0 + jnp.tanh(math.sqrt(2.0 / math.pi) * (x + 0.044715 * (x * x * x))))


def _merge_kernel(*refs, n_tiles, first):
    c_ref = refs[1] if first else None
    x_ref = refs[0]
    (cg_ref, cprev_ref, cnext_ref, u_ref, yf_ref, yb_ref, o_ref, gate_ref, g1_ref, cw_ref, cout_ref,
     d_ref, glua_ref, glub_ref, naout_ref, wout_ref, out_ref) = refs[2:] if first else refs[1:]
    i = pl.program_id(0)

    def gated_input(ref):
        return ref[:, 2 * CONV_W:3 * CONV_W].astype(F32) * ref[:, 0:CONV_W].astype(F32)

    z = gated_input(cg_ref)
    has_prev = (i >= 2).astype(F32)
    has_next = jnp.logical_and(i != 0, i != n_tiles - 1).astype(F32)
    z_before = gated_input(cprev_ref)[BF16_ROWS - 1:BF16_ROWS] * has_prev
    z_after = gated_input(cnext_ref)[0:1] * has_next
    row = lax.broadcasted_iota(jnp.int32, z.shape, 0)
    z_prev = jnp.where(row == 0, z_before, pltpu.roll(z, 1, 0))
    z_next = jnp.where(row == TM - 1, z_after, pltpu.roll(z, TM - 1, 0))
    conv = z_prev * cw_ref[0:1, :] + z * cw_ref[1:2, :] + z_next * cw_ref[2:3, :]
    a_pre = cg_ref[:, CONV_W:2 * CONV_W].astype(F32) * conv
    ya = _dot(a_pre.astype(BF16), cout_ref[...])

    y = (yf_ref[...].astype(F32) + yb_ref[...].astype(F32)
         + d_ref[...] * u_ref[...].astype(F32))
    g = _gelu_tanh(y).astype(BF16)
    def twice_logistic(half_x):
        return jnp.tanh(half_x) + 1.0

    yb = _dot(g, glua_ref[...]) * twice_logistic(_dot(g, glub_ref[...]))

    yc = lax.dot_general(o_ref[...], naout_ref[...], (((0,), (0,)), ((), ())),
                         preferred_element_type=F32)

    mix2 = (twice_logistic(gate_ref[:, 0:D_MODEL].astype(F32)) * ya
            + twice_logistic(gate_ref[:, D_MODEL:2 * D_MODEL].astype(F32)) * yb
            + twice_logistic(gate_ref[:, 2 * D_MODEL:3 * D_MODEL].astype(F32)) * yc)
    out_ref[...] = _stream_tile(x_ref, c_ref) + g1_ref[...] * _dot(mix2.astype(BF16), wout_ref[...])


def _merge(stream, conv_g, u_tm, yf, yb, attn_o, gates, gate1, conv_w, conv_out, s5_d, glu_a, glu_b,
           na_out, w_out):
    first = len(stream) == 2
    bsz, s, _ = conv_g.shape
    nt = s // TM
    halo_blocks = s // BF16_ROWS
    per_tile = TM // BF16_ROWS
    tok = lambda width: _tok_spec(width, batch_major=False)
    tm_spec = tok(SSM_W)
    prev_spec = pl.BlockSpec((None, BF16_ROWS, 3 * CONV_W),
                             lambda i, b: (b, jnp.maximum(i * per_tile - 1, 0), 0))
    next_spec = pl.BlockSpec((None, BF16_ROWS, 3 * CONV_W),
                             lambda i, b: (b, jnp.minimum((i + 1) * per_tile, halo_blocks - 1), 0))
    return pl.pallas_call(
        functools.partial(_merge_kernel, n_tiles=nt, first=first),
        grid=(nt, bsz),
        in_specs=_stream_specs(first) + [
            tok(3 * CONV_W), prev_spec, next_spec, tm_spec, tm_spec, tm_spec,
            _fm_spec(batch_major=False), tok(3 * D_MODEL), _mod_spec(batch_major=False),
            _const_spec((3, CONV_W)), _const_spec((CONV_W, D_MODEL)), _const_spec((1, SSM_W)),
            _const_spec((SSM_W, D_MODEL)), _const_spec((SSM_W, D_MODEL)),
            _const_spec((NA_W, D_MODEL)), _const_spec((D_MODEL, D_MODEL))],
        out_specs=tok(D_MODEL),
        out_shape=jax.ShapeDtypeStruct((bsz, s, D_MODEL), F32),
        compiler_params=_cparams("parallel", "arbitrary"),
        name="merge",
    )(*stream, conv_g, conv_g, conv_g, u_tm, yf, yb, attn_o, gates, gate1, conv_w, conv_out, s5_d,
      glu_a, glu_b, na_out, w_out)


def _mlp_kernel(x_ref, sh_ref, sc_ref, gt_ref, g_ref, w1_ref, w2_ref, fg_ref, out_ref, *, final):
    x = x_ref[...]
    h = (_rms(x, g_ref[...]) * (1.0 + sc_ref[...]) + sh_ref[...]).astype(BF16)
    acc = jnp.zeros_like(x)
    for c in range(MLP_HIDDEN // D_MODEL):
        cols = slice(c * D_MODEL, (c + 1) * D_MODEL)
        a = jnp.maximum(_dot(h, w1_ref[:, cols]), 0.0)
        acc = acc + _dot((a * a).astype(BF16), w2_ref[cols, :])
    y = x + gt_ref[...] * acc
    if final:
        y = _rms(y, fg_ref[...])
    out_ref[...] = y


def _mlp(xc, shift, scale, gate, g, w1, w2, final_g, final):
    bsz, s, _ = xc.shape
    first = CTX_LEN // TM if final else 0
    nt = s // TM - first
    tok_in = pl.BlockSpec((None, TM, D_MODEL), lambda b, i: (b, i + first, 0))
    mod = pl.BlockSpec((None, None, 1, D_MODEL), lambda b, i: (b, jnp.minimum(i + first, 1), 0, 0))
    return pl.pallas_call(
        functools.partial(_mlp_kernel, final=final),
        grid=(bsz, nt),
        in_specs=[tok_in, mod, mod, mod, _const_spec((1, D_MODEL)),
                  _const_spec((D_MODEL, MLP_HIDDEN)), _const_spec((MLP_HIDDEN, D_MODEL)),
                  _const_spec((1, D_MODEL))],
        out_specs=_tok_spec(D_MODEL),
        out_shape=jax.ShapeDtypeStruct((bsz, nt * TM, D_MODEL), F32),
        compiler_params=_cparams("parallel", "parallel"),
        name="mlp",
    )(xc, shift, scale, gate, g, w1, w2, final_g)


def kernel(x, c, ctx, c_ctx, w_mod, b_mod, norm1_g, w_in, conv_w, conv_out, s5_lam_re, s5_lam_im,
           s5_log_step, s5_b_re, s5_b_im, s5_c_re, s5_c_im, s5_d, s5_glu_a, s5_glu_b, na_rpb,
           na_out, w_out, norm2_g, mlp_w1, mlp_w2, final_norm_g):
    bsz, seq, _ = x.shape
    depth = w_mod.shape[0]
    assert ctx.shape[1] == CTX_LEN == TM and seq % TM == 0 and bsz == SUBLANES
    grid_rows = seq // GRID_W
    assert grid_rows >= NA_KROWS and grid_rows % NA_QROWS == 0
    s = CTX_LEN + seq

    cond_rows = 2 * SUBLANES
    cond = jnp.zeros((cond_rows, D_MODEL), F32).at[:bsz].set(c).at[bsz].set(c_ctx)
    mods = _adaln(cond, w_mod, b_mod).reshape(depth, cond_rows, N_MOD, D_MODEL)
    lat = mods[:, :bsz]
    cx = jnp.broadcast_to(mods[:, bsz:bsz + 1], lat.shape)
    mod = jnp.stack([cx, lat], axis=2)[:, :, :, :, None, :]

    s5_mats = _s5_matrices(s5_lam_re, s5_lam_im, s5_log_step, s5_b_re, s5_b_im, s5_c_re, s5_c_im)

    bf = lambda w: w.astype(BF16)
    gate_cols = (jnp.arange(IN_PROJ_W) >= IN_OFF[7])
    w_in_b = bf(w_in * jnp.where(gate_cols, 0.5, 1.0).astype(F32))
    conv_out_b, glu_a_b, glu_b_b = bf(conv_out), bf(s5_glu_a * 0.5), bf(s5_glu_b * 0.5)
    na_out_b, w_out_b, w1_b, w2_b = bf(na_out), bf(w_out * 0.5), bf(mlp_w1), bf(mlp_w2)
    w_qv_t = _qv_rows(w_in)

    na_bias = _na_bias_tables(na_rpb, grid_rows)

    stream = (x, ctx)
    row = lambda v: v.reshape(1, -1)
    for l in range(depth):
        m = mod[l]
        conv_g, u_tm, u_bm, qt, k, vt, gates = _in_proj(stream, m[:, :, 0], m[:, :, 1],
                                                        row(norm1_g[l]), w_in_b[l], w_qv_t[l])
        yf, yb = _s5_scan(u_tm, *(mat[l] for mat in s5_mats), s)
        attn_o = _attention(qt, k, vt, na_bias[l], grid_rows)
        xc = _merge(stream, conv_g, u_bm, yf, yb, attn_o, gates, m[:, :, 2], conv_w[l],
                    conv_out_b[l], row(s5_d[l]), glu_a_b[l], glu_b_b[l], na_out_b[l], w_out_b[l])
        xc = _mlp(xc, m[:, :, 3], m[:, :, 4], m[:, :, 5], row(norm2_g[l]), w1_b[l], w2_b[l],
                  row(final_norm_g), final=(l == depth - 1))
        stream = (xc,)
    return xc
```

```python
import functools
import math

import numpy as np
import jax
import jax.numpy as jnp
from jax import lax
from jax.experimental import pallas as pl
from jax.experimental.pallas import tpu as pltpu

D_MODEL = 1024
CTX_LEN = 256
GRID_W = 64
N_MOD = 6
CONV_W = 512
SSM_W = 512
SSM_GROUP = 16
SSM_GROUPS = SSM_W // SSM_GROUP
SSM_STATE = 64
NA_HEADS = 8
NA_HEAD_DIM = 64
NA_W = NA_HEADS * NA_HEAD_DIM
WIN_H = 8
WIN_W = 16
MLP_HIDDEN = 4 * D_MODEL
IN_SIZES = (CONV_W, CONV_W, CONV_W, SSM_W, NA_W, NA_W, NA_W, D_MODEL, D_MODEL, D_MODEL)
IN_OFF = tuple(sum(IN_SIZES[:i]) for i in range(len(IN_SIZES) + 1))
IN_PROJ_W = IN_OFF[-1]
RMS_EPS = 1e-6
NEG_INF = -1e30
S5_MIN_DECAY = 1e-4
LOG2_E = math.log2(math.e)

F32 = jnp.float32
BF16 = jnp.bfloat16

LANES = 128
SUBLANES = 8
BF16_ROWS = 16
TM = 256
SCAN_LC = 4
SCAN_T = 128
SCAN_CHUNKS = SCAN_T // SCAN_LC
NA_QROWS = TM // GRID_W
NA_KROWS = NA_QROWS + WIN_H
NA_KEY_BLOCK = TM
SSM_LANE_GROUPS = LANES // SSM_GROUP
SSM_BLOCKS = SSM_W // LANES
SSM_BLOCK_STATES = SSM_LANE_GROUPS * SSM_STATE
SSM_STATES = SSM_GROUPS * SSM_STATE
VMEM_LIMIT = 56 * 1024 * 1024


def _cparams(*sem):
    return pltpu.CompilerParams(dimension_semantics=sem, vmem_limit_bytes=VMEM_LIMIT)


def _rms(x, g):
    return x * lax.rsqrt(jnp.mean(x * x, axis=-1, keepdims=True) + RMS_EPS) * g


def _dot(a, b):
    return jnp.dot(a, b, preferred_element_type=F32)


def _dot_nt(a, b):
    return lax.dot_general(a, b, (((1,), (1,)), ((), ())), preferred_element_type=F32)


def _adaln_kernel(c_ref, w_ref, b_ref, o_ref):
    c = c_ref[...]
    s = c * jax.nn.sigmoid(c)
    o_ref[...] = jnp.dot(s, w_ref[...], preferred_element_type=F32,
                         precision=lax.Precision.HIGHEST) + b_ref[...]


def _adaln(cond, w_mod, b_mod):
    depth = w_mod.shape[0]
    rows = cond.shape[0]
    n_tiles = (N_MOD * D_MODEL) // D_MODEL
    return pl.pallas_call(
        _adaln_kernel,
        grid=(depth, n_tiles),
        in_specs=[
            pl.BlockSpec((rows, D_MODEL), lambda l, j: (0, 0)),
            pl.BlockSpec((None, D_MODEL, D_MODEL), lambda l, j: (l, 0, j)),
            pl.BlockSpec((None, 1, D_MODEL), lambda l, j: (l, 0, j)),
        ],
        out_specs=pl.BlockSpec((None, rows, D_MODEL), lambda l, j: (l, 0, j)),
        out_shape=jax.ShapeDtypeStruct((depth, rows, N_MOD * D_MODEL), F32),
        compiler_params=_cparams("parallel", "parallel"),
        name="adaln",
    )(cond, w_mod, b_mod.reshape(depth, 1, N_MOD * D_MODEL))


def _s5_prep_kernel(lre_ref, lim_ref, ls_ref, bre_ref, bim_ref, cre_ref, cim_ref,
                    are_ref, aim_ref, fre_ref, fim_ref, ore_ref, oim_ref):
    lr = jnp.minimum(lre_ref[...], -S5_MIN_DECAY)
    li = lim_ref[...]
    dt = jnp.exp(ls_ref[...])
    xr = lr * dt
    xi = li * dt

    def power(k):
        e = jnp.exp(k * xr)
        return e * jnp.cos(k * xi), e * jnp.sin(k * xi)

    ar, ai = power(1.0)
    nr = ar - 1.0
    den = lr * lr + li * li
    cr = (nr * lr + ai * li) / den
    ci = (ai * lr - nr * li) / den
    bre = bre_ref[...]
    bim = bim_ref[...]
    bbr = cr * bre - ci * bim
    bbi = cr * bim + ci * bre
    cre = cre_ref[...]
    cim = cim_ref[...]
    fre_ref[:, 0] = bbr
    fim_ref[:, 0] = bbi
    ore_ref[:, 0] = cre
    oim_ref[:, 0] = cim
    for k in range(1, SCAN_LC + 1):
        pr, pi = power(float(k))
        if k < SCAN_LC:
            fre_ref[:, k] = pr * bbr - pi * bbi
            fim_ref[:, k] = pr * bbi + pi * bbr
        ore_ref[:, k] = pr * cre - pi * cim
        oim_ref[:, k] = pr * cim + pi * cre
    are_ref[...], aim_ref[...] = power(float(SCAN_LC))


def _s5_prep(lam_re, lam_im, log_step, b_re, b_im, c_re, c_im):
    depth = lam_re.shape[0]
    rows = depth * 2 * SSM_GROUPS
    rb = SSM_GROUPS
    lre = lam_re.reshape(rows, 1, SSM_STATE)
    lim = lam_im.reshape(rows, 1, SSM_STATE)
    ls = jnp.broadcast_to(log_step.reshape(rows, 1, 1), (rows, 1, SSM_STATE))
    bre = jnp.swapaxes(b_re.reshape(rows, SSM_STATE, SSM_GROUP), 1, 2)
    bim = jnp.swapaxes(b_im.reshape(rows, SSM_STATE, SSM_GROUP), 1, 2)
    cre = c_re.reshape(rows, SSM_GROUP, SSM_STATE)
    cim = c_im.reshape(rows, SSM_GROUP, SSM_STATE)
    vec = pl.BlockSpec((rb, 1, SSM_STATE), lambda r: (r, 0, 0))
    mat = pl.BlockSpec((rb, SSM_GROUP, SSM_STATE), lambda r: (r, 0, 0))
    pw = lambda n: pl.BlockSpec((rb, n, SSM_GROUP, SSM_STATE), lambda r: (r, 0, 0, 0))
    v2 = jax.ShapeDtypeStruct((rows, 1, SSM_STATE), F32)
    v4 = lambda n: jax.ShapeDtypeStruct((rows, n, SSM_GROUP, SSM_STATE), F32)
    return pl.pallas_call(
        _s5_prep_kernel,
        grid=(rows // rb,),
        in_specs=[vec, vec, vec, mat, mat, mat, mat],
        out_specs=[vec, vec, pw(SCAN_LC), pw(SCAN_LC), pw(SCAN_LC + 1), pw(SCAN_LC + 1)],
        out_shape=[v2, v2, v4(SCAN_LC), v4(SCAN_LC), v4(SCAN_LC + 1), v4(SCAN_LC + 1)],
        compiler_params=pltpu.CompilerParams(dimension_semantics=("parallel",)),
        name="s5_prep",
    )(lre, lim, ls, bre, bim, cre, cim)


def _s5_operands_kernel(fre_ref, fim_ref, ore_ref, oim_ref,
                        fold_re_ref, fold_im_ref, out_re_ref, out_im_ref, toe_ref, bd_ref):
    lc = SCAN_LC
    hi = lax.Precision.HIGHEST
    forward = pl.program_id(0) % 2 == 0
    bd_ref[...] = jnp.zeros_like(bd_ref)

    def block_diag(x):
        for g in range(SSM_LANE_GROUPS):
            bd_ref[g * SSM_GROUP:(g + 1) * SSM_GROUP, g * SSM_STATE:(g + 1) * SSM_STATE] = x[g]
        return bd_ref[...]

    for j in range(lc):
        for f_ref, fold_ref in ((fre_ref, fold_re_ref), (fim_ref, fold_im_ref)):
            f = jnp.where(forward, f_ref[:, lc - 1 - j], f_ref[:, j])
            fold_ref[j * LANES:(j + 1) * LANES, :] = block_diag(f).astype(BF16)

    oc_re = [block_diag(ore_ref[:, k]).T for k in range(lc + 1)]
    oc_im = [block_diag(oim_ref[:, k]).T for k in range(lc + 1)]
    for j in range(lc):
        cols = slice(j * LANES, (j + 1) * LANES)
        out_re_ref[:, cols] = jnp.where(forward, oc_re[j + 1], oc_re[lc - j]).astype(BF16)
        out_im_ref[:, cols] = jnp.where(forward, oc_im[j + 1], oc_im[lc - j]).astype(BF16)

    bd_re = block_diag(fre_ref[:, 0])
    bd_im = block_diag(fim_ref[:, 0])
    kk = [jnp.dot(bd_re, oc_re[k], preferred_element_type=F32, precision=hi)
          - jnp.dot(bd_im, oc_im[k], preferred_element_type=F32, precision=hi) for k in range(lc)]
    zero = jnp.zeros((LANES, LANES), F32)
    for a in range(lc):
        for b in range(lc):
            fwd = kk[b - a] if b >= a else zero
            bwd = kk[a - b] if a >= b else zero
            toe_ref[a * LANES:(a + 1) * LANES, b * LANES:(b + 1) * LANES] = (
                jnp.where(forward, fwd, bwd).astype(BF16))


def _s5_operands(f_re, f_im, o_re, o_im, depth):
    lc = SCAN_LC
    ng = SSM_LANE_GROUPS
    group_block = lambda n: pl.BlockSpec((ng, n, SSM_GROUP, SSM_STATE),
                                         lambda ld, q: (ld * SSM_BLOCKS + q, 0, 0, 0))
    out = lambda r, c: pl.BlockSpec((None, None, r, c), lambda ld, q: (ld, q, 0, 0))
    sds = lambda r, c: jax.ShapeDtypeStruct((2 * depth, SSM_BLOCKS, r, c), BF16)
    fold_shape = (lc * LANES, SSM_BLOCK_STATES)
    out_shape = (SSM_BLOCK_STATES, lc * LANES)
    toe_shape = (lc * LANES, lc * LANES)
    results = pl.pallas_call(
        _s5_operands_kernel,
        grid=(2 * depth, SSM_BLOCKS),
        in_specs=[group_block(lc), group_block(lc), group_block(lc + 1), group_block(lc + 1)],
        out_specs=[out(*fold_shape), out(*fold_shape), out(*out_shape), out(*out_shape),
                   out(*toe_shape)],
        out_shape=[sds(*fold_shape), sds(*fold_shape), sds(*out_shape), sds(*out_shape),
                   sds(*toe_shape)],
        scratch_shapes=[pltpu.VMEM((LANES, SSM_BLOCK_STATES), F32)],
        compiler_params=pltpu.CompilerParams(dimension_semantics=("parallel", "parallel")),
        name="s5_operands",
    )(f_re, f_im, o_re, o_im)
    return [r.reshape((depth, 2) + r.shape[1:]) for r in results]


def _s5_matrices(lam_re, lam_im, log_step, b_re, b_im, c_re, c_im):
    depth = lam_re.shape[0]
    a_re, a_im, f_re, f_im, o_re, o_im = _s5_prep(lam_re, lam_im, log_step, b_re, b_im, c_re, c_im)
    fold_re, fold_im, out_re, out_im, toe = _s5_operands(f_re, f_im, o_re, o_im, depth)
    a_shape = (depth, 2, 1, SSM_STATES)
    a_re = jnp.broadcast_to(a_re.reshape(a_shape), (depth, 2, SUBLANES, SSM_STATES))
    a_im = jnp.broadcast_to(a_im.reshape(a_shape), (depth, 2, SUBLANES, SSM_STATES))
    return fold_re, fold_im, out_re, out_im, toe, a_re, a_im


def _batch_rows(b):
    return pl.ds(b, TM, stride=SUBLANES)


def _qv_rows_kernel(w_ref, o_ref):
    o_ref[...] = w_ref[...].T.astype(BF16)


def _qv_rows(w_in):
    depth = w_in.shape[0]
    col_blocks = (IN_OFF[4] // NA_W, IN_OFF[6] // NA_W)
    assert IN_OFF[4] % NA_W == 0 and IN_OFF[6] % NA_W == 0
    return pl.pallas_call(
        _qv_rows_kernel,
        grid=(depth, 2),
        in_specs=[pl.BlockSpec((None, D_MODEL, NA_W),
                               lambda l, j: (l, 0, jnp.where(j == 0, col_blocks[0], col_blocks[1])))],
        out_specs=pl.BlockSpec((None, NA_W, D_MODEL), lambda l, j: (l, j, 0)),
        out_shape=jax.ShapeDtypeStruct((depth, 2 * NA_W, D_MODEL), BF16),
        compiler_params=pltpu.CompilerParams(dimension_semantics=("parallel", "parallel")),
        name="qv_rows",
    )(w_in)


def _stream_tile(x_ref, c_ref):
    if c_ref is None:
        return x_ref[...]
    return jnp.where(pl.program_id(0) < CTX_LEN // TM, c_ref[...], x_ref[...])


def _in_proj_kernel(*refs, first):
    c_ref = refs[1] if first else None
    x_ref = refs[0]
    (sh_ref, sc_ref, g_ref, w_ref, wqv_ref, conv_ref, u_ref, ub_ref, qt_ref, k_ref, vt_ref,
     gate_ref) = refs[2:] if first else refs[1:]
    h = _rms(_stream_tile(x_ref, c_ref), g_ref[...]) * (1.0 + sc_ref[...]) + sh_ref[...]
    hb = h.astype(BF16)
    conv_ref[...] = _dot(hb, w_ref[:, IN_OFF[0]:IN_OFF[3]]).astype(BF16)
    u = _dot(hb, w_ref[:, IN_OFF[3]:IN_OFF[4]])
    rows = _batch_rows(pl.program_id(1))
    for q in range(SSM_BLOCKS):
        u_ref[q, rows, :] = u[:, q * LANES:(q + 1) * LANES]
    ub_ref[...] = u.astype(BF16)
    qt_ref[...] = _dot_nt(wqv_ref[0:NA_W, :], hb).astype(BF16)
    k_ref[...] = _dot(hb, w_ref[:, IN_OFF[5]:IN_OFF[6]]).astype(BF16)
    vt_ref[...] = _dot_nt(wqv_ref[NA_W:2 * NA_W, :], hb).astype(BF16)
    gate_ref[...] = _dot(hb, w_ref[:, IN_OFF[7]:IN_OFF[10]]).astype(BF16)


def _grid_bi(batch_major):
    return (lambda b, i: (b, i)) if batch_major else (lambda i, b: (b, i))


def _tok_spec(width, batch_major=True):
    bi = _grid_bi(batch_major)
    return pl.BlockSpec((None, TM, width), lambda *g: bi(*g) + (0,))


def _mod_spec(batch_major=True):
    bi = _grid_bi(batch_major)

    def index(*g):
        b, i = bi(*g)
        return (b, jnp.minimum(i, 1), 0, 0)

    return pl.BlockSpec((None, None, 1, D_MODEL), index)


def _const_spec(shape):
    nd = len(shape)
    return pl.BlockSpec(shape, lambda *g: (0,) * nd)


def _slab_spec():
    return pl.BlockSpec((SSM_BLOCKS, TM * SUBLANES, LANES), lambda i, b: (0, i, 0))


def _fm_spec(batch_major=True):
    bi = _grid_bi(batch_major)
    return pl.BlockSpec((None, None, NA_W, TM), lambda *g: bi(*g) + (0, 0))


def _stream_specs(first):
    if not first:
        return [_tok_spec(D_MODEL, batch_major=False)]
    nc = CTX_LEN // TM
    return [pl.BlockSpec((None, TM, D_MODEL), lambda i, b: (b, jnp.maximum(i - nc, 0), 0)),
            pl.BlockSpec((None, TM, D_MODEL), lambda i, b: (b, jnp.minimum(i, nc - 1), 0))]


def _in_proj(stream, shift, scale, g, w, w_qv_t):
    first = len(stream) == 2
    bsz = stream[0].shape[0]
    s = stream[0].shape[1] + (CTX_LEN if first else 0)
    nt = s // TM
    tok = lambda width: _tok_spec(width, batch_major=False)
    mod = _mod_spec(batch_major=False)
    fm = _fm_spec(batch_major=False)
    fm_sds = jax.ShapeDtypeStruct((bsz, nt, NA_W, TM), BF16)
    return pl.pallas_call(
        functools.partial(_in_proj_kernel, first=first),
        grid=(nt, bsz),
        in_specs=_stream_specs(first) + [mod, mod, _const_spec((1, D_MODEL)),
                                         _const_spec((D_MODEL, IN_PROJ_W)),
                                         _const_spec((2 * NA_W, D_MODEL))],
        out_specs=[tok(3 * CONV_W), _slab_spec(), tok(SSM_W), fm, tok(NA_W), fm, tok(3 * D_MODEL)],
        out_shape=[jax.ShapeDtypeStruct((bsz, s, 3 * CONV_W), BF16),
                   jax.ShapeDtypeStruct((SSM_BLOCKS, s * bsz, LANES), F32),
                   jax.ShapeDtypeStruct((bsz, s, SSM_W), BF16),
                   fm_sds,
                   jax.ShapeDtypeStruct((bsz, s, NA_W), BF16),
                   fm_sds,
                   jax.ShapeDtypeStruct((bsz, s, 3 * D_MODEL), BF16)],
        compiler_params=_cparams("parallel", "arbitrary"),
        name="in_proj",
    )(*stream, shift, scale, g, w, w_qv_t)


def _s5_scan_kernel(uf_ref, ub_ref, fre_ref, fim_ref, ore_ref, oim_ref, toe_ref, are_ref, aim_ref,
                    yf_ref, yb_ref, sre_ref, sim_ref, hre_ref, him_ref, ytoe_ref, yflat_ref):
    @pl.when(pl.program_id(0) == 0)
    def _():
        hre_ref[...] = jnp.zeros_like(hre_ref)
        him_ref[...] = jnp.zeros_like(him_ref)

    bs = SSM_BLOCK_STATES
    rows_c = SCAN_CHUNKS * SUBLANES

    for d, (u_ref, y_ref) in enumerate(((uf_ref, yf_ref), (ub_ref, yb_ref))):
        for q in range(SSM_BLOCKS):
            uc = jnp.concatenate([u_ref[q, :, j].reshape(rows_c, LANES) for j in range(SCAN_LC)],
                                 axis=1).astype(BF16)
            sre_ref[d, :, q * bs:(q + 1) * bs] = _dot(uc, fre_ref[d, q])
            sim_ref[d, :, q * bs:(q + 1) * bs] = _dot(uc, fim_ref[d, q])
            ytoe_ref[q] = _dot(uc, toe_ref[d, q])

        for q in range(SSM_BLOCKS):
            cols = slice(q * bs, (q + 1) * bs)
            ar = are_ref[d, :, cols]
            ai = aim_ref[d, :, cols]

            def step(c, carry, d=d, cols=cols, ar=ar, ai=ai):
                hr, hi = carry
                cc = c if d == 0 else SCAN_CHUNKS - 1 - c
                rows = pl.ds(pl.multiple_of(cc * SUBLANES, SUBLANES), SUBLANES)
                inc_r = sre_ref[d, rows, cols]
                inc_i = sim_ref[d, rows, cols]
                sre_ref[d, rows, cols] = hr
                sim_ref[d, rows, cols] = hi
                return ar * hr - ai * hi + inc_r, ar * hi + ai * hr + inc_i

            hr, hi = lax.fori_loop(0, SCAN_CHUNKS, step, (hre_ref[d, :, cols], him_ref[d, :, cols]),
                                   unroll=4)
            hre_ref[d, :, cols] = hr
            him_ref[d, :, cols] = hi

        for q in range(SSM_BLOCKS):
            cols = slice(q * bs, (q + 1) * bs)
            y = (ytoe_ref[q] + _dot(sre_ref[d, :, cols].astype(BF16), ore_ref[d, q])
                 - _dot(sim_ref[d, :, cols].astype(BF16), oim_ref[d, q]))
            for c in range(SCAN_CHUNKS):
                for j in range(SCAN_LC):
                    yflat_ref[q, pl.ds((c * SCAN_LC + j) * SUBLANES, SUBLANES), :] = (
                        y[c * SUBLANES:(c + 1) * SUBLANES, j * LANES:(j + 1) * LANES])

        for b in range(SUBLANES):
            for q in range(SSM_BLOCKS):
                y_ref[b, :, q * LANES:(q + 1) * LANES] = (
                    yflat_ref[q, pl.ds(b, SCAN_T, stride=SUBLANES), :].astype(BF16))


def _s5_scan(u_tm, f_re, f_im, o_re, o_im, toe, a_re, a_im, seq):
    rows_total = u_tm.shape[1]
    bsz = rows_total // seq
    assert bsz == SUBLANES and SCAN_T % SCAN_LC == 0 and CTX_LEN % SCAN_T == 0
    nt = seq // SCAN_T
    nc = CTX_LEN // SCAN_T
    chunked = (SSM_BLOCKS, seq // SCAN_LC, SCAN_LC, bsz, LANES)
    block = (SSM_BLOCKS, SCAN_CHUNKS, SCAN_LC, bsz, LANES)

    def bwd_tile(i):
        return jnp.where(i < nc, nc - 1 - i, nt - 1 - (i - nc))

    fwd_spec = pl.BlockSpec(block, lambda i: (0, i, 0, 0, 0))
    bwd_spec = pl.BlockSpec(block, lambda i: (0, bwd_tile(i), 0, 0, 0))
    full = lambda a: pl.BlockSpec(a.shape, lambda i: (0,) * a.ndim, pipeline_mode=pl.Buffered(1))
    y_sds = jax.ShapeDtypeStruct((bsz, seq, SSM_W), BF16)
    y_block = (bsz, SCAN_T, SSM_W)
    u5 = u_tm.reshape(chunked)
    return pl.pallas_call(
        _s5_scan_kernel,
        grid=(nt,),
        in_specs=[fwd_spec, bwd_spec, full(f_re), full(f_im), full(o_re), full(o_im), full(toe),
                  full(a_re), full(a_im)],
        out_specs=[pl.BlockSpec(y_block, lambda i: (0, i, 0)),
                   pl.BlockSpec(y_block, lambda i: (0, bwd_tile(i), 0))],
        out_shape=[y_sds, y_sds],
        scratch_shapes=[pltpu.VMEM((2, SCAN_CHUNKS * bsz, SSM_STATES), F32),
                        pltpu.VMEM((2, SCAN_CHUNKS * bsz, SSM_STATES), F32),
                        pltpu.VMEM((2, bsz, SSM_STATES), F32),
                        pltpu.VMEM((2, bsz, SSM_STATES), F32),
                        pltpu.VMEM((SSM_BLOCKS, SCAN_CHUNKS * bsz, SCAN_LC * LANES), F32),
                        pltpu.VMEM((SSM_BLOCKS, SCAN_T * bsz, LANES), F32)],
        compiler_params=_cparams("arbitrary"),
        name="s5_scan",
    )(u5, u5, f_re, f_im, o_re, o_im, toe, a_re, a_im)


def _attention_kernel(qt_ref, k_ref, vt_ref, bias_ref, o_ref, s_even_ref, s_odd_ref, *, grid_rows):
    i = pl.program_id(1)
    hd = NA_HEAD_DIM
    qt = (qt_ref[...].astype(F32) * (hd ** -0.5 * LOG2_E)).astype(BF16)
    zeros = jnp.zeros((hd, TM), BF16)
    ones = jnp.ones((BF16_ROWS, NA_KEY_BLOCK), BF16)
    ctx_rows = slice(0, CTX_LEN)

    def pair_cols(h):
        return slice((h // 2) * LANES, (h // 2 + 1) * LANES)

    def padded_qt(h):
        qh = qt[h * hd:(h + 1) * hd]
        return jnp.concatenate([qh, zeros] if h % 2 == 0 else [zeros, qh], axis=0)

    @pl.when(i == 0)
    def _():
        for h in range(NA_HEADS):
            s = _dot(k_ref[ctx_rows, pair_cols(h)], padded_qt(h))
            p = jnp.exp2(s - jnp.max(s, axis=0, keepdims=True)).astype(BF16)
            ol = _dot(jnp.concatenate([vt_ref[0, h * hd:(h + 1) * hd, :], ones], axis=0), p)
            o_ref[h * hd:(h + 1) * hd, :] = (ol[0:hd] / ol[hd:hd + 1]).astype(BF16)

    @pl.when(i > 0)
    def _():
        r_a = (i - 1) * NA_QROWS
        k_start = jnp.clip(r_a - WIN_H // 2, 0, grid_rows - NA_KROWS)
        start = pl.multiple_of(CTX_LEN + k_start * GRID_W, NA_KEY_BLOCK)
        kb = NA_KEY_BLOCK
        blk0 = start // kb
        n_loc = NA_KROWS * GRID_W // kb
        n_all = n_loc + 1
        k_rows = [pl.ds(start + j * kb, kb) for j in range(n_loc)] + [ctx_rows]
        v_blocks = [blk0 + j for j in range(n_loc)] + [0]
        rhs = [padded_qt(h) for h in range(NA_HEADS)]
        s_bufs = (s_even_ref, s_odd_ref)

        def scores(h, j):
            s = _dot(k_ref[k_rows[j], pair_cols(h)], rhs[h])
            if j < n_loc:
                s = s + bias_ref[h, j * kb:(j + 1) * kb, :]
            s_bufs[h % 2][j * kb:(j + 1) * kb] = s
            return jnp.max(s, axis=0, keepdims=True)

        m = functools.reduce(jnp.maximum, [scores(0, j) for j in range(n_all)])
        for h in range(NA_HEADS):
            ol = None
            next_max = []
            for j in range(n_all):
                if h + 1 < NA_HEADS:
                    next_max.append(scores(h + 1, j))
                p = jnp.exp2(s_bufs[h % 2][j * kb:(j + 1) * kb] - m).astype(BF16)
                lhs = jnp.concatenate([vt_ref[v_blocks[j], h * hd:(h + 1) * hd, :], ones], axis=0)
                part = _dot(lhs, p)
                ol = part if ol is None else ol + part
            o_ref[h * hd:(h + 1) * hd, :] = (ol[0:hd] / ol[hd:hd + 1]).astype(BF16)
            if next_max:
                m = functools.reduce(jnp.maximum, next_max)


def _na_tile_classes(grid_rows):
    return ((0, 0), (NA_QROWS, 0), (grid_rows - NA_QROWS, grid_rows - NA_KROWS))


def _na_valid(grid_rows):
    kr = np.arange(NA_KROWS)[:, None, None, None]
    kc = np.arange(GRID_W)[None, :, None, None]
    rq = np.arange(NA_QROWS)[None, None, :, None]
    cq = np.arange(GRID_W)[None, None, None, :]
    oks = []
    for r_a, k_start in _na_tile_classes(grid_rows):
        r0 = np.clip(r_a + rq - WIN_H // 2, 0, grid_rows - WIN_H)
        krow = k_start + kr
        cs = np.clip(cq - WIN_W // 2, 0, GRID_W - WIN_W)
        ok = (krow >= r0) & (krow < r0 + WIN_H) & (kc >= cs) & (kc < cs + WIN_W)
        oks.append(ok.reshape(NA_KROWS * GRID_W, TM))
    return np.stack(oks)


NA_DR_MARGIN = NA_QROWS
NA_DR_ROWS = 2 * WIN_H - 1 + 2 * NA_DR_MARGIN + 1


def _na_bias_kernel(w_ref, mask_ref, out_ref, *, grid_rows):
    lo_half = lax.broadcasted_iota(jnp.int32, (GRID_W, LANES), 1) < GRID_W
    cache = {}

    def toeplitz(d, upper):
        if (d, upper) not in cache:
            row = jnp.broadcast_to(w_ref[d:d + 1, :], (GRID_W, LANES))
            shift = GRID_W + 1 + (GRID_W if upper else 0)
            cache[d, upper] = pltpu.roll(row, shift, 1, stride=1, stride_axis=0)
        return cache[d, upper]

    for cls, (r_a, k_start) in enumerate(_na_tile_classes(grid_rows)):
        for kr in range(NA_KROWS):
            for pair in range(NA_QROWS // 2):
                d = k_start + kr - (r_a + 2 * pair) + (WIN_H - 1) + NA_DR_MARGIN
                block = jnp.where(lo_half, toeplitz(d, False), toeplitz(d - 1, True))
                rows = slice(kr * GRID_W, (kr + 1) * GRID_W)
                cols = slice(pair * LANES, (pair + 1) * LANES)
                out_ref[cls, rows, cols] = block + mask_ref[cls, rows, cols]


def _na_bias_tables(rpb, grid_rows):
    depth, heads, n_dr, n_dc = rpb.shape
    n_keys = NA_KROWS * GRID_W
    first_lane = (GRID_W - 1) - (WIN_W - 1)
    w = jnp.pad(jnp.flip(rpb * LOG2_E, axis=-1),
                ((0, 0), (0, 0), (NA_DR_MARGIN, NA_DR_ROWS - n_dr - NA_DR_MARGIN),
                 (first_lane, LANES - first_lane - n_dc)))
    mask = jnp.asarray(np.where(_na_valid(grid_rows), 0.0, NEG_INF).astype(np.float32))
    return pl.pallas_call(
        functools.partial(_na_bias_kernel, grid_rows=grid_rows),
        grid=(depth, heads),
        in_specs=[pl.BlockSpec((None, None, NA_DR_ROWS, LANES), lambda l, h: (l, h, 0, 0)),
                  pl.BlockSpec((3, n_keys, TM), lambda l, h: (0, 0, 0))],
        out_specs=pl.BlockSpec((None, 3, None, n_keys, TM), lambda l, h: (l, 0, h, 0, 0)),
        out_shape=jax.ShapeDtypeStruct((depth, 3, heads, n_keys, TM), F32),
        compiler_params=pltpu.CompilerParams(dimension_semantics=("parallel", "parallel")),
        name="na_bias",
    )(w, mask)


def _attention(qt, k, vt, bias, grid_rows):
    bsz, s, _ = k.shape
    nt = s // TM
    n_keys = NA_KROWS * GRID_W

    def bias_class(b, i):
        return (jnp.where(i <= 1, 0, jnp.where(i == nt - 1, 2, 1)), 0, 0, 0)

    return pl.pallas_call(
        functools.partial(_attention_kernel, grid_rows=grid_rows),
        grid=(bsz, nt),
        in_specs=[_fm_spec(),
                  pl.BlockSpec((None, s, NA_W), lambda b, i: (b, 0, 0)),
                  pl.BlockSpec((None, nt, NA_W, TM), lambda b, i: (b, 0, 0, 0)),
                  pl.BlockSpec((None, NA_HEADS, n_keys, TM), bias_class)],
        out_specs=_fm_spec(),
        out_shape=jax.ShapeDtypeStruct((bsz, nt, NA_W, TM), BF16),
        scratch_shapes=[pltpu.VMEM((n_keys + CTX_LEN, TM), F32),
                        pltpu.VMEM((n_keys + CTX_LEN, TM), F32)],
        compiler_params=_cparams("parallel", "arbitrary"),
        name="attention",
    )(qt, k, vt, bias)


def _gelu_tanh(x):
    return 0.5 * x * (1.0 + jnp.tanh(math.sqrt(2.0 / math.pi) * (x + 0.044715 * (x * x * x))))


def _merge_kernel(*refs, n_tiles, first):
    c_ref = refs[1] if first else None
    x_ref = refs[0]
    (cg_ref, cprev_ref, cnext_ref, u_ref, yf_ref, yb_ref, o_ref, gate_ref, g1_ref, cw_ref, cout_ref,
     d_ref, glua_ref, glub_ref, naout_ref, wout_ref, out_ref) = refs[2:] if first else refs[1:]
    i = pl.program_id(0)

    def gated_input(ref):
        return ref[:, 2 * CONV_W:3 * CONV_W].astype(F32) * ref[:, 0:CONV_W].astype(F32)

    z = gated_input(cg_ref)
    has_prev = (i >= 2).astype(F32)
    has_next = jnp.logical_and(i != 0, i != n_tiles - 1).astype(F32)
    z_before = gated_input(cprev_ref)[BF16_ROWS - 1:BF16_ROWS] * has_prev
    z_after = gated_input(cnext_ref)[0:1] * has_next
    row = lax.broadcasted_iota(jnp.int32, z.shape, 0)
    z_prev = jnp.where(row == 0, z_before, pltpu.roll(z, 1, 0))
    z_next = jnp.where(row == TM - 1, z_after, pltpu.roll(z, TM - 1, 0))
    conv = z_prev * cw_ref[0:1, :] + z * cw_ref[1:2, :] + z_next * cw_ref[2:3, :]
    a_pre = cg_ref[:, CONV_W:2 * CONV_W].astype(F32) * conv
    ya = _dot(a_pre.astype(BF16), cout_ref[...])

    y = (yf_ref[...].astype(F32) + yb_ref[...].astype(F32)
         + d_ref[...] * u_ref[...].astype(F32))
    g = _gelu_tanh(y).astype(BF16)
    def twice_logistic(half_x):
        return jnp.tanh(half_x) + 1.0

    yb = _dot(g, glua_ref[...]) * twice_logistic(_dot(g, glub_ref[...]))

    yc = lax.dot_general(o_ref[...], naout_ref[...], (((0,), (0,)), ((), ())),
                         preferred_element_type=F32)

    mix2 = (twice_logistic(gate_ref[:, 0:D_MODEL].astype(F32)) * ya
            + twice_logistic(gate_ref[:, D_MODEL:2 * D_MODEL].astype(F32)) * yb
            + twice_logistic(gate_ref[:, 2 * D_MODEL:3 * D_MODEL].astype(F32)) * yc)
    out_ref[...] = _stream_tile(x_ref, c_ref) + g1_ref[...] * _dot(mix2.astype(BF16), wout_ref[...])


def _merge(stream, conv_g, u_tm, yf, yb, attn_o, gates, gate1, conv_w, conv_out, s5_d, glu_a, glu_b,
           na_out, w_out):
    first = len(stream) == 2
    bsz, s, _ = conv_g.shape
    nt = s // TM
    halo_blocks = s // BF16_ROWS
    per_tile = TM // BF16_ROWS
    tok = lambda width: _tok_spec(width, batch_major=False)
    tm_spec = tok(SSM_W)
    prev_spec = pl.BlockSpec((None, BF16_ROWS, 3 * CONV_W),
                             lambda i, b: (b, jnp.maximum(i * per_tile - 1, 0), 0))
    next_spec = pl.BlockSpec((None, BF16_ROWS, 3 * CONV_W),
                             lambda i, b: (b, jnp.minimum((i + 1) * per_tile, halo_blocks - 1), 0))
    return pl.pallas_call(
        functools.partial(_merge_kernel, n_tiles=nt, first=first),
        grid=(nt, bsz),
        in_specs=_stream_specs(first) + [
            tok(3 * CONV_W), prev_spec, next_spec, tm_spec, tm_spec, tm_spec,
            _fm_spec(batch_major=False), tok(3 * D_MODEL), _mod_spec(batch_major=False),
            _const_spec((3, CONV_W)), _const_spec((CONV_W, D_MODEL)), _const_spec((1, SSM_W)),
            _const_spec((SSM_W, D_MODEL)), _const_spec((SSM_W, D_MODEL)),
            _const_spec((NA_W, D_MODEL)), _const_spec((D_MODEL, D_MODEL))],
        out_specs=tok(D_MODEL),
        out_shape=jax.ShapeDtypeStruct((bsz, s, D_MODEL), F32),
        compiler_params=_cparams("parallel", "arbitrary"),
        name="merge",
    )(*stream, conv_g, conv_g, conv_g, u_tm, yf, yb, attn_o, gates, gate1, conv_w, conv_out, s5_d,
      glu_a, glu_b, na_out, w_out)


def _mlp_kernel(x_ref, sh_ref, sc_ref, gt_ref, g_ref, w1_ref, w2_ref, fg_ref, out_ref, *, final):
    x = x_ref[...]
    h = (_rms(x, g_ref[...]) * (1.0 + sc_ref[...]) + sh_ref[...]).astype(BF16)
    acc = jnp.zeros_like(x)
    for c in range(MLP_HIDDEN // D_MODEL):
        cols = slice(c * D_MODEL, (c + 1) * D_MODEL)
        a = jnp.maximum(_dot(h, w1_ref[:, cols]), 0.0)
        acc = acc + _dot((a * a).astype(BF16), w2_ref[cols, :])
    y = x + gt_ref[...] * acc
    if final:
        y = _rms(y, fg_ref[...])
    out_ref[...] = y


def _mlp(xc, shift, scale, gate, g, w1, w2, final_g, final):
    bsz, s, _ = xc.shape
    first = CTX_LEN // TM if final else 0
    nt = s // TM - first
    tok_in = pl.BlockSpec((None, TM, D_MODEL), lambda b, i: (b, i + first, 0))
    mod = pl.BlockSpec((None, None, 1, D_MODEL), lambda b, i: (b, jnp.minimum(i + first, 1), 0, 0))
    return pl.pallas_call(
        functools.partial(_mlp_kernel, final=final),
        grid=(bsz, nt),
        in_specs=[tok_in, mod, mod, mod, _const_spec((1, D_MODEL)),
                  _const_spec((D_MODEL, MLP_HIDDEN)), _const_spec((MLP_HIDDEN, D_MODEL)),
                  _const_spec((1, D_MODEL))],
        out_specs=_tok_spec(D_MODEL),
        out_shape=jax.ShapeDtypeStruct((bsz, nt * TM, D_MODEL), F32),
        compiler_params=_cparams("parallel", "parallel"),
        name="mlp",
    )(xc, shift, scale, gate, g, w1, w2, final_g)


def kernel(x, c, ctx, c_ctx, w_mod, b_mod, norm1_g, w_in, conv_w, conv_out, s5_lam_re, s5_lam_im,
           s5_log_step, s5_b_re, s5_b_im, s5_c_re, s5_c_im, s5_d, s5_glu_a, s5_glu_b, na_rpb,
           na_out, w_out, norm2_g, mlp_w1, mlp_w2, final_norm_g):
    bsz, seq, _ = x.shape
    depth = w_mod.shape[0]
    assert ctx.shape[1] == CTX_LEN == TM and seq % TM == 0 and bsz == SUBLANES
    grid_rows = seq // GRID_W
    assert grid_rows >= NA_KROWS and grid_rows % NA_QROWS == 0
    s = CTX_LEN + seq

    cond_rows = 2 * SUBLANES
    cond = jnp.zeros((cond_rows, D_MODEL), F32).at[:bsz].set(c).at[bsz].set(c_ctx)
    mods = _adaln(cond, w_mod, b_mod).reshape(depth, cond_rows, N_MOD, D_MODEL)
    lat = mods[:, :bsz]
    cx = jnp.broadcast_to(mods[:, bsz:bsz + 1], lat.shape)
    mod = jnp.stack([cx, lat], axis=2)[:, :, :, :, None, :]

    s5_mats = _s5_matrices(s5_lam_re, s5_lam_im, s5_log_step, s5_b_re, s5_b_im, s5_c_re, s5_c_im)

    bf = lambda w: w.astype(BF16)
    gate_cols = (jnp.arange(IN_PROJ_W) >= IN_OFF[7])
    w_in_b = bf(w_in * jnp.where(gate_cols, 0.5, 1.0).astype(F32))
    conv_out_b, glu_a_b, glu_b_b = bf(conv_out), bf(s5_glu_a * 0.5), bf(s5_glu_b * 0.5)
    na_out_b, w_out_b, w1_b, w2_b = bf(na_out), bf(w_out * 0.5), bf(mlp_w1), bf(mlp_w2)
    w_qv_t = _qv_rows(w_in)

    na_bias = _na_bias_tables(na_rpb, grid_rows)

    stream = (x, ctx)
    row = lambda v: v.reshape(1, -1)
    for l in range(depth):
        m = mod[l]
        conv_g, u_tm, u_bm, qt, k, vt, gates = _in_proj(stream, m[:, :, 0], m[:, :, 1],
                                                        row(norm1_g[l]), w_in_b[l], w_qv_t[l])
        yf, yb = _s5_scan(u_tm, *(mat[l] for mat in s5_mats), s)
        attn_o = _attention(qt, k, vt, na_bias[l], grid_rows)
        xc = _merge(stream, conv_g, u_bm, yf, yb, attn_o, gates, m[:, :, 2], conv_w[l],
                    conv_out_b[l], row(s5_d[l]), glu_a_b[l], glu_b_b[l], na_out_b[l], w_out_b[l])
        xc = _mlp(xc, m[:, :, 3], m[:, :, 4], m[:, :, 5], row(norm2_g[l]), w1_b[l], w2_b[l],
                  row(final_norm_g), final=(l == depth - 1))
        stream = (xc,)
    return xc
```

```python
import functools
import math

import numpy as np
import jax
import jax.numpy as jnp
from jax import lax
from jax.experimental import pallas as pl
from jax.experimental.pallas import tpu as pltpu

D_MODEL = 1024
CTX_LEN = 256
GRID_W = 64
N_MOD = 6
CONV_W = 512
SSM_W = 512
SSM_GROUP = 16
SSM_GROUPS = SSM_W // SSM_GROUP
SSM_STATE = 64
NA_HEADS = 8
NA_HEAD_DIM = 64
NA_W = NA_HEADS * NA_HEAD_DIM
WIN_H = 8
WIN_W = 16
MLP_HIDDEN = 4 * D_MODEL
IN_SIZES = (CONV_W, CONV_W, CONV_W, SSM_W, NA_W, NA_W, NA_W, D_MODEL, D_MODEL, D_MODEL)
IN_OFF = tuple(sum(IN_SIZES[:i]) for i in range(len(IN_SIZES) + 1))
IN_PROJ_W = IN_OFF[-1]
RMS_EPS = 1e-6
NEG_INF = -1e30
S5_MIN_DECAY = 1e-4
LOG2_E = math.log2(math.e)

F32 = jnp.float32
BF16 = jnp.bfloat16

LANES = 128
SUBLANES = 8
BF16_ROWS = 16
TM = 256
SCAN_LC = 4
SCAN_T = 128
SCAN_CHUNKS = SCAN_T // SCAN_LC
NA_QROWS = TM // GRID_W
NA_KROWS = NA_QROWS + WIN_H
NA_KEY_BLOCK = TM
SSM_LANE_GROUPS = LANES // SSM_GROUP
SSM_BLOCKS = SSM_W // LANES
SSM_BLOCK_STATES = SSM_LANE_GROUPS * SSM_STATE
SSM_STATES = SSM_GROUPS * SSM_STATE
VMEM_LIMIT = 56 * 1024 * 1024


def _cparams(*sem):
    return pltpu.CompilerParams(dimension_semantics=sem, vmem_limit_bytes=VMEM_LIMIT)


def _rms(x, g):
    return x * lax.rsqrt(jnp.mean(x * x, axis=-1, keepdims=True) + RMS_EPS) * g


def _dot(a, b):
    return jnp.dot(a, b, preferred_element_type=F32)


def _dot_nt(a, b):
    return lax.dot_general(a, b, (((1,), (1,)), ((), ())), preferred_element_type=F32)


def _adaln_kernel(c_ref, w_ref, b_ref, o_ref):
    c = c_ref[...]
    s = c * jax.nn.sigmoid(c)
    o_ref[...] = jnp.dot(s, w_ref[...], preferred_element_type=F32,
                         precision=lax.Precision.HIGHEST) + b_ref[...]


def _adaln(cond, w_mod, b_mod):
    depth = w_mod.shape[0]
    rows = cond.shape[0]
    n_tiles = (N_MOD * D_MODEL) // D_MODEL
    return pl.pallas_call(
        _adaln_kernel,
        grid=(depth, n_tiles),
        in_specs=[
            pl.BlockSpec((rows, D_MODEL), lambda l, j: (0, 0)),
            pl.BlockSpec((None, D_MODEL, D_MODEL), lambda l, j: (l, 0, j)),
            pl.BlockSpec((None, 1, D_MODEL), lambda l, j: (l, 0, j)),
        ],
        out_specs=pl.BlockSpec((None, rows, D_MODEL), lambda l, j: (l, 0, j)),
        out_shape=jax.ShapeDtypeStruct((depth, rows, N_MOD * D_MODEL), F32),
        compiler_params=_cparams("parallel", "parallel"),
        name="adaln",
    )(cond, w_mod, b_mod.reshape(depth, 1, N_MOD * D_MODEL))


def _s5_prep_kernel(lre_ref, lim_ref, ls_ref, bre_ref, bim_ref, cre_ref, cim_ref,
                    are_ref, aim_ref, fre_ref, fim_ref, ore_ref, oim_ref):
    lr = jnp.minimum(lre_ref[...], -S5_MIN_DECAY)
    li = lim_ref[...]
    dt = jnp.exp(ls_ref[...])
    xr = lr * dt
    xi = li * dt

    def power(k):
        e = jnp.exp(k * xr)
        return e * jnp.cos(k * xi), e * jnp.sin(k * xi)

    ar, ai = power(1.0)
    nr = ar - 1.0
    den = lr * lr + li * li
    cr = (nr * lr + ai * li) / den
    ci = (ai * lr - nr * li) / den
    bre = bre_ref[...]
    bim = bim_ref[...]
    bbr = cr * bre - ci * bim
    bbi = cr * bim + ci * bre
    cre = cre_ref[...]
    cim = cim_ref[...]
    fre_ref[:, 0] = bbr
    fim_ref[:, 0] = bbi
    ore_ref[:, 0] = cre
    oim_ref[:, 0] = cim
    for k in range(1, SCAN_LC + 1):
        pr, pi = power(float(k))
        if k < SCAN_LC:
            fre_ref[:, k] = pr * bbr - pi * bbi
            fim_ref[:, k] = pr * bbi + pi * bbr
        ore_ref[:, k] = pr * cre - pi * cim
        oim_ref[:, k] = pr * cim + pi * cre
    are_ref[...], aim_ref[...] = power(float(SCAN_LC))


def _s5_prep(lam_re, lam_im, log_step, b_re, b_im, c_re, c_im):
    depth = lam_re.shape[0]
    rows = depth * 2 * SSM_GROUPS
    rb = SSM_GROUPS
    lre = lam_re.reshape(rows, 1, SSM_STATE)
    lim = lam_im.reshape(rows, 1, SSM_STATE)
    ls = jnp.broadcast_to(log_step.reshape(rows, 1, 1), (rows, 1, SSM_STATE))
    bre = jnp.swapaxes(b_re.reshape(rows, SSM_STATE, SSM_GROUP), 1, 2)
    bim = jnp.swapaxes(b_im.reshape(rows, SSM_STATE, SSM_GROUP), 1, 2)
    cre = c_re.reshape(rows, SSM_GROUP, SSM_STATE)
    cim = c_im.reshape(rows, SSM_GROUP, SSM_STATE)
    vec = pl.BlockSpec((rb, 1, SSM_STATE), lambda r: (r, 0, 0))
    mat = pl.BlockSpec((rb, SSM_GROUP, SSM_STATE), lambda r: (r, 0, 0))
    pw = lambda n: pl.BlockSpec((rb, n, SSM_GROUP, SSM_STATE), lambda r: (r, 0, 0, 0))
    v2 = jax.ShapeDtypeStruct((rows, 1, SSM_STATE), F32)
    v4 = lambda n: jax.ShapeDtypeStruct((rows, n, SSM_GROUP, SSM_STATE), F32)
    return pl.pallas_call(
        _s5_prep_kernel,
        grid=(rows // rb,),
        in_specs=[vec, vec, vec, mat, mat, mat, mat],
        out_specs=[vec, vec, pw(SCAN_LC), pw(SCAN_LC), pw(SCAN_LC + 1), pw(SCAN_LC + 1)],
        out_shape=[v2, v2, v4(SCAN_LC), v4(SCAN_LC), v4(SCAN_LC + 1), v4(SCAN_LC + 1)],
        compiler_params=pltpu.CompilerParams(dimension_semantics=("parallel",)),
        name="s5_prep",
    )(lre, lim, ls, bre, bim, cre, cim)


def _s5_operands_kernel(fre_ref, fim_ref, ore_ref, oim_ref,
                        fold_re_ref, fold_im_ref, out_re_ref, out_im_ref, toe_ref, bd_ref):
    lc = SCAN_LC
    hi = lax.Precision.HIGHEST
    forward = pl.program_id(0) % 2 == 0
    bd_ref[...] = jnp.zeros_like(bd_ref)

    def block_diag(x):
        for g in range(SSM_LANE_GROUPS):
            bd_ref[g * SSM_GROUP:(g + 1) * SSM_GROUP, g * SSM_STATE:(g + 1) * SSM_STATE] = x[g]
        return bd_ref[...]

    for j in range(lc):
        for f_ref, fold_ref in ((fre_ref, fold_re_ref), (fim_ref, fold_im_ref)):
            f = jnp.where(forward, f_ref[:, lc - 1 - j], f_ref[:, j])
            fold_ref[j * LANES:(j + 1) * LANES, :] = block_diag(f).astype(BF16)

    oc_re = [block_diag(ore_ref[:, k]).T for k in range(lc + 1)]
    oc_im = [block_diag(oim_ref[:, k]).T for k in range(lc + 1)]
    for j in range(lc):
        cols = slice(j * LANES, (j + 1) * LANES)
        out_re_ref[:, cols] = jnp.where(forward, oc_re[j + 1], oc_re[lc - j]).astype(BF16)
        out_im_ref[:, cols] = jnp.where(forward, oc_im[j + 1], oc_im[lc - j]).astype(BF16)

    bd_re = block_diag(fre_ref[:, 0])
    bd_im = block_diag(fim_ref[:, 0])
    kk = [jnp.dot(bd_re, oc_re[k], preferred_element_type=F32, precision=hi)
          - jnp.dot(bd_im, oc_im[k], preferred_element_type=F32, precision=hi) for k in range(lc)]
    zero = jnp.zeros((LANES, LANES), F32)
    for a in range(lc):
        for b in range(lc):
            fwd = kk[b - a] if b >= a else zero
            bwd = kk[a - b] if a >= b else zero
            toe_ref[a * LANES:(a + 1) * LANES, b * LANES:(b + 1) * LANES] = (
                jnp.where(forward, fwd, bwd).astype(BF16))


def _s5_operands(f_re, f_im, o_re, o_im, depth):
    lc = SCAN_LC
    ng = SSM_LANE_GROUPS
    group_block = lambda n: pl.BlockSpec((ng, n, SSM_GROUP, SSM_STATE),
                                         lambda ld, q: (ld * SSM_BLOCKS + q, 0, 0, 0))
    out = lambda r, c: pl.BlockSpec((None, None, r, c), lambda ld, q: (ld, q, 0, 0))
    sds = lambda r, c: jax.ShapeDtypeStruct((2 * depth, SSM_BLOCKS, r, c), BF16)
    fold_shape = (lc * LANES, SSM_BLOCK_STATES)
    out_shape = (SSM_BLOCK_STATES, lc * LANES)
    toe_shape = (lc * LANES, lc * LANES)
    results = pl.pallas_call(
        _s5_operands_kernel,
        grid=(2 * depth, SSM_BLOCKS),
        in_specs=[group_block(lc), group_block(lc), group_block(lc + 1), group_block(lc + 1)],
        out_specs=[out(*fold_shape), out(*fold_shape), out(*out_shape), out(*out_shape),
                   out(*toe_shape)],
        out_shape=[sds(*fold_shape), sds(*fold_shape), sds(*out_shape), sds(*out_shape),
                   sds(*toe_shape)],
        scratch_shapes=[pltpu.VMEM((LANES, SSM_BLOCK_STATES), F32)],
        compiler_params=pltpu.CompilerParams(dimension_semantics=("parallel", "parallel")),
        name="s5_operands",
    )(f_re, f_im, o_re, o_im)
    return [r.reshape((depth, 2) + r.shape[1:]) for r in results]


def _s5_matrices(lam_re, lam_im, log_step, b_re, b_im, c_re, c_im):
    depth = lam_re.shape[0]
    a_re, a_im, f_re, f_im, o_re, o_im = _s5_prep(lam_re, lam_im, log_step, b_re, b_im, c_re, c_im)
    fold_re, fold_im, out_re, out_im, toe = _s5_operands(f_re, f_im, o_re, o_im, depth)
    a_shape = (depth, 2, 1, SSM_STATES)
    a_re = jnp.broadcast_to(a_re.reshape(a_shape), (depth, 2, SUBLANES, SSM_STATES))
    a_im = jnp.broadcast_to(a_im.reshape(a_shape), (depth, 2, SUBLANES, SSM_STATES))
    return fold_re, fold_im, out_re, out_im, toe, a_re, a_im


def _batch_rows(b):
    return pl.ds(b, TM, stride=SUBLANES)


def _qv_rows_kernel(w_ref, o_ref):
    o_ref[...] = w_ref[...].T.astype(BF16)


def _qv_rows(w_in):
    depth = w_in.shape[0]
    col_blocks = (IN_OFF[4] // NA_W, IN_OFF[6] // NA_W)
    assert IN_OFF[4] % NA_W == 0 and IN_OFF[6] % NA_W == 0
    return pl.pallas_call(
        _qv_rows_kernel,
        grid=(depth, 2),
        in_specs=[pl.BlockSpec((None, D_MODEL, NA_W),
                               lambda l, j: (l, 0, jnp.where(j == 0, col_blocks[0], col_blocks[1])))],
        out_specs=pl.BlockSpec((None, NA_W, D_MODEL), lambda l, j: (l, j, 0)),
        out_shape=jax.ShapeDtypeStruct((depth, 2 * NA_W, D_MODEL), BF16),
        compiler_params=pltpu.CompilerParams(dimension_semantics=("parallel", "parallel")),
        name="qv_rows",
    )(w_in)


def _stream_tile(x_ref, c_ref):
    if c_ref is None:
        return x_ref[...]
    return jnp.where(pl.program_id(0) < CTX_LEN // TM, c_ref[...], x_ref[...])


def _in_proj_kernel(*refs, first):
    c_ref = refs[1] if first else None
    x_ref = refs[0]
    (sh_ref, sc_ref, g_ref, w_ref, wqv_ref, conv_ref, u_ref, ub_ref, qt_ref, k_ref, vt_ref,
     gate_ref) = refs[2:] if first else refs[1:]
    h = _rms(_stream_tile(x_ref, c_ref), g_ref[...]) * (1.0 + sc_ref[...]) + sh_ref[...]
    hb = h.astype(BF16)
    conv_ref[...] = _dot(hb, w_ref[:, IN_OFF[0]:IN_OFF[3]]).astype(BF16)
    u = _dot(hb, w_ref[:, IN_OFF[3]:IN_OFF[4]])
    rows = _batch_rows(pl.program_id(1))
    for q in range(SSM_BLOCKS):
        u_ref[q, rows, :] = u[:, q * LANES:(q + 1) * LANES]
    ub_ref[...] = u.astype(BF16)
    qt_ref[...] = _dot_nt(wqv_ref[0:NA_W, :], hb).astype(BF16)
    k_ref[...] = _dot(hb, w_ref[:, IN_OFF[5]:IN_OFF[6]]).astype(BF16)
    vt_ref[...] = _dot_nt(wqv_ref[NA_W:2 * NA_W, :], hb).astype(BF16)
    gate_ref[...] = _dot(hb, w_ref[:, IN_OFF[7]:IN_OFF[10]]).astype(BF16)


def _grid_bi(batch_major):
    return (lambda b, i: (b, i)) if batch_major else (lambda i, b: (b, i))


def _tok_spec(width, batch_major=True):
    bi = _grid_bi(batch_major)
    return pl.BlockSpec((None, TM, width), lambda *g: bi(*g) + (0,))


def _mod_spec(batch_major=True):
    bi = _grid_bi(batch_major)

    def index(*g):
        b, i = bi(*g)
        return (b, jnp.minimum(i, 1), 0, 0)

    return pl.BlockSpec((None, None, 1, D_MODEL), index)


def _const_spec(shape):
    nd = len(shape)
    return pl.BlockSpec(shape, lambda *g: (0,) * nd)


def _layer_spec(stacked, layer, **kwargs):
    shape = stacked.shape[1:]
    return pl.BlockSpec((None,) + shape, lambda *g: (layer,) + (0,) * len(shape), **kwargs)


def _slab_spec():
    return pl.BlockSpec((SSM_BLOCKS, TM * SUBLANES, LANES), lambda i, b: (0, i, 0))


def _fm_spec(batch_major=True):
    bi = _grid_bi(batch_major)
    return pl.BlockSpec((None, None, NA_W, TM), lambda *g: bi(*g) + (0, 0))


def _stream_specs(first):
    if not first:
        return [_tok_spec(D_MODEL, batch_major=False)]
    nc = CTX_LEN // TM
    return [pl.BlockSpec((None, TM, D_MODEL), lambda i, b: (b, jnp.maximum(i - nc, 0), 0)),
            pl.BlockSpec((None, TM, D_MODEL), lambda i, b: (b, jnp.minimum(i, nc - 1), 0))]


def _in_proj(stream, shift, scale, g, w, w_qv_t, layer):
    first = len(stream) == 2
    bsz = stream[0].shape[0]
    s = stream[0].shape[1] + (CTX_LEN if first else 0)
    nt = s // TM
    tok = lambda width: _tok_spec(width, batch_major=False)
    mod = _mod_spec(batch_major=False)
    fm = _fm_spec(batch_major=False)
    fm_sds = jax.ShapeDtypeStruct((bsz, nt, NA_W, TM), BF16)
    return pl.pallas_call(
        functools.partial(_in_proj_kernel, first=first),
        grid=(nt, bsz),
        in_specs=_stream_specs(first) + [mod, mod, _const_spec((1, D_MODEL)),
                                         _layer_spec(w, layer), _layer_spec(w_qv_t, layer)],
        out_specs=[tok(3 * CONV_W), _slab_spec(), tok(SSM_W), fm, tok(NA_W), fm, tok(3 * D_MODEL)],
        out_shape=[jax.ShapeDtypeStruct((bsz, s, 3 * CONV_W), BF16),
                   jax.ShapeDtypeStruct((SSM_BLOCKS, s * bsz, LANES), F32),
                   jax.ShapeDtypeStruct((bsz, s, SSM_W), BF16),
                   fm_sds,
                   jax.ShapeDtypeStruct((bsz, s, NA_W), BF16),
                   fm_sds,
                   jax.ShapeDtypeStruct((bsz, s, 3 * D_MODEL), BF16)],
        compiler_params=_cparams("parallel", "arbitrary"),
        name="in_proj",
    )(*stream, shift, scale, g, w, w_qv_t)


def _s5_scan_kernel(uf_ref, ub_ref, fre_ref, fim_ref, ore_ref, oim_ref, toe_ref, are_ref, aim_ref,
                    yf_ref, yb_ref, sre_ref, sim_ref, hre_ref, him_ref, ytoe_ref, yflat_ref):
    @pl.when(pl.program_id(0) == 0)
    def _():
        hre_ref[...] = jnp.zeros_like(hre_ref)
        him_ref[...] = jnp.zeros_like(him_ref)

    bs = SSM_BLOCK_STATES
    rows_c = SCAN_CHUNKS * SUBLANES

    for d, (u_ref, y_ref) in enumerate(((uf_ref, yf_ref), (ub_ref, yb_ref))):
        for q in range(SSM_BLOCKS):
            uc = jnp.concatenate([u_ref[q, :, j].reshape(rows_c, LANES) for j in range(SCAN_LC)],
                                 axis=1).astype(BF16)
            sre_ref[d, :, q * bs:(q + 1) * bs] = _dot(uc, fre_ref[d, q])
            sim_ref[d, :, q * bs:(q + 1) * bs] = _dot(uc, fim_ref[d, q])
            ytoe_ref[q] = _dot(uc, toe_ref[d, q])

        for q in range(SSM_BLOCKS):
            cols = slice(q * bs, (q + 1) * bs)
            ar = are_ref[d, :, cols]
            ai = aim_ref[d, :, cols]

            def step(c, carry, d=d, cols=cols, ar=ar, ai=ai):
                hr, hi = carry
                cc = c if d == 0 else SCAN_CHUNKS - 1 - c
                rows = pl.ds(pl.multiple_of(cc * SUBLANES, SUBLANES), SUBLANES)
                inc_r = sre_ref[d, rows, cols]
                inc_i = sim_ref[d, rows, cols]
                sre_ref[d, rows, cols] = hr
                sim_ref[d, rows, cols] = hi
                return ar * hr - ai * hi + inc_r, ar * hi + ai * hr + inc_i

            hr, hi = lax.fori_loop(0, SCAN_CHUNKS, step, (hre_ref[d, :, cols], him_ref[d, :, cols]),
                                   unroll=4)
            hre_ref[d, :, cols] = hr
            him_ref[d, :, cols] = hi

        for q in range(SSM_BLOCKS):
            cols = slice(q * bs, (q + 1) * bs)
            y = (ytoe_ref[q] + _dot(sre_ref[d, :, cols].astype(BF16), ore_ref[d, q])
                 - _dot(sim_ref[d, :, cols].astype(BF16), oim_ref[d, q]))
            for c in range(SCAN_CHUNKS):
                for j in range(SCAN_LC):
                    yflat_ref[q, pl.ds((c * SCAN_LC + j) * SUBLANES, SUBLANES), :] = (
                        y[c * SUBLANES:(c + 1) * SUBLANES, j * LANES:(j + 1) * LANES])

        for b in range(SUBLANES):
            for q in range(SSM_BLOCKS):
                y_ref[b, :, q * LANES:(q + 1) * LANES] = (
                    yflat_ref[q, pl.ds(b, SCAN_T, stride=SUBLANES), :].astype(BF16))


def _s5_scan(u_tm, f_re, f_im, o_re, o_im, toe, a_re, a_im, seq, layer):
    rows_total = u_tm.shape[1]
    bsz = rows_total // seq
    assert bsz == SUBLANES and SCAN_T % SCAN_LC == 0 and CTX_LEN % SCAN_T == 0
    nt = seq // SCAN_T
    nc = CTX_LEN // SCAN_T
    chunked = (SSM_BLOCKS, seq // SCAN_LC, SCAN_LC, bsz, LANES)
    block = (SSM_BLOCKS, SCAN_CHUNKS, SCAN_LC, bsz, LANES)

    def bwd_tile(i):
        return jnp.where(i < nc, nc - 1 - i, nt - 1 - (i - nc))

    fwd_spec = pl.BlockSpec(block, lambda i: (0, i, 0, 0, 0))
    bwd_spec = pl.BlockSpec(block, lambda i: (0, bwd_tile(i), 0, 0, 0))
    full = lambda a: _layer_spec(a, layer, pipeline_mode=pl.Buffered(1))
    y_sds = jax.ShapeDtypeStruct((bsz, seq, SSM_W), BF16)
    y_block = (bsz, SCAN_T, SSM_W)
    u5 = u_tm.reshape(chunked)
    return pl.pallas_call(
        _s5_scan_kernel,
        grid=(nt,),
        in_specs=[fwd_spec, bwd_spec, full(f_re), full(f_im), full(o_re), full(o_im), full(toe),
                  full(a_re), full(a_im)],
        out_specs=[pl.BlockSpec(y_block, lambda i: (0, i, 0)),
                   pl.BlockSpec(y_block, lambda i: (0, bwd_tile(i), 0))],
        out_shape=[y_sds, y_sds],
        scratch_shapes=[pltpu.VMEM((2, SCAN_CHUNKS * bsz, SSM_STATES), F32),
                        pltpu.VMEM((2, SCAN_CHUNKS * bsz, SSM_STATES), F32),
                        pltpu.VMEM((2, bsz, SSM_STATES), F32),
                        pltpu.VMEM((2, bsz, SSM_STATES), F32),
                        pltpu.VMEM((SSM_BLOCKS, SCAN_CHUNKS * bsz, SCAN_LC * LANES), F32),
                        pltpu.VMEM((SSM_BLOCKS, SCAN_T * bsz, LANES), F32)],
        compiler_params=_cparams("arbitrary"),
        name="s5_scan",
    )(u5, u5, f_re, f_im, o_re, o_im, toe, a_re, a_im)


def _attention_kernel(qt_ref, k_ref, vt_ref, bias_ref, o_ref, s_even_ref, s_odd_ref, *, grid_rows):
    i = pl.program_id(1)
    hd = NA_HEAD_DIM
    qt = (qt_ref[...].astype(F32) * (hd ** -0.5 * LOG2_E)).astype(BF16)
    zeros = jnp.zeros((hd, TM), BF16)
    ones = jnp.ones((BF16_ROWS, NA_KEY_BLOCK), BF16)
    ctx_rows = slice(0, CTX_LEN)

    def pair_cols(h):
        return slice((h // 2) * LANES, (h // 2 + 1) * LANES)

    def padded_qt(h):
        qh = qt[h * hd:(h + 1) * hd]
        return jnp.concatenate([qh, zeros] if h % 2 == 0 else [zeros, qh], axis=0)

    @pl.when(i == 0)
    def _():
        for h in range(NA_HEADS):
            s = _dot(k_ref[ctx_rows, pair_cols(h)], padded_qt(h))
            p = jnp.exp2(s - jnp.max(s, axis=0, keepdims=True)).astype(BF16)
            ol = _dot(jnp.concatenate([vt_ref[0, h * hd:(h + 1) * hd, :], ones], axis=0), p)
            o_ref[h * hd:(h + 1) * hd, :] = (ol[0:hd] / ol[hd:hd + 1]).astype(BF16)

    @pl.when(i > 0)
    def _():
        r_a = (i - 1) * NA_QROWS
        k_start = jnp.clip(r_a - WIN_H // 2, 0, grid_rows - NA_KROWS)
        start = pl.multiple_of(CTX_LEN + k_start * GRID_W, NA_KEY_BLOCK)
        kb = NA_KEY_BLOCK
        blk0 = start // kb
        n_loc = NA_KROWS * GRID_W // kb
        n_all = n_loc + 1
        k_rows = [pl.ds(start + j * kb, kb) for j in range(n_loc)] + [ctx_rows]
        v_blocks = [blk0 + j for j in range(n_loc)] + [0]
        rhs = [padded_qt(h) for h in range(NA_HEADS)]
        s_bufs = (s_even_ref, s_odd_ref)

        def scores(h, j):
            s = _dot(k_ref[k_rows[j], pair_cols(h)], rhs[h])
            if j < n_loc:
                s = s + bias_ref[h, j * kb:(j + 1) * kb, :]
            s_bufs[h % 2][j * kb:(j + 1) * kb] = s
            return jnp.max(s, axis=0, keepdims=True)

        m = functools.reduce(jnp.maximum, [scores(0, j) for j in range(n_all)])
        for h in range(NA_HEADS):
            ol = None
            next_max = []
            for j in range(n_all):
                if h + 1 < NA_HEADS:
                    next_max.append(scores(h + 1, j))
                p = jnp.exp2(s_bufs[h % 2][j * kb:(j + 1) * kb] - m).astype(BF16)
                lhs = jnp.concatenate([vt_ref[v_blocks[j], h * hd:(h + 1) * hd, :], ones], axis=0)
                part = _dot(lhs, p)
                ol = part if ol is None else ol + part
            o_ref[h * hd:(h + 1) * hd, :] = (ol[0:hd] / ol[hd:hd + 1]).astype(BF16)
            if next_max:
                m = functools.reduce(jnp.maximum, next_max)


def _na_tile_classes(grid_rows):
    return ((0, 0), (NA_QROWS, 0), (grid_rows - NA_QROWS, grid_rows - NA_KROWS))


def _na_valid(grid_rows):
    kr = np.arange(NA_KROWS)[:, None, None, None]
    kc = np.arange(GRID_W)[None, :, None, None]
    rq = np.arange(NA_QROWS)[None, None, :, None]
    cq = np.arange(GRID_W)[None, None, None, :]
    oks = []
    for r_a, k_start in _na_tile_classes(grid_rows):
        r0 = np.clip(r_a + rq - WIN_H // 2, 0, grid_rows - WIN_H)
        krow = k_start + kr
        cs = np.clip(cq - WIN_W // 2, 0, GRID_W - WIN_W)
        ok = (krow >= r0) & (krow < r0 + WIN_H) & (kc >= cs) & (kc < cs + WIN_W)
        oks.append(ok.reshape(NA_KROWS * GRID_W, TM))
    return np.stack(oks)


NA_DR_MARGIN = NA_QROWS
NA_DR_ROWS = 2 * WIN_H - 1 + 2 * NA_DR_MARGIN + 1


def _na_bias_kernel(w_ref, mask_ref, out_ref, *, grid_rows):
    lo_half = lax.broadcasted_iota(jnp.int32, (GRID_W, LANES), 1) < GRID_W
    cache = {}

    def toeplitz(d, upper):
        if (d, upper) not in cache:
            row = jnp.broadcast_to(w_ref[d:d + 1, :], (GRID_W, LANES))
            shift = GRID_W + 1 + (GRID_W if upper else 0)
            cache[d, upper] = pltpu.roll(row, shift, 1, stride=1, stride_axis=0)
        return cache[d, upper]

    for cls, (r_a, k_start) in enumerate(_na_tile_classes(grid_rows)):
        for kr in range(NA_KROWS):
            for pair in range(NA_QROWS // 2):
                d = k_start + kr - (r_a + 2 * pair) + (WIN_H - 1) + NA_DR_MARGIN
                block = jnp.where(lo_half, toeplitz(d, False), toeplitz(d - 1, True))
                rows = slice(kr * GRID_W, (kr + 1) * GRID_W)
                cols = slice(pair * LANES, (pair + 1) * LANES)
                out_ref[cls, rows, cols] = block + mask_ref[cls, rows, cols]


def _na_bias_tables(rpb, grid_rows):
    depth, heads, n_dr, n_dc = rpb.shape
    n_keys = NA_KROWS * GRID_W
    first_lane = (GRID_W - 1) - (WIN_W - 1)
    w = jnp.pad(jnp.flip(rpb * LOG2_E, axis=-1),
                ((0, 0), (0, 0), (NA_DR_MARGIN, NA_DR_ROWS - n_dr - NA_DR_MARGIN),
                 (first_lane, LANES - first_lane - n_dc)))
    mask = jnp.asarray(np.where(_na_valid(grid_rows), 0.0, NEG_INF).astype(np.float32))
    return pl.pallas_call(
        functools.partial(_na_bias_kernel, grid_rows=grid_rows),
        grid=(depth, heads),
        in_specs=[pl.BlockSpec((None, None, NA_DR_ROWS, LANES), lambda l, h: (l, h, 0, 0)),
                  pl.BlockSpec((3, n_keys, TM), lambda l, h: (0, 0, 0))],
        out_specs=pl.BlockSpec((None, 3, None, n_keys, TM), lambda l, h: (l, 0, h, 0, 0)),
        out_shape=jax.ShapeDtypeStruct((depth, 3, heads, n_keys, TM), F32),
        compiler_params=pltpu.CompilerParams(dimension_semantics=("parallel", "parallel")),
        name="na_bias",
    )(w, mask)


def _attention(qt, k, vt, bias, grid_rows, layer):
    bsz, s, _ = k.shape
    nt = s // TM
    n_keys = NA_KROWS * GRID_W

    def bias_class(b, i):
        return (layer, jnp.where(i <= 1, 0, jnp.where(i == nt - 1, 2, 1)), 0, 0, 0)

    return pl.pallas_call(
        functools.partial(_attention_kernel, grid_rows=grid_rows),
        grid=(bsz, nt),
        in_specs=[_fm_spec(),
                  pl.BlockSpec((None, s, NA_W), lambda b, i: (b, 0, 0)),
                  pl.BlockSpec((None, nt, NA_W, TM), lambda b, i: (b, 0, 0, 0)),
                  pl.BlockSpec((None, None, NA_HEADS, n_keys, TM), bias_class)],
        out_specs=_fm_spec(),
        out_shape=jax.ShapeDtypeStruct((bsz, nt, NA_W, TM), BF16),
        scratch_shapes=[pltpu.VMEM((n_keys + CTX_LEN, TM), F32),
                        pltpu.VMEM((n_keys + CTX_LEN, TM), F32)],
        compiler_params=_cparams("parallel", "arbitrary"),
        name="attention",
    )(qt, k, vt, bias)


def _gelu_tanh(x):
    return 0.5 * x * (1.0 + jnp.tanh(math.sqrt(2.0 / math.pi) * (x + 0.044715 * (x * x * x))))


def _merge_kernel(*refs, n_tiles, first):
    c_ref = refs[1] if first else None
    x_ref = refs[0]
    (cg_ref, cprev_ref, cnext_ref, u_ref, yf_ref, yb_ref, o_ref, gate_ref, g1_ref, cw_ref, cout_ref,
     d_ref, glua_ref, glub_ref, naout_ref, wout_ref, out_ref) = refs[2:] if first else refs[1:]
    i = pl.program_id(0)

    def gated_input(ref):
        return ref[:, 2 * CONV_W:3 * CONV_W].astype(F32) * ref[:, 0:CONV_W].astype(F32)

    z = gated_input(cg_ref)
    has_prev = (i >= 2).astype(F32)
    has_next = jnp.logical_and(i != 0, i != n_tiles - 1).astype(F32)
    z_before = gated_input(cprev_ref)[BF16_ROWS - 1:BF16_ROWS] * has_prev
    z_after = gated_input(cnext_ref)[0:1] * has_next
    row = lax.broadcasted_iota(jnp.int32, z.shape, 0)
    z_prev = jnp.where(row == 0, z_before, pltpu.roll(z, 1, 0))
    z_next = jnp.where(row == TM - 1, z_after, pltpu.roll(z, TM - 1, 0))
    conv = z_prev * cw_ref[0:1, :] + z * cw_ref[1:2, :] + z_next * cw_ref[2:3, :]
    a_pre = cg_ref[:, CONV_W:2 * CONV_W].astype(F32) * conv
    ya = _dot(a_pre.astype(BF16), cout_ref[...])

    y = (yf_ref[...].astype(F32) + yb_ref[...].astype(F32)
         + d_ref[...] * u_ref[...].astype(F32))
    g = _gelu_tanh(y).astype(BF16)
    def twice_logistic(half_x):
        return jnp.tanh(half_x) + 1.0

    yb = _dot(g, glua_ref[...]) * twice_logistic(_dot(g, glub_ref[...]))

    yc = lax.dot_general(o_ref[...], naout_ref[...], (((0,), (0,)), ((), ())),
                         preferred_element_type=F32)

    mix2 = (twice_logistic(gate_ref[:, 0:D_MODEL].astype(F32)) * ya
            + twice_logistic(gate_ref[:, D_MODEL:2 * D_MODEL].astype(F32)) * yb
            + twice_logistic(gate_ref[:, 2 * D_MODEL:3 * D_MODEL].astype(F32)) * yc)
    out_ref[...] = _stream_tile(x_ref, c_ref) + g1_ref[...] * _dot(mix2.astype(BF16), wout_ref[...])


def _merge(stream, conv_g, u_tm, yf, yb, attn_o, gates, gate1, conv_w, conv_out, s5_d, glu_a, glu_b,
           na_out, w_out, layer):
    first = len(stream) == 2
    bsz, s, _ = conv_g.shape
    nt = s // TM
    halo_blocks = s // BF16_ROWS
    per_tile = TM // BF16_ROWS
    tok = lambda width: _tok_spec(width, batch_major=False)
    tm_spec = tok(SSM_W)
    prev_spec = pl.BlockSpec((None, BF16_ROWS, 3 * CONV_W),
                             lambda i, b: (b, jnp.maximum(i * per_tile - 1, 0), 0))
    next_spec = pl.BlockSpec((None, BF16_ROWS, 3 * CONV_W),
                             lambda i, b: (b, jnp.minimum((i + 1) * per_tile, halo_blocks - 1), 0))
    return pl.pallas_call(
        functools.partial(_merge_kernel, n_tiles=nt, first=first),
        grid=(nt, bsz),
        in_specs=_stream_specs(first) + [
            tok(3 * CONV_W), prev_spec, next_spec, tm_spec, tm_spec, tm_spec,
            _fm_spec(batch_major=False), tok(3 * D_MODEL), _mod_spec(batch_major=False),
            _const_spec((3, CONV_W)), _layer_spec(conv_out, layer), _const_spec((1, SSM_W)),
            _layer_spec(glu_a, layer), _layer_spec(glu_b, layer),
            _layer_spec(na_out, layer), _layer_spec(w_out, layer)],
        out_specs=tok(D_MODEL),
        out_shape=jax.ShapeDtypeStruct((bsz, s, D_MODEL), F32),
        compiler_params=_cparams("parallel", "arbitrary"),
        name="merge",
    )(*stream, conv_g, conv_g, conv_g, u_tm, yf, yb, attn_o, gates, gate1, conv_w, conv_out, s5_d,
      glu_a, glu_b, na_out, w_out)


def _mlp_kernel(x_ref, sh_ref, sc_ref, gt_ref, g_ref, w1_ref, w2_ref, fg_ref, out_ref, *, final):
    x = x_ref[...]
    h = (_rms(x, g_ref[...]) * (1.0 + sc_ref[...]) + sh_ref[...]).astype(BF16)
    acc = jnp.zeros_like(x)
    for c in range(MLP_HIDDEN // D_MODEL):
        cols = slice(c * D_MODEL, (c + 1) * D_MODEL)
        a = jnp.maximum(_dot(h, w1_ref[:, cols]), 0.0)
        acc = acc + _dot((a * a).astype(BF16), w2_ref[cols, :])
    y = x + gt_ref[...] * acc
    if final:
        y = _rms(y, fg_ref[...])
    out_ref[...] = y


def _mlp(xc, shift, scale, gate, g, w1, w2, final_g, final, layer):
    bsz, s, _ = xc.shape
    first = CTX_LEN // TM if final else 0
    nt = s // TM - first
    tok_in = pl.BlockSpec((None, TM, D_MODEL), lambda b, i: (b, i + first, 0))
    mod = pl.BlockSpec((None, None, 1, D_MODEL), lambda b, i: (b, jnp.minimum(i + first, 1), 0, 0))
    return pl.pallas_call(
        functools.partial(_mlp_kernel, final=final),
        grid=(bsz, nt),
        in_specs=[tok_in, mod, mod, mod, _const_spec((1, D_MODEL)),
                  _layer_spec(w1, layer), _layer_spec(w2, layer), _const_spec((1, D_MODEL))],
        out_specs=_tok_spec(D_MODEL),
        out_shape=jax.ShapeDtypeStruct((bsz, nt * TM, D_MODEL), F32),
        compiler_params=_cparams("parallel", "parallel"),
        name="mlp",
    )(xc, shift, scale, gate, g, w1, w2, final_g)


def kernel(x, c, ctx, c_ctx, w_mod, b_mod, norm1_g, w_in, conv_w, conv_out, s5_lam_re, s5_lam_im,
           s5_log_step, s5_b_re, s5_b_im, s5_c_re, s5_c_im, s5_d, s5_glu_a, s5_glu_b, na_rpb,
           na_out, w_out, norm2_g, mlp_w1, mlp_w2, final_norm_g):
    bsz, seq, _ = x.shape
    depth = w_mod.shape[0]
    assert ctx.shape[1] == CTX_LEN == TM and seq % TM == 0 and bsz == SUBLANES
    grid_rows = seq // GRID_W
    assert grid_rows >= NA_KROWS and grid_rows % NA_QROWS == 0
    s = CTX_LEN + seq

    cond_rows = 2 * SUBLANES
    cond = jnp.zeros((cond_rows, D_MODEL), F32).at[:bsz].set(c).at[bsz].set(c_ctx)
    mods = _adaln(cond, w_mod, b_mod).reshape(depth, cond_rows, N_MOD, D_MODEL)
    lat = mods[:, :bsz]
    cx = jnp.broadcast_to(mods[:, bsz:bsz + 1], lat.shape)
    mod = jnp.stack([cx, lat], axis=2)[:, :, :, :, None, :]

    s5_mats = _s5_matrices(s5_lam_re, s5_lam_im, s5_log_step, s5_b_re, s5_b_im, s5_c_re, s5_c_im)

    bf = lambda w: w.astype(BF16)
    gate_cols = (jnp.arange(IN_PROJ_W) >= IN_OFF[7])
    w_in_b = bf(w_in * jnp.where(gate_cols, 0.5, 1.0).astype(F32))
    conv_out_b, glu_a_b, glu_b_b = bf(conv_out), bf(s5_glu_a * 0.5), bf(s5_glu_b * 0.5)
    na_out_b, w_out_b, w1_b, w2_b = bf(na_out), bf(w_out * 0.5), bf(mlp_w1), bf(mlp_w2)
    w_qv_t = _qv_rows(w_in)

    na_bias = _na_bias_tables(na_rpb, grid_rows)

    stream = (x, ctx)
    row = lambda v: v.reshape(1, -1)
    for l in range(depth):
        m = mod[l]
        conv_g, u_tm, u_bm, qt, k, vt, gates = _in_proj(stream, m[:, :, 0], m[:, :, 1],
                                                        row(norm1_g[l]), w_in_b, w_qv_t, l)
        yf, yb = _s5_scan(u_tm, *s5_mats, s, l)
        attn_o = _attention(qt, k, vt, na_bias, grid_rows, l)
        xc = _merge(stream, conv_g, u_bm, yf, yb, attn_o, gates, m[:, :, 2], conv_w[l],
                    conv_out_b, row(s5_d[l]), glu_a_b, glu_b_b, na_out_b, w_out_b, l)
        xc = _mlp(xc, m[:, :, 3], m[:, :, 4], m[:, :, 5], row(norm2_g[l]), w1_b, w2_b,
                  row(final_norm_g), final=(l == depth - 1), layer=l)
        stream = (xc,)
    return xc
```

```python
import functools
import math

import numpy as np
import jax
import jax.numpy as jnp
from jax import lax
from jax.experimental import pallas as pl
from jax.experimental.pallas import tpu as pltpu

D_MODEL = 1024
CTX_LEN = 256
GRID_W = 64
N_MOD = 6
CONV_W = 512
SSM_W = 512
SSM_GROUP = 16
SSM_GROUPS = SSM_W // SSM_GROUP
SSM_STATE = 64
NA_HEADS = 8
NA_HEAD_DIM = 64
NA_W = NA_HEADS * NA_HEAD_DIM
WIN_H = 8
WIN_W = 16
MLP_HIDDEN = 4 * D_MODEL
IN_SIZES = (CONV_W, CONV_W, CONV_W, SSM_W, NA_W, NA_W, NA_W, D_MODEL, D_MODEL, D_MODEL)
IN_OFF = tuple(sum(IN_SIZES[:i]) for i in range(len(IN_SIZES) + 1))
IN_PROJ_W = IN_OFF[-1]
RMS_EPS = 1e-6
NEG_INF = -1e30
S5_MIN_DECAY = 1e-4
LOG2_E = math.log2(math.e)

F32 = jnp.float32
BF16 = jnp.bfloat16

LANES = 128
SUBLANES = 8
BF16_ROWS = 16
TM = 256
SCAN_LC = 4
SCAN_T = 128
SCAN_CHUNKS = SCAN_T // SCAN_LC
NA_QROWS = TM // GRID_W
NA_KROWS = NA_QROWS + WIN_H
NA_KEY_BLOCK = TM
SSM_LANE_GROUPS = LANES // SSM_GROUP
SSM_BLOCKS = SSM_W // LANES
SSM_BLOCK_STATES = SSM_LANE_GROUPS * SSM_STATE
SSM_STATES = SSM_GROUPS * SSM_STATE
VMEM_LIMIT = 56 * 1024 * 1024


def _cparams(*sem):
    return pltpu.CompilerParams(dimension_semantics=sem, vmem_limit_bytes=VMEM_LIMIT)


def _rms(x, g):
    return x * lax.rsqrt(jnp.mean(x * x, axis=-1, keepdims=True) + RMS_EPS) * g


def _dot(a, b):
    return jnp.dot(a, b, preferred_element_type=F32)


def _dot_nt(a, b):
    return lax.dot_general(a, b, (((1,), (1,)), ((), ())), preferred_element_type=F32)


def _adaln_kernel(c_ref, w_ref, b_ref, o_ref):
    c = c_ref[...]
    s = c * jax.nn.sigmoid(c)
    o_ref[...] = jnp.dot(s, w_ref[...], preferred_element_type=F32,
                         precision=lax.Precision.HIGHEST) + b_ref[...]


def _adaln(cond, w_mod, b_mod):
    depth = w_mod.shape[0]
    rows = cond.shape[0]
    n_tiles = (N_MOD * D_MODEL) // D_MODEL
    return pl.pallas_call(
        _adaln_kernel,
        grid=(depth, n_tiles),
        in_specs=[
            pl.BlockSpec((rows, D_MODEL), lambda l, j: (0, 0)),
            pl.BlockSpec((None, D_MODEL, D_MODEL), lambda l, j: (l, 0, j)),
            pl.BlockSpec((None, 1, D_MODEL), lambda l, j: (l, 0, j)),
        ],
        out_specs=pl.BlockSpec((None, rows, D_MODEL), lambda l, j: (l, 0, j)),
        out_shape=jax.ShapeDtypeStruct((depth, rows, N_MOD * D_MODEL), F32),
        compiler_params=_cparams("parallel", "parallel"),
        name="adaln",
    )(cond, w_mod, b_mod.reshape(depth, 1, N_MOD * D_MODEL))


def _s5_prep_kernel(lre_ref, lim_ref, ls_ref, bre_ref, bim_ref, cre_ref, cim_ref,
                    are_ref, aim_ref, fre_ref, fim_ref, ore_ref, oim_ref):
    lr = jnp.minimum(lre_ref[...], -S5_MIN_DECAY)
    li = lim_ref[...]
    dt = jnp.exp(ls_ref[...])
    xr = lr * dt
    xi = li * dt

    def power(k):
        e = jnp.exp(k * xr)
        return e * jnp.cos(k * xi), e * jnp.sin(k * xi)

    ar, ai = power(1.0)
    nr = ar - 1.0
    den = lr * lr + li * li
    cr = (nr * lr + ai * li) / den
    ci = (ai * lr - nr * li) / den
    bre = bre_ref[...]
    bim = bim_ref[...]
    bbr = cr * bre - ci * bim
    bbi = cr * bim + ci * bre
    cre = cre_ref[...]
    cim = cim_ref[...]
    fre_ref[:, 0] = bbr
    fim_ref[:, 0] = bbi
    ore_ref[:, 0] = cre
    oim_ref[:, 0] = cim
    for k in range(1, SCAN_LC + 1):
        pr, pi = power(float(k))
        if k < SCAN_LC:
            fre_ref[:, k] = pr * bbr - pi * bbi
            fim_ref[:, k] = pr * bbi + pi * bbr
        ore_ref[:, k] = pr * cre - pi * cim
        oim_ref[:, k] = pr * cim + pi * cre
    are_ref[...], aim_ref[...] = power(float(SCAN_LC))


def _s5_prep(lam_re, lam_im, log_step, b_re, b_im, c_re, c_im):
    depth = lam_re.shape[0]
    rows = depth * 2 * SSM_GROUPS
    rb = SSM_GROUPS
    lre = lam_re.reshape(rows, 1, SSM_STATE)
    lim = lam_im.reshape(rows, 1, SSM_STATE)
    ls = jnp.broadcast_to(log_step.reshape(rows, 1, 1), (rows, 1, SSM_STATE))
    bre = jnp.swapaxes(b_re.reshape(rows, SSM_STATE, SSM_GROUP), 1, 2)
    bim = jnp.swapaxes(b_im.reshape(rows, SSM_STATE, SSM_GROUP), 1, 2)
    cre = c_re.reshape(rows, SSM_GROUP, SSM_STATE)
    cim = c_im.reshape(rows, SSM_GROUP, SSM_STATE)
    vec = pl.BlockSpec((rb, 1, SSM_STATE), lambda r: (r, 0, 0))
    mat = pl.BlockSpec((rb, SSM_GROUP, SSM_STATE), lambda r: (r, 0, 0))
    pw = lambda n: pl.BlockSpec((rb, n, SSM_GROUP, SSM_STATE), lambda r: (r, 0, 0, 0))
    v2 = jax.ShapeDtypeStruct((rows, 1, SSM_STATE), F32)
    v4 = lambda n: jax.ShapeDtypeStruct((rows, n, SSM_GROUP, SSM_STATE), F32)
    return pl.pallas_call(
        _s5_prep_kernel,
        grid=(rows // rb,),
        in_specs=[vec, vec, vec, mat, mat, mat, mat],
        out_specs=[vec, vec, pw(SCAN_LC), pw(SCAN_LC), pw(SCAN_LC + 1), pw(SCAN_LC + 1)],
        out_shape=[v2, v2, v4(SCAN_LC), v4(SCAN_LC), v4(SCAN_LC + 1), v4(SCAN_LC + 1)],
        compiler_params=pltpu.CompilerParams(dimension_semantics=("parallel",)),
        name="s5_prep",
    )(lre, lim, ls, bre, bim, cre, cim)


def _s5_operands_kernel(fre_ref, fim_ref, ore_ref, oim_ref,
                        fold_re_ref, fold_im_ref, out_re_ref, out_im_ref, toe_ref, bd_ref):
    lc = SCAN_LC
    hi = lax.Precision.HIGHEST
    forward = pl.program_id(0) % 2 == 0
    bd_ref[...] = jnp.zeros_like(bd_ref)

    def block_diag(x):
        for g in range(SSM_LANE_GROUPS):
            bd_ref[g * SSM_GROUP:(g + 1) * SSM_GROUP, g * SSM_STATE:(g + 1) * SSM_STATE] = x[g]
        return bd_ref[...]

    for j in range(lc):
        for f_ref, fold_ref in ((fre_ref, fold_re_ref), (fim_ref, fold_im_ref)):
            f = jnp.where(forward, f_ref[:, lc - 1 - j], f_ref[:, j])
            fold_ref[j * LANES:(j + 1) * LANES, :] = block_diag(f).astype(BF16)

    oc_re = [block_diag(ore_ref[:, k]).T for k in range(lc + 1)]
    oc_im = [block_diag(oim_ref[:, k]).T for k in range(lc + 1)]
    for j in range(lc):
        cols = slice(j * LANES, (j + 1) * LANES)
        out_re_ref[:, cols] = jnp.where(forward, oc_re[j + 1], oc_re[lc - j]).astype(BF16)
        out_im_ref[:, cols] = jnp.where(forward, oc_im[j + 1], oc_im[lc - j]).astype(BF16)

    bd_re = block_diag(fre_ref[:, 0])
    bd_im = block_diag(fim_ref[:, 0])
    kk = [jnp.dot(bd_re, oc_re[k], preferred_element_type=F32, precision=hi)
          - jnp.dot(bd_im, oc_im[k], preferred_element_type=F32, precision=hi) for k in range(lc)]
    zero = jnp.zeros((LANES, LANES), F32)
    for a in range(lc):
        for b in range(lc):
            fwd = kk[b - a] if b >= a else zero
            bwd = kk[a - b] if a >= b else zero
            toe_ref[a * LANES:(a + 1) * LANES, b * LANES:(b + 1) * LANES] = (
                jnp.where(forward, fwd, bwd).astype(BF16))


def _s5_operands(f_re, f_im, o_re, o_im, depth):
    lc = SCAN_LC
    ng = SSM_LANE_GROUPS
    group_block = lambda n: pl.BlockSpec((ng, n, SSM_GROUP, SSM_STATE),
                                         lambda ld, q: (ld * SSM_BLOCKS + q, 0, 0, 0))
    out = lambda r, c: pl.BlockSpec((None, None, r, c), lambda ld, q: (ld, q, 0, 0))
    sds = lambda r, c: jax.ShapeDtypeStruct((2 * depth, SSM_BLOCKS, r, c), BF16)
    fold_shape = (lc * LANES, SSM_BLOCK_STATES)
    out_shape = (SSM_BLOCK_STATES, lc * LANES)
    toe_shape = (lc * LANES, lc * LANES)
    results = pl.pallas_call(
        _s5_operands_kernel,
        grid=(2 * depth, SSM_BLOCKS),
        in_specs=[group_block(lc), group_block(lc), group_block(lc + 1), group_block(lc + 1)],
        out_specs=[out(*fold_shape), out(*fold_shape), out(*out_shape), out(*out_shape),
                   out(*toe_shape)],
        out_shape=[sds(*fold_shape), sds(*fold_shape), sds(*out_shape), sds(*out_shape),
                   sds(*toe_shape)],
        scratch_shapes=[pltpu.VMEM((LANES, SSM_BLOCK_STATES), F32)],
        compiler_params=pltpu.CompilerParams(dimension_semantics=("parallel", "parallel")),
        name="s5_operands",
    )(f_re, f_im, o_re, o_im)
    return [r.reshape((depth, 2) + r.shape[1:]) for r in results]


def _s5_matrices(lam_re, lam_im, log_step, b_re, b_im, c_re, c_im):
    depth = lam_re.shape[0]
    a_re, a_im, f_re, f_im, o_re, o_im = _s5_prep(lam_re, lam_im, log_step, b_re, b_im, c_re, c_im)
    fold_re, fold_im, out_re, out_im, toe = _s5_operands(f_re, f_im, o_re, o_im, depth)
    a_shape = (depth, 2, 1, SSM_STATES)
    a_re = jnp.broadcast_to(a_re.reshape(a_shape), (depth, 2, SUBLANES, SSM_STATES))
    a_im = jnp.broadcast_to(a_im.reshape(a_shape), (depth, 2, SUBLANES, SSM_STATES))
    return fold_re, fold_im, out_re, out_im, toe, a_re, a_im


def _batch_rows(b):
    return pl.ds(b, TM, stride=SUBLANES)


def _qv_rows_kernel(w_ref, o_ref):
    o_ref[...] = w_ref[...].T.astype(BF16)


def _qv_rows(w_in):
    depth = w_in.shape[0]
    col_blocks = (IN_OFF[4] // NA_W, IN_OFF[6] // NA_W)
    assert IN_OFF[4] % NA_W == 0 and IN_OFF[6] % NA_W == 0
    return pl.pallas_call(
        _qv_rows_kernel,
        grid=(depth, 2),
        in_specs=[pl.BlockSpec((None, D_MODEL, NA_W),
                               lambda l, j: (l, 0, jnp.where(j == 0, col_blocks[0], col_blocks[1])))],
        out_specs=pl.BlockSpec((None, NA_W, D_MODEL), lambda l, j: (l, j, 0)),
        out_shape=jax.ShapeDtypeStruct((depth, 2 * NA_W, D_MODEL), BF16),
        compiler_params=pltpu.CompilerParams(dimension_semantics=("parallel", "parallel")),
        name="qv_rows",
    )(w_in)


def _stream_tile(x_ref, c_ref):
    if c_ref is None:
        return x_ref[...]
    return jnp.where(pl.program_id(0) < CTX_LEN // TM, c_ref[...], x_ref[...])


def _in_proj_kernel(*refs, first):
    c_ref = refs[1] if first else None
    x_ref = refs[0]
    (sh_ref, sc_ref, g_ref, w_ref, wqv_ref, conv_ref, u_ref, ub_ref, qt_ref, k_ref, vt_ref,
     gate_ref) = refs[2:] if first else refs[1:]
    h = _rms(_stream_tile(x_ref, c_ref), g_ref[...]) * (1.0 + sc_ref[...]) + sh_ref[...]
    hb = h.astype(BF16)
    conv_ref[...] = _dot(hb, w_ref[:, IN_OFF[0]:IN_OFF[3]]).astype(BF16)
    u = _dot(hb, w_ref[:, IN_OFF[3]:IN_OFF[4]])
    rows = _batch_rows(pl.program_id(1))
    for q in range(SSM_BLOCKS):
        u_ref[q, rows, :] = u[:, q * LANES:(q + 1) * LANES]
    ub_ref[...] = u.astype(BF16)
    qt_ref[...] = _dot_nt(wqv_ref[0:NA_W, :], hb).astype(BF16)
    k_ref[...] = _dot(hb, w_ref[:, IN_OFF[5]:IN_OFF[6]]).astype(BF16)
    vt_ref[...] = _dot_nt(wqv_ref[NA_W:2 * NA_W, :], hb).astype(BF16)
    gate_ref[...] = _dot(hb, w_ref[:, IN_OFF[7]:IN_OFF[10]]).astype(BF16)


def _grid_bi(batch_major):
    return (lambda b, i: (b, i)) if batch_major else (lambda i, b: (b, i))


def _tok_spec(width, batch_major=True):
    bi = _grid_bi(batch_major)
    return pl.BlockSpec((None, TM, width), lambda *g: bi(*g) + (0,))


def _mod_spec(batch_major=True):
    bi = _grid_bi(batch_major)

    def index(*g):
        b, i = bi(*g)
        return (b, jnp.minimum(i, 1), 0, 0)

    return pl.BlockSpec((None, None, 1, D_MODEL), index)


def _const_spec(shape):
    nd = len(shape)
    return pl.BlockSpec(shape, lambda *g: (0,) * nd)


def _layer_spec(stacked, layer, **kwargs):
    shape = stacked.shape[1:]
    return pl.BlockSpec((None,) + shape, lambda *g: (layer,) + (0,) * len(shape), **kwargs)


def _slab_spec():
    return pl.BlockSpec((SSM_BLOCKS, TM * SUBLANES, LANES), lambda i, b: (0, i, 0))


def _fm_spec(batch_major=True):
    bi = _grid_bi(batch_major)
    return pl.BlockSpec((None, None, NA_W, TM), lambda *g: bi(*g) + (0, 0))


def _stream_specs(first):
    if not first:
        return [_tok_spec(D_MODEL, batch_major=False)]
    nc = CTX_LEN // TM
    return [pl.BlockSpec((None, TM, D_MODEL), lambda i, b: (b, jnp.maximum(i - nc, 0), 0)),
            pl.BlockSpec((None, TM, D_MODEL), lambda i, b: (b, jnp.minimum(i, nc - 1), 0))]


def _in_proj(stream, shift, scale, g, w, w_qv_t, layer):
    first = len(stream) == 2
    bsz = stream[0].shape[0]
    s = stream[0].shape[1] + (CTX_LEN if first else 0)
    nt = s // TM
    tok = lambda width: _tok_spec(width, batch_major=False)
    mod = _mod_spec(batch_major=False)
    fm = _fm_spec(batch_major=False)
    fm_sds = jax.ShapeDtypeStruct((bsz, nt, NA_W, TM), BF16)
    return pl.pallas_call(
        functools.partial(_in_proj_kernel, first=first),
        grid=(nt, bsz),
        in_specs=_stream_specs(first) + [mod, mod, _const_spec((1, D_MODEL)),
                                         _layer_spec(w, layer), _layer_spec(w_qv_t, layer)],
        out_specs=[tok(3 * CONV_W), _slab_spec(), tok(SSM_W), fm, tok(NA_W), fm, tok(3 * D_MODEL)],
        out_shape=[jax.ShapeDtypeStruct((bsz, s, 3 * CONV_W), BF16),
                   jax.ShapeDtypeStruct((SSM_BLOCKS, s * bsz, LANES), F32),
                   jax.ShapeDtypeStruct((bsz, s, SSM_W), BF16),
                   fm_sds,
                   jax.ShapeDtypeStruct((bsz, s, NA_W), BF16),
                   fm_sds,
                   jax.ShapeDtypeStruct((bsz, s, 3 * D_MODEL), BF16)],
        compiler_params=_cparams("parallel", "arbitrary"),
        name="in_proj",
    )(*stream, shift, scale, g, w, w_qv_t)


def _s5_scan_kernel(uf_ref, ub_ref, fre_ref, fim_ref, ore_ref, oim_ref, toe_ref, are_ref, aim_ref,
                    yf_ref, yb_ref, sre_ref, sim_ref, hre_ref, him_ref, ytoe_ref, yflat_ref):
    @pl.when(pl.program_id(0) == 0)
    def _():
        hre_ref[...] = jnp.zeros_like(hre_ref)
        him_ref[...] = jnp.zeros_like(him_ref)

    bs = SSM_BLOCK_STATES
    rows_c = SCAN_CHUNKS * SUBLANES

    for d, (u_ref, y_ref) in enumerate(((uf_ref, yf_ref), (ub_ref, yb_ref))):
        for q in range(SSM_BLOCKS):
            uc = jnp.concatenate([u_ref[q, :, j].reshape(rows_c, LANES) for j in range(SCAN_LC)],
                                 axis=1).astype(BF16)
            sre_ref[d, :, q * bs:(q + 1) * bs] = _dot(uc, fre_ref[d, q])
            sim_ref[d, :, q * bs:(q + 1) * bs] = _dot(uc, fim_ref[d, q])
            ytoe_ref[q] = _dot(uc, toe_ref[d, q])

        for q in range(SSM_BLOCKS):
            cols = slice(q * bs, (q + 1) * bs)
            ar = are_ref[d, :, cols]
            ai = aim_ref[d, :, cols]

            def step(c, carry, d=d, cols=cols, ar=ar, ai=ai):
                hr, hi = carry
                cc = c if d == 0 else SCAN_CHUNKS - 1 - c
                rows = pl.ds(pl.multiple_of(cc * SUBLANES, SUBLANES), SUBLANES)
                inc_r = sre_ref[d, rows, cols]
                inc_i = sim_ref[d, rows, cols]
                sre_ref[d, rows, cols] = hr
                sim_ref[d, rows, cols] = hi
                return ar * hr - ai * hi + inc_r, ar * hi + ai * hr + inc_i

            hr, hi = lax.fori_loop(0, SCAN_CHUNKS, step, (hre_ref[d, :, cols], him_ref[d, :, cols]),
                                   unroll=4)
            hre_ref[d, :, cols] = hr
            him_ref[d, :, cols] = hi

        for q in range(SSM_BLOCKS):
            cols = slice(q * bs, (q + 1) * bs)
            y = (ytoe_ref[q] + _dot(sre_ref[d, :, cols].astype(BF16), ore_ref[d, q])
                 - _dot(sim_ref[d, :, cols].astype(BF16), oim_ref[d, q]))
            for c in range(SCAN_CHUNKS):
                for j in range(SCAN_LC):
                    yflat_ref[q, pl.ds((c * SCAN_LC + j) * SUBLANES, SUBLANES), :] = (
                        y[c * SUBLANES:(c + 1) * SUBLANES, j * LANES:(j + 1) * LANES])

        for b in range(SUBLANES):
            for q in range(SSM_BLOCKS):
                y_ref[b, :, q * LANES:(q + 1) * LANES] = (
                    yflat_ref[q, pl.ds(b, SCAN_T, stride=SUBLANES), :].astype(BF16))


def _s5_scan(u_tm, f_re, f_im, o_re, o_im, toe, a_re, a_im, seq, layer):
    rows_total = u_tm.shape[1]
    bsz = rows_total // seq
    assert bsz == SUBLANES and SCAN_T % SCAN_LC == 0 and CTX_LEN % SCAN_T == 0
    nt = seq // SCAN_T
    nc = CTX_LEN // SCAN_T
    chunked = (SSM_BLOCKS, seq // SCAN_LC, SCAN_LC, bsz, LANES)
    block = (SSM_BLOCKS, SCAN_CHUNKS, SCAN_LC, bsz, LANES)

    def bwd_tile(i):
        return jnp.where(i < nc, nc - 1 - i, nt - 1 - (i - nc))

    fwd_spec = pl.BlockSpec(block, lambda i: (0, i, 0, 0, 0))
    bwd_spec = pl.BlockSpec(block, lambda i: (0, bwd_tile(i), 0, 0, 0))
    full = lambda a: _layer_spec(a, layer, pipeline_mode=pl.Buffered(1))
    y_sds = jax.ShapeDtypeStruct((bsz, seq, SSM_W), BF16)
    y_block = (bsz, SCAN_T, SSM_W)
    u5 = u_tm.reshape(chunked)
    return pl.pallas_call(
        _s5_scan_kernel,
        grid=(nt,),
        in_specs=[fwd_spec, bwd_spec, full(f_re), full(f_im), full(o_re), full(o_im), full(toe),
                  full(a_re), full(a_im)],
        out_specs=[pl.BlockSpec(y_block, lambda i: (0, i, 0)),
                   pl.BlockSpec(y_block, lambda i: (0, bwd_tile(i), 0))],
        out_shape=[y_sds, y_sds],
        scratch_shapes=[pltpu.VMEM((2, SCAN_CHUNKS * bsz, SSM_STATES), F32),
                        pltpu.VMEM((2, SCAN_CHUNKS * bsz, SSM_STATES), F32),
                        pltpu.VMEM((2, bsz, SSM_STATES), F32),
                        pltpu.VMEM((2, bsz, SSM_STATES), F32),
                        pltpu.VMEM((SSM_BLOCKS, SCAN_CHUNKS * bsz, SCAN_LC * LANES), F32),
                        pltpu.VMEM((SSM_BLOCKS, SCAN_T * bsz, LANES), F32)],
        compiler_params=_cparams("arbitrary"),
        name="s5_scan",
    )(u5, u5, f_re, f_im, o_re, o_im, toe, a_re, a_im)


def _attention_kernel(qt_ref, k_ref, vt_ref, bias_ref, o_ref, s_even_ref, s_odd_ref, *, grid_rows):
    i = pl.program_id(1)
    hd = NA_HEAD_DIM
    qt = (qt_ref[...].astype(F32) * (hd ** -0.5 * LOG2_E)).astype(BF16)
    zeros = jnp.zeros((hd, TM), BF16)
    ones = jnp.ones((BF16_ROWS, NA_KEY_BLOCK), BF16)
    ctx_rows = slice(0, CTX_LEN)

    def pair_cols(h):
        return slice((h // 2) * LANES, (h // 2 + 1) * LANES)

    def padded_qt(h):
        qh = qt[h * hd:(h + 1) * hd]
        return jnp.concatenate([qh, zeros] if h % 2 == 0 else [zeros, qh], axis=0)

    @pl.when(i == 0)
    def _():
        for h in range(NA_HEADS):
            s = _dot(k_ref[ctx_rows, pair_cols(h)], padded_qt(h))
            p = jnp.exp2(s - jnp.max(s, axis=0, keepdims=True)).astype(BF16)
            o = _dot(vt_ref[0, h * hd:(h + 1) * hd, :], p)
            l = _dot(ones, p)
            o_ref[h * hd:(h + 1) * hd, :] = (o / l[0:1]).astype(BF16)

    @pl.when(i > 0)
    def _():
        r_a = (i - 1) * NA_QROWS
        k_start = jnp.clip(r_a - WIN_H // 2, 0, grid_rows - NA_KROWS)
        start = pl.multiple_of(CTX_LEN + k_start * GRID_W, NA_KEY_BLOCK)
        kb = NA_KEY_BLOCK
        blk0 = start // kb
        n_loc = NA_KROWS * GRID_W // kb
        n_all = n_loc + 1
        k_rows = [pl.ds(start + j * kb, kb) for j in range(n_loc)] + [ctx_rows]
        v_blocks = [blk0 + j for j in range(n_loc)] + [0]
        rhs = [padded_qt(h) for h in range(NA_HEADS)]
        s_bufs = (s_even_ref, s_odd_ref)

        def scores(h, j):
            s = _dot(k_ref[k_rows[j], pair_cols(h)], rhs[h])
            if j < n_loc:
                s = s + bias_ref[h, j * kb:(j + 1) * kb, :]
            s_bufs[h % 2][j * kb:(j + 1) * kb] = s
            return jnp.max(s, axis=0, keepdims=True)

        m = functools.reduce(jnp.maximum, [scores(0, j) for j in range(n_all)])
        for h in range(NA_HEADS):
            ol = None
            next_max = []
            for j in range(n_all):
                if h + 1 < NA_HEADS:
                    next_max.append(scores(h + 1, j))
                p = jnp.exp2(s_bufs[h % 2][j * kb:(j + 1) * kb] - m).astype(BF16)
                lhs = jnp.concatenate([vt_ref[v_blocks[j], h * hd:(h + 1) * hd, :], ones], axis=0)
                part = _dot(lhs, p)
                ol = part if ol is None else ol + part
            o_ref[h * hd:(h + 1) * hd, :] = (ol[0:hd] / ol[hd:hd + 1]).astype(BF16)
            if next_max:
                m = functools.reduce(jnp.maximum, next_max)


def _na_tile_classes(grid_rows):
    return ((0, 0), (NA_QROWS, 0), (grid_rows - NA_QROWS, grid_rows - NA_KROWS))


def _na_valid(grid_rows):
    kr = np.arange(NA_KROWS)[:, None, None, None]
    kc = np.arange(GRID_W)[None, :, None, None]
    rq = np.arange(NA_QROWS)[None, None, :, None]
    cq = np.arange(GRID_W)[None, None, None, :]
    oks = []
    for r_a, k_start in _na_tile_classes(grid_rows):
        r0 = np.clip(r_a + rq - WIN_H // 2, 0, grid_rows - WIN_H)
        krow = k_start + kr
        cs = np.clip(cq - WIN_W // 2, 0, GRID_W - WIN_W)
        ok = (krow >= r0) & (krow < r0 + WIN_H) & (kc >= cs) & (kc < cs + WIN_W)
        oks.append(ok.reshape(NA_KROWS * GRID_W, TM))
    return np.stack(oks)


NA_DR_MARGIN = NA_QROWS
NA_DR_ROWS = 2 * WIN_H - 1 + 2 * NA_DR_MARGIN + 1


def _na_bias_kernel(w_ref, mask_ref, out_ref, *, grid_rows):
    lo_half = lax.broadcasted_iota(jnp.int32, (GRID_W, LANES), 1) < GRID_W
    cache = {}

    def toeplitz(d, upper):
        if (d, upper) not in cache:
            row = jnp.broadcast_to(w_ref[d:d + 1, :], (GRID_W, LANES))
            shift = GRID_W + 1 + (GRID_W if upper else 0)
            cache[d, upper] = pltpu.roll(row, shift, 1, stride=1, stride_axis=0)
        return cache[d, upper]

    for cls, (r_a, k_start) in enumerate(_na_tile_classes(grid_rows)):
        for kr in range(NA_KROWS):
            for pair in range(NA_QROWS // 2):
                d = k_start + kr - (r_a + 2 * pair) + (WIN_H - 1) + NA_DR_MARGIN
                block = jnp.where(lo_half, toeplitz(d, False), toeplitz(d - 1, True))
                rows = slice(kr * GRID_W, (kr + 1) * GRID_W)
                cols = slice(pair * LANES, (pair + 1) * LANES)
                out_ref[cls, rows, cols] = block + mask_ref[cls, rows, cols]


def _na_bias_tables(rpb, grid_rows):
    depth, heads, n_dr, n_dc = rpb.shape
    n_keys = NA_KROWS * GRID_W
    first_lane = (GRID_W - 1) - (WIN_W - 1)
    w = jnp.pad(jnp.flip(rpb * LOG2_E, axis=-1),
                ((0, 0), (0, 0), (NA_DR_MARGIN, NA_DR_ROWS - n_dr - NA_DR_MARGIN),
                 (first_lane, LANES - first_lane - n_dc)))
    mask = jnp.asarray(np.where(_na_valid(grid_rows), 0.0, NEG_INF).astype(np.float32))
    return pl.pallas_call(
        functools.partial(_na_bias_kernel, grid_rows=grid_rows),
        grid=(depth, heads),
        in_specs=[pl.BlockSpec((None, None, NA_DR_ROWS, LANES), lambda l, h: (l, h, 0, 0)),
                  pl.BlockSpec((3, n_keys, TM), lambda l, h: (0, 0, 0))],
        out_specs=pl.BlockSpec((None, 3, None, n_keys, TM), lambda l, h: (l, 0, h, 0, 0)),
        out_shape=jax.ShapeDtypeStruct((depth, 3, heads, n_keys, TM), F32),
        compiler_params=pltpu.CompilerParams(dimension_semantics=("parallel", "parallel")),
        name="na_bias",
    )(w, mask)


def _attention(qt, k, vt, bias, grid_rows, layer):
    bsz, s, _ = k.shape
    nt = s // TM
    n_keys = NA_KROWS * GRID_W

    def bias_class(b, i):
        return (layer, jnp.where(i <= 1, 0, jnp.where(i == nt - 1, 2, 1)), 0, 0, 0)

    return pl.pallas_call(
        functools.partial(_attention_kernel, grid_rows=grid_rows),
        grid=(bsz, nt),
        in_specs=[_fm_spec(),
                  pl.BlockSpec((None, s, NA_W), lambda b, i: (b, 0, 0)),
                  pl.BlockSpec((None, nt, NA_W, TM), lambda b, i: (b, 0, 0, 0)),
                  pl.BlockSpec((None, None, NA_HEADS, n_keys, TM), bias_class)],
        out_specs=_fm_spec(),
        out_shape=jax.ShapeDtypeStruct((bsz, nt, NA_W, TM), BF16),
        scratch_shapes=[pltpu.VMEM((n_keys + CTX_LEN, TM), F32),
                        pltpu.VMEM((n_keys + CTX_LEN, TM), F32)],
        compiler_params=_cparams("parallel", "arbitrary"),
        name="attention",
    )(qt, k, vt, bias)


def _gelu_tanh(x):
    return 0.5 * x * (1.0 + jnp.tanh(math.sqrt(2.0 / math.pi) * (x + 0.044715 * (x * x * x))))


def _merge_kernel(*refs, n_tiles, first):
    c_ref = refs[1] if first else None
    x_ref = refs[0]
    (cg_ref, cprev_ref, cnext_ref, u_ref, yf_ref, yb_ref, o_ref, gate_ref, g1_ref, cw_ref, cout_ref,
     d_ref, glua_ref, glub_ref, naout_ref, wout_ref, out_ref) = refs[2:] if first else refs[1:]
    i = pl.program_id(0)

    def gated_input(ref):
        return ref[:, 2 * CONV_W:3 * CONV_W].astype(F32) * ref[:, 0:CONV_W].astype(F32)

    z = gated_input(cg_ref)
    has_prev = (i >= 2).astype(F32)
    has_next = jnp.logical_and(i != 0, i != n_tiles - 1).astype(F32)
    z_before = gated_input(cprev_ref)[BF16_ROWS - 1:BF16_ROWS] * has_prev
    z_after = gated_input(cnext_ref)[0:1] * has_next
    row = lax.broadcasted_iota(jnp.int32, z.shape, 0)
    z_prev = jnp.where(row == 0, z_before, pltpu.roll(z, 1, 0))
    z_next = jnp.where(row == TM - 1, z_after, pltpu.roll(z, TM - 1, 0))
    conv = z_prev * cw_ref[0:1, :] + z * cw_ref[1:2, :] + z_next * cw_ref[2:3, :]
    a_pre = cg_ref[:, CONV_W:2 * CONV_W].astype(F32) * conv
    ya = _dot(a_pre.astype(BF16), cout_ref[...])

    y = (yf_ref[...].astype(F32) + yb_ref[...].astype(F32)
         + d_ref[...] * u_ref[...].astype(F32))
    g = _gelu_tanh(y).astype(BF16)
    def twice_logistic(half_x):
        return jnp.tanh(half_x) + 1.0

    yb = _dot(g, glua_ref[...]) * twice_logistic(_dot(g, glub_ref[...]))

    yc = lax.dot_general(o_ref[...], naout_ref[...], (((0,), (0,)), ((), ())),
                         preferred_element_type=F32)

    mix2 = (twice_logistic(gate_ref[:, 0:D_MODEL].astype(F32)) * ya
            + twice_logistic(gate_ref[:, D_MODEL:2 * D_MODEL].astype(F32)) * yb
            + twice_logistic(gate_ref[:, 2 * D_MODEL:3 * D_MODEL].astype(F32)) * yc)
    out_ref[...] = _stream_tile(x_ref, c_ref) + g1_ref[...] * _dot(mix2.astype(BF16), wout_ref[...])


def _merge(stream, conv_g, u_tm, yf, yb, attn_o, gates, gate1, conv_w, conv_out, s5_d, glu_a, glu_b,
           na_out, w_out, layer):
    first = len(stream) == 2
    bsz, s, _ = conv_g.shape
    nt = s // TM
    halo_blocks = s // BF16_ROWS
    per_tile = TM // BF16_ROWS
    tok = lambda width: _tok_spec(width, batch_major=False)
    tm_spec = tok(SSM_W)
    prev_spec = pl.BlockSpec((None, BF16_ROWS, 3 * CONV_W),
                             lambda i, b: (b, jnp.maximum(i * per_tile - 1, 0), 0))
    next_spec = pl.BlockSpec((None, BF16_ROWS, 3 * CONV_W),
                             lambda i, b: (b, jnp.minimum((i + 1) * per_tile, halo_blocks - 1), 0))
    return pl.pallas_call(
        functools.partial(_merge_kernel, n_tiles=nt, first=first),
        grid=(nt, bsz),
        in_specs=_stream_specs(first) + [
            tok(3 * CONV_W), prev_spec, next_spec, tm_spec, tm_spec, tm_spec,
            _fm_spec(batch_major=False), tok(3 * D_MODEL), _mod_spec(batch_major=False),
            _const_spec((3, CONV_W)), _layer_spec(conv_out, layer), _const_spec((1, SSM_W)),
            _layer_spec(glu_a, layer), _layer_spec(glu_b, layer),
            _layer_spec(na_out, layer), _layer_spec(w_out, layer)],
        out_specs=tok(D_MODEL),
        out_shape=jax.ShapeDtypeStruct((bsz, s, D_MODEL), F32),
        compiler_params=_cparams("parallel", "arbitrary"),
        name="merge",
    )(*stream, conv_g, conv_g, conv_g, u_tm, yf, yb, attn_o, gates, gate1, conv_w, conv_out, s5_d,
      glu_a, glu_b, na_out, w_out)


def _mlp_kernel(x_ref, sh_ref, sc_ref, gt_ref, g_ref, w1_ref, w2_ref, fg_ref, out_ref, *, final):
    x = x_ref[...]
    h = (_rms(x, g_ref[...]) * (1.0 + sc_ref[...]) + sh_ref[...]).astype(BF16)
    a = jnp.maximum(_dot(h, w1_ref[...]), 0.0)
    y = x + gt_ref[...] * _dot((a * a).astype(BF16), w2_ref[...])
    if final:
        y = _rms(y, fg_ref[...])
    out_ref[...] = y


def _mlp(xc, shift, scale, gate, g, w1, w2, final_g, final, layer):
    bsz, s, _ = xc.shape
    first = CTX_LEN // TM if final else 0
    nt = s // TM - first
    tok_in = pl.BlockSpec((None, TM, D_MODEL), lambda b, i: (b, i + first, 0))
    mod = pl.BlockSpec((None, None, 1, D_MODEL), lambda b, i: (b, jnp.minimum(i + first, 1), 0, 0))
    return pl.pallas_call(
        functools.partial(_mlp_kernel, final=final),
        grid=(bsz, nt),
        in_specs=[tok_in, mod, mod, mod, _const_spec((1, D_MODEL)),
                  _layer_spec(w1, layer), _layer_spec(w2, layer), _const_spec((1, D_MODEL))],
        out_specs=_tok_spec(D_MODEL),
        out_shape=jax.ShapeDtypeStruct((bsz, nt * TM, D_MODEL), F32),
        compiler_params=_cparams("parallel", "parallel"),
        name="mlp",
    )(xc, shift, scale, gate, g, w1, w2, final_g)


def kernel(x, c, ctx, c_ctx, w_mod, b_mod, norm1_g, w_in, conv_w, conv_out, s5_lam_re, s5_lam_im,
           s5_log_step, s5_b_re, s5_b_im, s5_c_re, s5_c_im, s5_d, s5_glu_a, s5_glu_b, na_rpb,
           na_out, w_out, norm2_g, mlp_w1, mlp_w2, final_norm_g):
    bsz, seq, _ = x.shape
    depth = w_mod.shape[0]
    assert ctx.shape[1] == CTX_LEN == TM and seq % TM == 0 and bsz == SUBLANES
    grid_rows = seq // GRID_W
    assert grid_rows >= NA_KROWS and grid_rows % NA_QROWS == 0
    s = CTX_LEN + seq

    cond_rows = 2 * SUBLANES
    cond = jnp.zeros((cond_rows, D_MODEL), F32).at[:bsz].set(c).at[bsz].set(c_ctx)
    mods = _adaln(cond, w_mod, b_mod).reshape(depth, cond_rows, N_MOD, D_MODEL)
    lat = mods[:, :bsz]
    cx = jnp.broadcast_to(mods[:, bsz:bsz + 1], lat.shape)
    mod = jnp.stack([cx, lat], axis=2)[:, :, :, :, None, :]

    s5_mats = _s5_matrices(s5_lam_re, s5_lam_im, s5_log_step, s5_b_re, s5_b_im, s5_c_re, s5_c_im)

    bf = lambda w: w.astype(BF16)
    gate_cols = (jnp.arange(IN_PROJ_W) >= IN_OFF[7])
    w_in_b = bf(w_in * jnp.where(gate_cols, 0.5, 1.0).astype(F32))
    conv_out_b, glu_a_b, glu_b_b = bf(conv_out), bf(s5_glu_a * 0.5), bf(s5_glu_b * 0.5)
    na_out_b, w_out_b, w1_b, w2_b = bf(na_out), bf(w_out * 0.5), bf(mlp_w1), bf(mlp_w2)
    w_qv_t = _qv_rows(w_in)

    na_bias = _na_bias_tables(na_rpb, grid_rows)

    stream = (x, ctx)
    row = lambda v: v.reshape(1, -1)
    for l in range(depth):
        m = mod[l]
        conv_g, u_tm, u_bm, qt, k, vt, gates = _in_proj(stream, m[:, :, 0], m[:, :, 1],
                                                        row(norm1_g[l]), w_in_b, w_qv_t, l)
        yf, yb = _s5_scan(u_tm, *s5_mats, s, l)
        attn_o = _attention(qt, k, vt, na_bias, grid_rows, l)
        xc = _merge(stream, conv_g, u_bm, yf, yb, attn_o, gates, m[:, :, 2], conv_w[l],
                    conv_out_b, row(s5_d[l]), glu_a_b, glu_b_b, na_out_b, w_out_b, l)
        xc = _mlp(xc, m[:, :, 3], m[:, :, 4], m[:, :, 5], row(norm2_g[l]), w1_b, w2_b,
                  row(final_norm_g), final=(l == depth - 1), layer=l)
        stream = (xc,)
    return xc
```

```python
import functools
import math

import numpy as np
import jax
import jax.numpy as jnp
from jax import lax
from jax.experimental import pallas as pl
from jax.experimental.pallas import tpu as pltpu

D_MODEL = 1024
CTX_LEN = 256
GRID_W = 64
N_MOD = 6
CONV_W = 512
SSM_W = 512
SSM_GROUP = 16
SSM_GROUPS = SSM_W // SSM_GROUP
SSM_STATE = 64
NA_HEADS = 8
NA_HEAD_DIM = 64
NA_W = NA_HEADS * NA_HEAD_DIM
WIN_H = 8
WIN_W = 16
MLP_HIDDEN = 4 * D_MODEL
IN_SIZES = (CONV_W, CONV_W, CONV_W, SSM_W, NA_W, NA_W, NA_W, D_MODEL, D_MODEL, D_MODEL)
IN_OFF = tuple(sum(IN_SIZES[:i]) for i in range(len(IN_SIZES) + 1))
IN_PROJ_W = IN_OFF[-1]
RMS_EPS = 1e-6
NEG_INF = -1e30
S5_MIN_DECAY = 1e-4
LOG2_E = math.log2(math.e)

F32 = jnp.float32
BF16 = jnp.bfloat16

LANES = 128
SUBLANES = 8
BF16_ROWS = 16
TM = 256
SCAN_LC = 4
SCAN_T = 128
SCAN_CHUNKS = SCAN_T // SCAN_LC
NA_QROWS = TM // GRID_W
NA_KROWS = NA_QROWS + WIN_H
NA_KEY_BLOCK = TM
SSM_LANE_GROUPS = LANES // SSM_GROUP
SSM_BLOCKS = SSM_W // LANES
SSM_BLOCK_STATES = SSM_LANE_GROUPS * SSM_STATE
SSM_STATES = SSM_GROUPS * SSM_STATE
VMEM_LIMIT = 56 * 1024 * 1024


def _cparams(*sem):
    return pltpu.CompilerParams(dimension_semantics=sem, vmem_limit_bytes=VMEM_LIMIT)


def _rms(x, g):
    return x * lax.rsqrt(jnp.mean(x * x, axis=-1, keepdims=True) + RMS_EPS) * g


def _dot(a, b):
    return jnp.dot(a, b, preferred_element_type=F32)


def _dot_nt(a, b):
    return lax.dot_general(a, b, (((1,), (1,)), ((), ())), preferred_element_type=F32)


def _adaln_kernel(c_ref, w_ref, b_ref, o_ref):
    c = c_ref[...]
    s = c * jax.nn.sigmoid(c)
    o_ref[...] = jnp.dot(s, w_ref[...], preferred_element_type=F32,
                         precision=lax.Precision.HIGHEST) + b_ref[...]


def _adaln(cond, w_mod, b_mod):
    depth = w_mod.shape[0]
    rows = cond.shape[0]
    n_tiles = (N_MOD * D_MODEL) // D_MODEL
    return pl.pallas_call(
        _adaln_kernel,
        grid=(depth, n_tiles),
        in_specs=[
            pl.BlockSpec((rows, D_MODEL), lambda l, j: (0, 0)),
            pl.BlockSpec((None, D_MODEL, D_MODEL), lambda l, j: (l, 0, j)),
            pl.BlockSpec((None, 1, D_MODEL), lambda l, j: (l, 0, j)),
        ],
        out_specs=pl.BlockSpec((None, rows, D_MODEL), lambda l, j: (l, 0, j)),
        out_shape=jax.ShapeDtypeStruct((depth, rows, N_MOD * D_MODEL), F32),
        compiler_params=_cparams("parallel", "parallel"),
        name="adaln",
    )(cond, w_mod, b_mod.reshape(depth, 1, N_MOD * D_MODEL))


def _s5_prep_kernel(lre_ref, lim_ref, ls_ref, bre_ref, bim_ref, cre_ref, cim_ref,
                    are_ref, aim_ref, fre_ref, fim_ref, ore_ref, oim_ref):
    lr = jnp.minimum(lre_ref[...], -S5_MIN_DECAY)
    li = lim_ref[...]
    dt = jnp.exp(ls_ref[...])
    xr = lr * dt
    xi = li * dt

    def power(k):
        e = jnp.exp(k * xr)
        return e * jnp.cos(k * xi), e * jnp.sin(k * xi)

    ar, ai = power(1.0)
    nr = ar - 1.0
    den = lr * lr + li * li
    cr = (nr * lr + ai * li) / den
    ci = (ai * lr - nr * li) / den
    bre = bre_ref[...]
    bim = bim_ref[...]
    bbr = cr * bre - ci * bim
    bbi = cr * bim + ci * bre
    cre = cre_ref[...]
    cim = cim_ref[...]
    fre_ref[:, 0] = bbr
    fim_ref[:, 0] = bbi
    ore_ref[:, 0] = cre
    oim_ref[:, 0] = cim
    for k in range(1, SCAN_LC + 1):
        pr, pi = power(float(k))
        if k < SCAN_LC:
            fre_ref[:, k] = pr * bbr - pi * bbi
            fim_ref[:, k] = pr * bbi + pi * bbr
        ore_ref[:, k] = pr * cre - pi * cim
        oim_ref[:, k] = pr * cim + pi * cre
    are_ref[...], aim_ref[...] = power(float(SCAN_LC))


def _s5_prep(lam_re, lam_im, log_step, b_re, b_im, c_re, c_im):
    depth = lam_re.shape[0]
    rows = depth * 2 * SSM_GROUPS
    rb = SSM_GROUPS
    lre = lam_re.reshape(rows, 1, SSM_STATE)
    lim = lam_im.reshape(rows, 1, SSM_STATE)
    ls = jnp.broadcast_to(log_step.reshape(rows, 1, 1), (rows, 1, SSM_STATE))
    bre = jnp.swapaxes(b_re.reshape(rows, SSM_STATE, SSM_GROUP), 1, 2)
    bim = jnp.swapaxes(b_im.reshape(rows, SSM_STATE, SSM_GROUP), 1, 2)
    cre = c_re.reshape(rows, SSM_GROUP, SSM_STATE)
    cim = c_im.reshape(rows, SSM_GROUP, SSM_STATE)
    vec = pl.BlockSpec((rb, 1, SSM_STATE), lambda r: (r, 0, 0))
    mat = pl.BlockSpec((rb, SSM_GROUP, SSM_STATE), lambda r: (r, 0, 0))
    pw = lambda n: pl.BlockSpec((rb, n, SSM_GROUP, SSM_STATE), lambda r: (r, 0, 0, 0))
    v2 = jax.ShapeDtypeStruct((rows, 1, SSM_STATE), F32)
    v4 = lambda n: jax.ShapeDtypeStruct((rows, n, SSM_GROUP, SSM_STATE), F32)
    return pl.pallas_call(
        _s5_prep_kernel,
        grid=(rows // rb,),
        in_specs=[vec, vec, vec, mat, mat, mat, mat],
        out_specs=[vec, vec, pw(SCAN_LC), pw(SCAN_LC), pw(SCAN_LC + 1), pw(SCAN_LC + 1)],
        out_shape=[v2, v2, v4(SCAN_LC), v4(SCAN_LC), v4(SCAN_LC + 1), v4(SCAN_LC + 1)],
        compiler_params=pltpu.CompilerParams(dimension_semantics=("parallel",)),
        name="s5_prep",
    )(lre, lim, ls, bre, bim, cre, cim)


def _s5_operands_kernel(fre_ref, fim_ref, ore_ref, oim_ref,
                        fold_re_ref, fold_im_ref, out_re_ref, out_im_ref, toe_ref, bd_ref):
    lc = SCAN_LC
    hi = lax.Precision.HIGHEST
    forward = pl.program_id(0) % 2 == 0
    bd_ref[...] = jnp.zeros_like(bd_ref)

    def block_diag(x):
        for g in range(SSM_LANE_GROUPS):
            bd_ref[g * SSM_GROUP:(g + 1) * SSM_GROUP, g * SSM_STATE:(g + 1) * SSM_STATE] = x[g]
        return bd_ref[...]

    for j in range(lc):
        for f_ref, fold_ref in ((fre_ref, fold_re_ref), (fim_ref, fold_im_ref)):
            f = jnp.where(forward, f_ref[:, lc - 1 - j], f_ref[:, j])
            fold_ref[j * LANES:(j + 1) * LANES, :] = block_diag(f).astype(BF16)

    oc_re = [block_diag(ore_ref[:, k]).T for k in range(lc + 1)]
    oc_im = [block_diag(oim_ref[:, k]).T for k in range(lc + 1)]
    for j in range(lc):
        cols = slice(j * LANES, (j + 1) * LANES)
        out_re_ref[:, cols] = jnp.where(forward, oc_re[j + 1], oc_re[lc - j]).astype(BF16)
        out_im_ref[:, cols] = jnp.where(forward, oc_im[j + 1], oc_im[lc - j]).astype(BF16)

    bd_re = block_diag(fre_ref[:, 0])
    bd_im = block_diag(fim_ref[:, 0])
    kk = [jnp.dot(bd_re, oc_re[k], preferred_element_type=F32, precision=hi)
          - jnp.dot(bd_im, oc_im[k], preferred_element_type=F32, precision=hi) for k in range(lc)]
    zero = jnp.zeros((LANES, LANES), F32)
    for a in range(lc):
        for b in range(lc):
            fwd = kk[b - a] if b >= a else zero
            bwd = kk[a - b] if a >= b else zero
            toe_ref[a * LANES:(a + 1) * LANES, b * LANES:(b + 1) * LANES] = (
                jnp.where(forward, fwd, bwd).astype(BF16))


def _s5_operands(f_re, f_im, o_re, o_im, depth):
    lc = SCAN_LC
    ng = SSM_LANE_GROUPS
    group_block = lambda n: pl.BlockSpec((ng, n, SSM_GROUP, SSM_STATE),
                                         lambda ld, q: (ld * SSM_BLOCKS + q, 0, 0, 0))
    out = lambda r, c: pl.BlockSpec((None, None, r, c), lambda ld, q: (ld, q, 0, 0))
    sds = lambda r, c: jax.ShapeDtypeStruct((2 * depth, SSM_BLOCKS, r, c), BF16)
    fold_shape = (lc * LANES, SSM_BLOCK_STATES)
    out_shape = (SSM_BLOCK_STATES, lc * LANES)
    toe_shape = (lc * LANES, lc * LANES)
    results = pl.pallas_call(
        _s5_operands_kernel,
        grid=(2 * depth, SSM_BLOCKS),
        in_specs=[group_block(lc), group_block(lc), group_block(lc + 1), group_block(lc + 1)],
        out_specs=[out(*fold_shape), out(*fold_shape), out(*out_shape), out(*out_shape),
                   out(*toe_shape)],
        out_shape=[sds(*fold_shape), sds(*fold_shape), sds(*out_shape), sds(*out_shape),
                   sds(*toe_shape)],
        scratch_shapes=[pltpu.VMEM((LANES, SSM_BLOCK_STATES), F32)],
        compiler_params=pltpu.CompilerParams(dimension_semantics=("parallel", "parallel")),
        name="s5_operands",
    )(f_re, f_im, o_re, o_im)
    return [r.reshape((depth, 2) + r.shape[1:]) for r in results]


def _s5_matrices(lam_re, lam_im, log_step, b_re, b_im, c_re, c_im):
    depth = lam_re.shape[0]
    a_re, a_im, f_re, f_im, o_re, o_im = _s5_prep(lam_re, lam_im, log_step, b_re, b_im, c_re, c_im)
    fold_re, fold_im, out_re, out_im, toe = _s5_operands(f_re, f_im, o_re, o_im, depth)
    a_shape = (depth, 2, 1, SSM_STATES)
    a_re = jnp.broadcast_to(a_re.reshape(a_shape), (depth, 2, SUBLANES, SSM_STATES))
    a_im = jnp.broadcast_to(a_im.reshape(a_shape), (depth, 2, SUBLANES, SSM_STATES))
    return fold_re, fold_im, out_re, out_im, toe, a_re, a_im


def _batch_rows(b):
    return pl.ds(b, TM, stride=SUBLANES)


def _qv_rows_kernel(w_ref, o_ref):
    o_ref[...] = w_ref[...].T.astype(BF16)


def _qv_rows(w_in):
    depth = w_in.shape[0]
    col_blocks = (IN_OFF[4] // NA_W, IN_OFF[6] // NA_W)
    assert IN_OFF[4] % NA_W == 0 and IN_OFF[6] % NA_W == 0
    return pl.pallas_call(
        _qv_rows_kernel,
        grid=(depth, 2),
        in_specs=[pl.BlockSpec((None, D_MODEL, NA_W),
                               lambda l, j: (l, 0, jnp.where(j == 0, col_blocks[0], col_blocks[1])))],
        out_specs=pl.BlockSpec((None, NA_W, D_MODEL), lambda l, j: (l, j, 0)),
        out_shape=jax.ShapeDtypeStruct((depth, 2 * NA_W, D_MODEL), BF16),
        compiler_params=pltpu.CompilerParams(dimension_semantics=("parallel", "parallel")),
        name="qv_rows",
    )(w_in)


def _stream_tile(x_ref, c_ref):
    if c_ref is None:
        return x_ref[...]
    return jnp.where(pl.program_id(0) < CTX_LEN // TM, c_ref[...], x_ref[...])


def _in_proj_kernel(*refs, first):
    c_ref = refs[1] if first else None
    x_ref = refs[0]
    (sh_ref, sc_ref, g_ref, w_ref, wqv_ref, conv_ref, u_ref, ub_ref, qt_ref, k_ref, vt_ref,
     gate_ref) = refs[2:] if first else refs[1:]
    h = _rms(_stream_tile(x_ref, c_ref), g_ref[...]) * (1.0 + sc_ref[...]) + sh_ref[...]
    hb = h.astype(BF16)
    conv_ref[...] = _dot(hb, w_ref[:, IN_OFF[0]:IN_OFF[3]]).astype(BF16)
    u = _dot(hb, w_ref[:, IN_OFF[3]:IN_OFF[4]])
    rows = _batch_rows(pl.program_id(1))
    for q in range(SSM_BLOCKS):
        u_ref[q, rows, :] = u[:, q * LANES:(q + 1) * LANES]
    ub_ref[...] = u.astype(BF16)
    qt_ref[...] = _dot_nt(wqv_ref[0:NA_W, :], hb).astype(BF16)
    k_ref[...] = _dot(hb, w_ref[:, IN_OFF[5]:IN_OFF[6]]).astype(BF16)
    vt_ref[...] = _dot_nt(wqv_ref[NA_W:2 * NA_W, :], hb).astype(BF16)
    gate_ref[...] = _dot(hb, w_ref[:, IN_OFF[7]:IN_OFF[10]]).astype(BF16)


def _grid_bi(batch_major):
    return (lambda b, i: (b, i)) if batch_major else (lambda i, b: (b, i))


def _tok_spec(width, batch_major=True):
    bi = _grid_bi(batch_major)
    return pl.BlockSpec((None, TM, width), lambda *g: bi(*g) + (0,))


def _mod_spec(batch_major=True):
    bi = _grid_bi(batch_major)

    def index(*g):
        b, i = bi(*g)
        return (b, jnp.minimum(i, 1), 0, 0)

    return pl.BlockSpec((None, None, 1, D_MODEL), index)


def _const_spec(shape):
    nd = len(shape)
    return pl.BlockSpec(shape, lambda *g: (0,) * nd)


def _layer_spec(stacked, layer):
    shape = stacked.shape[1:]
    return pl.BlockSpec((None,) + shape, lambda *g: (layer,) + (0,) * len(shape),
                        pipeline_mode=pl.Buffered(1))


def _slab_spec():
    return pl.BlockSpec((SSM_BLOCKS, TM * SUBLANES, LANES), lambda i, b: (0, i, 0))


def _fm_spec(batch_major=True):
    bi = _grid_bi(batch_major)
    return pl.BlockSpec((None, None, NA_W, TM), lambda *g: bi(*g) + (0, 0))


def _stream_specs(first):
    if not first:
        return [_tok_spec(D_MODEL, batch_major=False)]
    nc = CTX_LEN // TM
    return [pl.BlockSpec((None, TM, D_MODEL), lambda i, b: (b, jnp.maximum(i - nc, 0), 0)),
            pl.BlockSpec((None, TM, D_MODEL), lambda i, b: (b, jnp.minimum(i, nc - 1), 0))]


def _in_proj(stream, shift, scale, g, w, w_qv_t, layer):
    first = len(stream) == 2
    bsz = stream[0].shape[0]
    s = stream[0].shape[1] + (CTX_LEN if first else 0)
    nt = s // TM
    tok = lambda width: _tok_spec(width, batch_major=False)
    mod = _mod_spec(batch_major=False)
    fm = _fm_spec(batch_major=False)
    fm_sds = jax.ShapeDtypeStruct((bsz, nt, NA_W, TM), BF16)
    return pl.pallas_call(
        functools.partial(_in_proj_kernel, first=first),
        grid=(nt, bsz),
        in_specs=_stream_specs(first) + [mod, mod, _const_spec((1, D_MODEL)),
                                         _layer_spec(w, layer), _layer_spec(w_qv_t, layer)],
        out_specs=[tok(3 * CONV_W), _slab_spec(), tok(SSM_W), fm, tok(NA_W), fm, tok(3 * D_MODEL)],
        out_shape=[jax.ShapeDtypeStruct((bsz, s, 3 * CONV_W), BF16),
                   jax.ShapeDtypeStruct((SSM_BLOCKS, s * bsz, LANES), F32),
                   jax.ShapeDtypeStruct((bsz, s, SSM_W), BF16),
                   fm_sds,
                   jax.ShapeDtypeStruct((bsz, s, NA_W), BF16),
                   fm_sds,
                   jax.ShapeDtypeStruct((bsz, s, 3 * D_MODEL), BF16)],
        compiler_params=_cparams("parallel", "arbitrary"),
        name="in_proj",
    )(*stream, shift, scale, g, w, w_qv_t)


def _s5_scan_kernel(uf_ref, ub_ref, fre_ref, fim_ref, ore_ref, oim_ref, toe_ref, are_ref, aim_ref,
                    yf_ref, yb_ref, sre_ref, sim_ref, hre_ref, him_ref, ytoe_ref, yflat_ref):
    @pl.when(pl.program_id(0) == 0)
    def _():
        hre_ref[...] = jnp.zeros_like(hre_ref)
        him_ref[...] = jnp.zeros_like(him_ref)

    bs = SSM_BLOCK_STATES
    rows_c = SCAN_CHUNKS * SUBLANES

    for d, (u_ref, y_ref) in enumerate(((uf_ref, yf_ref), (ub_ref, yb_ref))):
        for q in range(SSM_BLOCKS):
            uc = jnp.concatenate([u_ref[q, :, j].reshape(rows_c, LANES) for j in range(SCAN_LC)],
                                 axis=1).astype(BF16)
            sre_ref[d, :, q * bs:(q + 1) * bs] = _dot(uc, fre_ref[d, q])
            sim_ref[d, :, q * bs:(q + 1) * bs] = _dot(uc, fim_ref[d, q])
            ytoe_ref[q] = _dot(uc, toe_ref[d, q])

        for q in range(SSM_BLOCKS):
            cols = slice(q * bs, (q + 1) * bs)
            ar = are_ref[d, :, cols]
            ai = aim_ref[d, :, cols]

            def step(c, carry, d=d, cols=cols, ar=ar, ai=ai):
                hr, hi = carry
                cc = c if d == 0 else SCAN_CHUNKS - 1 - c
                rows = pl.ds(pl.multiple_of(cc * SUBLANES, SUBLANES), SUBLANES)
                inc_r = sre_ref[d, rows, cols]
                inc_i = sim_ref[d, rows, cols]
                sre_ref[d, rows, cols] = hr
                sim_ref[d, rows, cols] = hi
                return ar * hr - ai * hi + inc_r, ar * hi + ai * hr + inc_i

            hr, hi = lax.fori_loop(0, SCAN_CHUNKS, step, (hre_ref[d, :, cols], him_ref[d, :, cols]),
                                   unroll=4)
            hre_ref[d, :, cols] = hr
            him_ref[d, :, cols] = hi

        for q in range(SSM_BLOCKS):
            cols = slice(q * bs, (q + 1) * bs)
            y = (ytoe_ref[q] + _dot(sre_ref[d, :, cols].astype(BF16), ore_ref[d, q])
                 - _dot(sim_ref[d, :, cols].astype(BF16), oim_ref[d, q]))
            for c in range(SCAN_CHUNKS):
                for j in range(SCAN_LC):
                    yflat_ref[q, pl.ds((c * SCAN_LC + j) * SUBLANES, SUBLANES), :] = (
                        y[c * SUBLANES:(c + 1) * SUBLANES, j * LANES:(j + 1) * LANES])

        for b in range(SUBLANES):
            for q in range(SSM_BLOCKS):
                y_ref[b, :, q * LANES:(q + 1) * LANES] = (
                    yflat_ref[q, pl.ds(b, SCAN_T, stride=SUBLANES), :].astype(BF16))


def _s5_scan(u_tm, f_re, f_im, o_re, o_im, toe, a_re, a_im, seq, layer):
    rows_total = u_tm.shape[1]
    bsz = rows_total // seq
    assert bsz == SUBLANES and SCAN_T % SCAN_LC == 0 and CTX_LEN % SCAN_T == 0
    nt = seq // SCAN_T
    nc = CTX_LEN // SCAN_T
    chunked = (SSM_BLOCKS, seq // SCAN_LC, SCAN_LC, bsz, LANES)
    block = (SSM_BLOCKS, SCAN_CHUNKS, SCAN_LC, bsz, LANES)

    def bwd_tile(i):
        return jnp.where(i < nc, nc - 1 - i, nt - 1 - (i - nc))

    fwd_spec = pl.BlockSpec(block, lambda i: (0, i, 0, 0, 0))
    bwd_spec = pl.BlockSpec(block, lambda i: (0, bwd_tile(i), 0, 0, 0))
    full = lambda a: _layer_spec(a, layer)
    y_sds = jax.ShapeDtypeStruct((bsz, seq, SSM_W), BF16)
    y_block = (bsz, SCAN_T, SSM_W)
    u5 = u_tm.reshape(chunked)
    return pl.pallas_call(
        _s5_scan_kernel,
        grid=(nt,),
        in_specs=[fwd_spec, bwd_spec, full(f_re), full(f_im), full(o_re), full(o_im), full(toe),
                  full(a_re), full(a_im)],
        out_specs=[pl.BlockSpec(y_block, lambda i: (0, i, 0)),
                   pl.BlockSpec(y_block, lambda i: (0, bwd_tile(i), 0))],
        out_shape=[y_sds, y_sds],
        scratch_shapes=[pltpu.VMEM((2, SCAN_CHUNKS * bsz, SSM_STATES), F32),
                        pltpu.VMEM((2, SCAN_CHUNKS * bsz, SSM_STATES), F32),
                        pltpu.VMEM((2, bsz, SSM_STATES), F32),
                        pltpu.VMEM((2, bsz, SSM_STATES), F32),
                        pltpu.VMEM((SSM_BLOCKS, SCAN_CHUNKS * bsz, SCAN_LC * LANES), F32),
                        pltpu.VMEM((SSM_BLOCKS, SCAN_T * bsz, LANES), F32)],
        compiler_params=_cparams("arbitrary"),
        name="s5_scan",
    )(u5, u5, f_re, f_im, o_re, o_im, toe, a_re, a_im)


def _attention_kernel(qt_ref, k_ref, vt_ref, bias_ref, o_ref, s_even_ref, s_odd_ref, *, grid_rows):
    i = pl.program_id(1)
    hd = NA_HEAD_DIM
    qt = (qt_ref[...].astype(F32) * (hd ** -0.5 * LOG2_E)).astype(BF16)
    zeros = jnp.zeros((hd, TM), BF16)
    ones = jnp.ones((BF16_ROWS, NA_KEY_BLOCK), BF16)
    ctx_rows = slice(0, CTX_LEN)

    def pair_cols(h):
        return slice((h // 2) * LANES, (h // 2 + 1) * LANES)

    def padded_qt(h):
        qh = qt[h * hd:(h + 1) * hd]
        return jnp.concatenate([qh, zeros] if h % 2 == 0 else [zeros, qh], axis=0)

    @pl.when(i == 0)
    def _():
        for h in range(NA_HEADS):
            s = _dot(k_ref[ctx_rows, pair_cols(h)], padded_qt(h))
            p = jnp.exp2(s - jnp.max(s, axis=0, keepdims=True)).astype(BF16)
            o = _dot(vt_ref[0, h * hd:(h + 1) * hd, :], p)
            l = _dot(ones, p)
            o_ref[h * hd:(h + 1) * hd, :] = (o / l[0:1]).astype(BF16)

    @pl.when(i > 0)
    def _():
        r_a = (i - 1) * NA_QROWS
        k_start = jnp.clip(r_a - WIN_H // 2, 0, grid_rows - NA_KROWS)
        start = pl.multiple_of(CTX_LEN + k_start * GRID_W, NA_KEY_BLOCK)
        kb = NA_KEY_BLOCK
        blk0 = start // kb
        n_loc = NA_KROWS * GRID_W // kb
        n_all = n_loc + 1
        k_rows = [pl.ds(start + j * kb, kb) for j in range(n_loc)] + [ctx_rows]
        v_blocks = [blk0 + j for j in range(n_loc)] + [0]
        rhs = [padded_qt(h) for h in range(NA_HEADS)]
        s_bufs = (s_even_ref, s_odd_ref)

        def scores(h, j):
            s = _dot(k_ref[k_rows[j], pair_cols(h)], rhs[h])
            if j < n_loc:
                s = s + bias_ref[h, j * kb:(j + 1) * kb, :]
            s_bufs[h % 2][j * kb:(j + 1) * kb] = s
            return jnp.max(s, axis=0, keepdims=True)

        m = functools.reduce(jnp.maximum, [scores(0, j) for j in range(n_all)])
        for h in range(NA_HEADS):
            ol = None
            next_max = []
            for j in range(n_all):
                if h + 1 < NA_HEADS:
                    next_max.append(scores(h + 1, j))
                p = jnp.exp2(s_bufs[h % 2][j * kb:(j + 1) * kb] - m).astype(BF16)
                lhs = jnp.concatenate([vt_ref[v_blocks[j], h * hd:(h + 1) * hd, :], ones], axis=0)
                part = _dot(lhs, p)
                ol = part if ol is None else ol + part
            o_ref[h * hd:(h + 1) * hd, :] = (ol[0:hd] / ol[hd:hd + 1]).astype(BF16)
            if next_max:
                m = functools.reduce(jnp.maximum, next_max)


def _na_tile_classes(grid_rows):
    return ((0, 0), (NA_QROWS, 0), (grid_rows - NA_QROWS, grid_rows - NA_KROWS))


def _na_valid(grid_rows):
    kr = np.arange(NA_KROWS)[:, None, None, None]
    kc = np.arange(GRID_W)[None, :, None, None]
    rq = np.arange(NA_QROWS)[None, None, :, None]
    cq = np.arange(GRID_W)[None, None, None, :]
    oks = []
    for r_a, k_start in _na_tile_classes(grid_rows):
        r0 = np.clip(r_a + rq - WIN_H // 2, 0, grid_rows - WIN_H)
        krow = k_start + kr
        cs = np.clip(cq - WIN_W // 2, 0, GRID_W - WIN_W)
        ok = (krow >= r0) & (krow < r0 + WIN_H) & (kc >= cs) & (kc < cs + WIN_W)
        oks.append(ok.reshape(NA_KROWS * GRID_W, TM))
    return np.stack(oks)


NA_DR_MARGIN = NA_QROWS
NA_DR_ROWS = 2 * WIN_H - 1 + 2 * NA_DR_MARGIN + 1


def _na_bias_kernel(w_ref, mask_ref, out_ref, *, grid_rows):
    lo_half = lax.broadcasted_iota(jnp.int32, (GRID_W, LANES), 1) < GRID_W
    cache = {}

    def toeplitz(d, upper):
        if (d, upper) not in cache:
            row = jnp.broadcast_to(w_ref[d:d + 1, :], (GRID_W, LANES))
            shift = GRID_W + 1 + (GRID_W if upper else 0)
            cache[d, upper] = pltpu.roll(row, shift, 1, stride=1, stride_axis=0)
        return cache[d, upper]

    for cls, (r_a, k_start) in enumerate(_na_tile_classes(grid_rows)):
        for kr in range(NA_KROWS):
            for pair in range(NA_QROWS // 2):
                d = k_start + kr - (r_a + 2 * pair) + (WIN_H - 1) + NA_DR_MARGIN
                block = jnp.where(lo_half, toeplitz(d, False), toeplitz(d - 1, True))
                rows = slice(kr * GRID_W, (kr + 1) * GRID_W)
                cols = slice(pair * LANES, (pair + 1) * LANES)
                out_ref[cls, rows, cols] = block + mask_ref[cls, rows, cols]


def _na_bias_tables(rpb, grid_rows):
    depth, heads, n_dr, n_dc = rpb.shape
    n_keys = NA_KROWS * GRID_W
    first_lane = (GRID_W - 1) - (WIN_W - 1)
    w = jnp.pad(jnp.flip(rpb * LOG2_E, axis=-1),
                ((0, 0), (0, 0), (NA_DR_MARGIN, NA_DR_ROWS - n_dr - NA_DR_MARGIN),
                 (first_lane, LANES - first_lane - n_dc)))
    mask = jnp.asarray(np.where(_na_valid(grid_rows), 0.0, NEG_INF).astype(np.float32))
    return pl.pallas_call(
        functools.partial(_na_bias_kernel, grid_rows=grid_rows),
        grid=(depth, heads),
        in_specs=[pl.BlockSpec((None, None, NA_DR_ROWS, LANES), lambda l, h: (l, h, 0, 0)),
                  pl.BlockSpec((3, n_keys, TM), lambda l, h: (0, 0, 0))],
        out_specs=pl.BlockSpec((None, 3, None, n_keys, TM), lambda l, h: (l, 0, h, 0, 0)),
        out_shape=jax.ShapeDtypeStruct((depth, 3, heads, n_keys, TM), F32),
        compiler_params=pltpu.CompilerParams(dimension_semantics=("parallel", "parallel")),
        name="na_bias",
    )(w, mask)


def _attention(qt, k, vt, bias, grid_rows, layer):
    bsz, s, _ = k.shape
    nt = s // TM
    n_keys = NA_KROWS * GRID_W

    def bias_class(b, i):
        return (layer, jnp.where(i <= 1, 0, jnp.where(i == nt - 1, 2, 1)), 0, 0, 0)

    return pl.pallas_call(
        functools.partial(_attention_kernel, grid_rows=grid_rows),
        grid=(bsz, nt),
        in_specs=[_fm_spec(),
                  pl.BlockSpec((None, s, NA_W), lambda b, i: (b, 0, 0)),
                  pl.BlockSpec((None, nt, NA_W, TM), lambda b, i: (b, 0, 0, 0)),
                  pl.BlockSpec((None, None, NA_HEADS, n_keys, TM), bias_class)],
        out_specs=_fm_spec(),
        out_shape=jax.ShapeDtypeStruct((bsz, nt, NA_W, TM), BF16),
        scratch_shapes=[pltpu.VMEM((n_keys + CTX_LEN, TM), F32),
                        pltpu.VMEM((n_keys + CTX_LEN, TM), F32)],
        compiler_params=_cparams("parallel", "arbitrary"),
        name="attention",
    )(qt, k, vt, bias)


def _gelu_tanh(x):
    return 0.5 * x * (1.0 + jnp.tanh(math.sqrt(2.0 / math.pi) * (x + 0.044715 * (x * x * x))))


def _merge_kernel(*refs, n_tiles, first):
    c_ref = refs[1] if first else None
    x_ref = refs[0]
    (cg_ref, cprev_ref, cnext_ref, u_ref, yf_ref, yb_ref, o_ref, gate_ref, g1_ref, cw_ref, cout_ref,
     d_ref, glua_ref, glub_ref, naout_ref, wout_ref, out_ref) = refs[2:] if first else refs[1:]
    i = pl.program_id(0)

    def gated_input(ref):
        return ref[:, 2 * CONV_W:3 * CONV_W].astype(F32) * ref[:, 0:CONV_W].astype(F32)

    z = gated_input(cg_ref)
    has_prev = (i >= 2).astype(F32)
    has_next = jnp.logical_and(i != 0, i != n_tiles - 1).astype(F32)
    z_before = gated_input(cprev_ref)[BF16_ROWS - 1:BF16_ROWS] * has_prev
    z_after = gated_input(cnext_ref)[0:1] * has_next
    row = lax.broadcasted_iota(jnp.int32, z.shape, 0)
    z_prev = jnp.where(row == 0, z_before, pltpu.roll(z, 1, 0))
    z_next = jnp.where(row == TM - 1, z_after, pltpu.roll(z, TM - 1, 0))
    conv = z_prev * cw_ref[0:1, :] + z * cw_ref[1:2, :] + z_next * cw_ref[2:3, :]
    a_pre = cg_ref[:, CONV_W:2 * CONV_W].astype(F32) * conv
    ya = _dot(a_pre.astype(BF16), cout_ref[...])

    y = (yf_ref[...].astype(F32) + yb_ref[...].astype(F32)
         + d_ref[...] * u_ref[...].astype(F32))
    g = _gelu_tanh(y).astype(BF16)
    def twice_logistic(half_x):
        return jnp.tanh(half_x) + 1.0

    yb = _dot(g, glua_ref[...]) * twice_logistic(_dot(g, glub_ref[...]))

    yc = lax.dot_general(o_ref[...], naout_ref[...], (((0,), (0,)), ((), ())),
                         preferred_element_type=F32)

    mix2 = (twice_logistic(gate_ref[:, 0:D_MODEL].astype(F32)) * ya
            + twice_logistic(gate_ref[:, D_MODEL:2 * D_MODEL].astype(F32)) * yb
            + twice_logistic(gate_ref[:, 2 * D_MODEL:3 * D_MODEL].astype(F32)) * yc)
    out_ref[...] = _stream_tile(x_ref, c_ref) + g1_ref[...] * _dot(mix2.astype(BF16), wout_ref[...])


def _merge(stream, conv_g, u_tm, yf, yb, attn_o, gates, gate1, conv_w, conv_out, s5_d, glu_a, glu_b,
           na_out, w_out, layer):
    first = len(stream) == 2
    bsz, s, _ = conv_g.shape
    nt = s // TM
    halo_blocks = s // BF16_ROWS
    per_tile = TM // BF16_ROWS
    tok = lambda width: _tok_spec(width, batch_major=False)
    tm_spec = tok(SSM_W)
    prev_spec = pl.BlockSpec((None, BF16_ROWS, 3 * CONV_W),
                             lambda i, b: (b, jnp.maximum(i * per_tile - 1, 0), 0))
    next_spec = pl.BlockSpec((None, BF16_ROWS, 3 * CONV_W),
                             lambda i, b: (b, jnp.minimum((i + 1) * per_tile, halo_blocks - 1), 0))
    return pl.pallas_call(
        functools.partial(_merge_kernel, n_tiles=nt, first=first),
        grid=(nt, bsz),
        in_specs=_stream_specs(first) + [
            tok(3 * CONV_W), prev_spec, next_spec, tm_spec, tm_spec, tm_spec,
            _fm_spec(batch_major=False), tok(3 * D_MODEL), _mod_spec(batch_major=False),
            _const_spec((3, CONV_W)), _layer_spec(conv_out, layer), _const_spec((1, SSM_W)),
            _layer_spec(glu_a, layer), _layer_spec(glu_b, layer),
            _layer_spec(na_out, layer), _layer_spec(w_out, layer)],
        out_specs=tok(D_MODEL),
        out_shape=jax.ShapeDtypeStruct((bsz, s, D_MODEL), F32),
        compiler_params=_cparams("parallel", "arbitrary"),
        name="merge",
    )(*stream, conv_g, conv_g, conv_g, u_tm, yf, yb, attn_o, gates, gate1, conv_w, conv_out, s5_d,
      glu_a, glu_b, na_out, w_out)


def _mlp_kernel(x_ref, sh_ref, sc_ref, gt_ref, g_ref, w1_ref, w2_ref, fg_ref, out_ref, *, final):
    x = x_ref[...]
    h = (_rms(x, g_ref[...]) * (1.0 + sc_ref[...]) + sh_ref[...]).astype(BF16)
    a = jnp.maximum(_dot(h, w1_ref[...]), 0.0)
    y = x + gt_ref[...] * _dot((a * a).astype(BF16), w2_ref[...])
    if final:
        y = _rms(y, fg_ref[...])
    out_ref[...] = y


def _mlp(xc, shift, scale, gate, g, w1, w2, final_g, final, layer):
    bsz, s, _ = xc.shape
    first = CTX_LEN // TM if final else 0
    nt = s // TM - first
    tok_in = pl.BlockSpec((None, TM, D_MODEL), lambda b, i: (b, i + first, 0))
    mod = pl.BlockSpec((None, None, 1, D_MODEL), lambda b, i: (b, jnp.minimum(i + first, 1), 0, 0))
    return pl.pallas_call(
        functools.partial(_mlp_kernel, final=final),
        grid=(bsz, nt),
        in_specs=[tok_in, mod, mod, mod, _const_spec((1, D_MODEL)),
                  _layer_spec(w1, layer), _layer_spec(w2, layer), _const_spec((1, D_MODEL))],
        out_specs=_tok_spec(D_MODEL),
        out_shape=jax.ShapeDtypeStruct((bsz, nt * TM, D_MODEL), F32),
        compiler_params=_cparams("parallel", "parallel"),
        name="mlp",
    )(xc, shift, scale, gate, g, w1, w2, final_g)


def kernel(x, c, ctx, c_ctx, w_mod, b_mod, norm1_g, w_in, conv_w, conv_out, s5_lam_re, s5_lam_im,
           s5_log_step, s5_b_re, s5_b_im, s5_c_re, s5_c_im, s5_d, s5_glu_a, s5_glu_b, na_rpb,
           na_out, w_out, norm2_g, mlp_w1, mlp_w2, final_norm_g):
    bsz, seq, _ = x.shape
    depth = w_mod.shape[0]
    assert ctx.shape[1] == CTX_LEN == TM and seq % TM == 0 and bsz == SUBLANES
    grid_rows = seq // GRID_W
    assert grid_rows >= NA_KROWS and grid_rows % NA_QROWS == 0
    s = CTX_LEN + seq

    cond_rows = 2 * SUBLANES
    cond = jnp.zeros((cond_rows, D_MODEL), F32).at[:bsz].set(c).at[bsz].set(c_ctx)
    mods = _adaln(cond, w_mod, b_mod).reshape(depth, cond_rows, N_MOD, D_MODEL)
    lat = mods[:, :bsz]
    cx = jnp.broadcast_to(mods[:, bsz:bsz + 1], lat.shape)
    mod = jnp.stack([cx, lat], axis=2)[:, :, :, :, None, :]

    s5_mats = _s5_matrices(s5_lam_re, s5_lam_im, s5_log_step, s5_b_re, s5_b_im, s5_c_re, s5_c_im)

    bf = lambda w: w.astype(BF16)
    gate_cols = (jnp.arange(IN_PROJ_W) >= IN_OFF[7])
    w_in_b = bf(w_in * jnp.where(gate_cols, 0.5, 1.0).astype(F32))
    conv_out_b, glu_a_b, glu_b_b = bf(conv_out), bf(s5_glu_a * 0.5), bf(s5_glu_b * 0.5)
    na_out_b, w_out_b, w1_b, w2_b = bf(na_out), bf(w_out * 0.5), bf(mlp_w1), bf(mlp_w2)
    w_qv_t = _qv_rows(w_in)

    na_bias = _na_bias_tables(na_rpb, grid_rows)

    stream = (x, ctx)
    row = lambda v: v.reshape(1, -1)
    for l in range(depth):
        m = mod[l]
        conv_g, u_tm, u_bm, qt, k, vt, gates = _in_proj(stream, m[:, :, 0], m[:, :, 1],
                                                        row(norm1_g[l]), w_in_b, w_qv_t, l)
        yf, yb = _s5_scan(u_tm, *s5_mats, s, l)
        attn_o = _attention(qt, k, vt, na_bias, grid_rows, l)
        xc = _merge(stream, conv_g, u_bm, yf, yb, attn_o, gates, m[:, :, 2], conv_w[l],
                    conv_out_b, row(s5_d[l]), glu_a_b, glu_b_b, na_out_b, w_out_b, l)
        xc = _mlp(xc, m[:, :, 3], m[:, :, 4], m[:, :, 5], row(norm2_g[l]), w1_b, w2_b,
                  row(final_norm_g), final=(l == depth - 1), layer=l)
        stream = (xc,)
    return xc
```
